```python
import math
import jax, jax.numpy as jnp
from jax import lax
import numpy as np


D_MODEL = 1024
BATCH = 8
SEQ = 4096
DEPTH = 2

MEM_LEN = 256
CHUNK = 128
SSD_INNER = D_MODEL
SSD_HEAD_DIM = 64
SSD_HEADS = SSD_INNER // SSD_HEAD_DIM
SSD_GROUPS = 4
SSD_HPG = SSD_HEADS // SSD_GROUPS
SSD_STATE = 128
CONV_K = 4
SSD_CONV_DIM = SSD_INNER + 2 * SSD_GROUPS * SSD_STATE
GM_WIDTH = D_MODEL
GM_GROUPS = 8
GM_GROUP_DIM = GM_WIDTH // GM_GROUPS
XA_HEADS = 4
XA_HEAD_DIM = D_MODEL // XA_HEADS
N_EXPERTS = 32
TOP_K = 4
D_EXPERT = D_MODEL
SWIGLU_LIMIT = 7.0
SWIGLU_ALPHA = 1.702
DN_ALPHA = (2 * DEPTH) ** 0.25
DN_BETA = (8 * DEPTH) ** -0.25
EPS = 1e-5
OFF_Z = 0
OFF_XBC = OFF_Z + SSD_INNER
OFF_DT = OFF_XBC + SSD_CONV_DIM
OFF_U = OFF_DT + SSD_HEADS
OFF_V = OFF_U + GM_WIDTH
OFF_G = OFF_V + GM_WIDTH
IN_COLS = OFF_G + 2 * D_MODEL

kernel_name = 'hybrid_ssd_gmlp_xattn_moe_deepnorm'


def layer_norm(x, g, b):
    xf = x.astype(jnp.float32)
    mu = jnp.mean(xf, axis=-1, keepdims=True)
    var = jnp.mean(jnp.square(xf - mu), axis=-1, keepdims=True)
    return ((xf - mu) * lax.rsqrt(var + EPS) * g + b).astype(x.dtype)


def causal_dwconv(x, w, b):
    s = x.shape[1]
    xp = jnp.pad(x, ((0, 0), (CONV_K - 1, 0), (0, 0)))
    out = xp[:, 0:s] * w[0]
    for k in range(1, CONV_K):
        out = out + xp[:, k:k + s] * w[k]
    return out + b


def ssd_chunked(xh, dt, a, bm, cm):
    b, s = xh.shape[0], xh.shape[1]
    c = s // CHUNK
    f32 = jnp.float32
    X = (xh.astype(f32) * dt[..., None]).reshape(b, c, CHUNK, SSD_GROUPS, SSD_HPG, SSD_HEAD_DIM)
    dA = (dt * a).reshape(b, c, CHUNK, SSD_GROUPS, SSD_HPG).transpose(0, 1, 3, 4, 2)
    a_cs = jnp.cumsum(dA, axis=-1)
    Bc = bm.astype(f32).reshape(b, c, CHUNK, SSD_GROUPS, SSD_STATE)
    Cc = cm.astype(f32).reshape(b, c, CHUNK, SSD_GROUPS, SSD_STATE)
    mask = jnp.tril(jnp.ones((CHUNK, CHUNK), dtype=bool))
    lmat = jnp.exp(jnp.where(mask, a_cs[..., :, None] - a_cs[..., None, :], -jnp.inf))
    cb = jnp.einsum('bclgn,bcsgn->bcgls', Cc, Bc)
    y_diag = jnp.einsum('bcgls,bcgrls,bcsgrp->bclgrp', cb, lmat, X)
    decay_states = jnp.exp(a_cs[..., -1:] - a_cs)
    states = jnp.einsum('bclgn,bcgrl,bclgrp->bcgrpn', Bc, decay_states, X)
    chunk_decay = jnp.exp(a_cs[..., -1])

    def step(h, inp):
        st, d = inp
        return d[..., None, None] * h + st, h

    h0 = jnp.zeros((b, SSD_GROUPS, SSD_HPG, SSD_HEAD_DIM, SSD_STATE), f32)
    _, prev = lax.scan(step, h0, (states.transpose(1, 0, 2, 3, 4, 5), chunk_decay.transpose(1, 0, 2, 3)))
    prev = prev.transpose(1, 0, 2, 3, 4, 5)
    y_off = jnp.einsum('bclgn,bcgrpn,bcgrl->bclgrp', Cc, prev, jnp.exp(a_cs))
    return (y_diag + y_off).reshape(b, s, SSD_HEADS, SSD_HEAD_DIM)


def hybrid_mixer(x, w_in, conv_w, conv_b, dt_bias, a_log, d_skip, ssd_norm_w,
                 gm_ln_g, gm_ln_b, w_sp, b_sp, p_ssd, p_gm, w_out):
    b, s, _ = x.shape
    f32 = jnp.float32
    zx = x @ w_in
    z = zx[..., OFF_Z:OFF_XBC]
    xbc = zx[..., OFF_XBC:OFF_DT]
    dt_raw = zx[..., OFF_DT:OFF_U]
    u = zx[..., OFF_U:OFF_V]
    v = zx[..., OFF_V:OFF_G]
    g_ssd = zx[..., OFF_G:OFF_G + D_MODEL]
    g_gm = zx[..., OFF_G + D_MODEL:]

    xbc = jax.nn.silu(causal_dwconv(xbc, conv_w, conv_b))
    gn = SSD_GROUPS * SSD_STATE
    xs = xbc[..., :SSD_INNER].reshape(b, s, SSD_HEADS, SSD_HEAD_DIM)
    bm = xbc[..., SSD_INNER:SSD_INNER + gn].reshape(b, s, SSD_GROUPS, SSD_STATE)
    cm = xbc[..., SSD_INNER + gn:].reshape(b, s, SSD_GROUPS, SSD_STATE)
    dt = jax.nn.softplus(dt_raw.astype(f32) + dt_bias.astype(f32))
    a = -jnp.exp(a_log.astype(f32))
    y = ssd_chunked(xs, dt, a, bm, cm) + d_skip.astype(f32)[:, None] * xs.astype(f32)
    y = y.reshape(b, s, SSD_INNER) * jax.nn.silu(z.astype(f32))
    yg = y.reshape(b, s, SSD_GROUPS, SSD_INNER // SSD_GROUPS)
    yg = yg * lax.rsqrt(jnp.mean(jnp.square(yg), axis=-1, keepdims=True) + EPS)
    y_ssd = (yg.reshape(b, s, SSD_INNER) * ssd_norm_w).astype(x.dtype)

    u = jax.nn.gelu(u, approximate=False)
    v = layer_norm(jax.nn.gelu(v, approximate=False), gm_ln_g, gm_ln_b)
    vc = v.reshape(b, s // CHUNK, CHUNK, GM_GROUPS, GM_GROUP_DIM)
    mask = jnp.tril(jnp.ones((CHUNK, CHUNK), dtype=bool))
    w_causal = jnp.where(mask, w_sp, 0.0).astype(v.dtype)
    sv = jnp.einsum('gts,bcsgd->bctgd', w_causal, vc) + b_sp.T[:, :, None]
    y_gm = u * sv.reshape(b, s, GM_WIDTH)

    h = jax.nn.sigmoid(g_ssd) * (y_ssd @ p_ssd) + jax.nn.sigmoid(g_gm) * (y_gm @ p_gm)
    return h @ w_out


def cross_attn(x, mem, wq, wk, wv, wo):
    b, s, _ = x.shape
    m = mem.shape[1]
    q = (x @ wq).reshape(b, s, XA_HEADS, XA_HEAD_DIM)
    k = (mem @ wk).reshape(b, m, XA_HEADS, XA_HEAD_DIM)
    v = (mem @ wv).reshape(b, m, XA_HEADS, XA_HEAD_DIM)
    sc = jnp.einsum('bshd,bmhd->bhsm', q, k).astype(jnp.float32) * (XA_HEAD_DIM ** -0.5)
    p = jax.nn.softmax(sc, axis=-1).astype(x.dtype)
    o = jnp.einsum('bhsm,bmhd->bshd', p, v).reshape(b, s, D_MODEL)
    return o @ wo


def moe(x, w_router, b_router, w_gu, b_gu, w_down, b_down):
    shp = x.shape
    f32 = jnp.float32
    xt = x.reshape(-1, shp[-1])
    logits = (xt @ w_router + b_router).astype(f32)
    top_v, top_i = lax.top_k(logits, TOP_K)
    top_w = jax.nn.softmax(top_v, axis=-1)
    gates = jnp.sum(jax.nn.one_hot(top_i, N_EXPERTS, dtype=f32) * top_w[..., None], axis=1)

    def expert_step(acc, p):
        wgu, bgu, wd, bd, g = p
        hgu = xt @ wgu + bgu
        gate = jnp.minimum(hgu[:, :D_EXPERT], SWIGLU_LIMIT)
        up = jnp.clip(hgu[:, D_EXPERT:], -SWIGLU_LIMIT, SWIGLU_LIMIT)
        glu = gate * jax.nn.sigmoid(SWIGLU_ALPHA * gate)
        y = ((up + 1.0) * glu) @ wd + bd
        return acc + g[:, None] * y.astype(f32), None

    acc0 = jnp.zeros((xt.shape[0], shp[-1]), f32)
    acc, _ = lax.scan(expert_step, acc0, (w_gu, b_gu, w_down, b_down, gates.T))
    return acc.astype(x.dtype).reshape(shp)


def setup_inputs(seed: int = 0) -> dict:
    key = jax.random.key(seed)
    ks = jax.random.split(key, 40)
    f32 = jnp.float32

    def nrm(k, shape, scale):
        return jax.random.normal(k, shape, f32) * scale

    dsc = D_MODEL ** -0.5
    u_dt = jax.random.uniform(ks[8], (DEPTH, SSD_HEADS), f32)
    dt0 = jnp.exp(u_dt * (math.log(0.1) - math.log(0.001)) + math.log(0.001))
    dt0 = jnp.maximum(dt0, 1e-4)
    dt_bias = dt0 + jnp.log(-jnp.expm1(-dt0))
    a_log = jnp.log(jax.random.uniform(ks[9], (DEPTH, SSD_HEADS), f32, 1.0, 16.0))
    return {
        'x': nrm(ks[0], (BATCH, SEQ, D_MODEL), 1.0),
        'mem': nrm(ks[1], (BATCH, MEM_LEN, D_MODEL), 1.0),
        'ln0_g': 1.0 + nrm(ks[2], (D_MODEL,), 0.02),
        'ln0_b': nrm(ks[3], (D_MODEL,), 0.02),
        'w_in': nrm(ks[4], (DEPTH, D_MODEL, IN_COLS), dsc),
        'conv_w': nrm(ks[5], (DEPTH, CONV_K, SSD_CONV_DIM), CONV_K ** -0.5),
        'conv_b': nrm(ks[6], (DEPTH, SSD_CONV_DIM), 0.02),
        'dt_bias': dt_bias,
        'a_log': a_log,
        'd_skip': 1.0 + nrm(ks[10], (DEPTH, SSD_HEADS), 0.1),
        'ssd_norm_w': 1.0 + nrm(ks[11], (DEPTH, SSD_INNER), 0.02),
        'gm_ln_g': 1.0 + nrm(ks[12], (DEPTH, GM_WIDTH), 0.02),
        'gm_ln_b': nrm(ks[13], (DEPTH, GM_WIDTH), 0.02),
        'w_sp': nrm(ks[14], (DEPTH, GM_GROUPS, CHUNK, CHUNK), CHUNK ** -0.5),
        'b_sp': 1.0 + nrm(ks[15], (DEPTH, GM_GROUPS, CHUNK), 0.02),
        'p_ssd': nrm(ks[16], (DEPTH, SSD_INNER, D_MODEL), SSD_INNER ** -0.5),
        'p_gm': nrm(ks[17], (DEPTH, GM_WIDTH, D_MODEL), GM_WIDTH ** -0.5),
        'w_out': nrm(ks[18], (DEPTH, D_MODEL, D_MODEL), dsc * DN_BETA),
        'wq': nrm(ks[19], (DEPTH, D_MODEL, D_MODEL), dsc),
        'wk': nrm(ks[20], (DEPTH, D_MODEL, D_MODEL), dsc),
        'wv': nrm(ks[21], (DEPTH, D_MODEL, D_MODEL), dsc * DN_BETA),
        'wo': nrm(ks[22], (DEPTH, D_MODEL, D_MODEL), dsc * DN_BETA),
        'w_router': nrm(ks[23], (DEPTH, D_MODEL, N_EXPERTS), dsc),
        'b_router': nrm(ks[24], (DEPTH, N_EXPERTS), 0.01),
        'w_gu': nrm(ks[25], (DEPTH, N_EXPERTS, D_MODEL, 2 * D_EXPERT), dsc * DN_BETA),
        'b_gu': nrm(ks[26], (DEPTH, N_EXPERTS, 2 * D_EXPERT), 0.01),
        'w_down': nrm(ks[27], (DEPTH, N_EXPERTS, D_EXPERT, D_MODEL), D_EXPERT ** -0.5 * DN_BETA),
        'b_down': nrm(ks[28], (DEPTH, N_EXPERTS, D_MODEL), 0.01),
        'ln_g': 1.0 + nrm(ks[29], (DEPTH, 3, D_MODEL), 0.02),
        'ln_b': nrm(ks[30], (DEPTH, 3, D_MODEL), 0.02),
    }


def reference(x, mem, ln0_g, ln0_b, w_in, conv_w, conv_b, dt_bias, a_log, d_skip,
              ssd_norm_w, gm_ln_g, gm_ln_b, w_sp, b_sp, p_ssd, p_gm, w_out,
              wq, wk, wv, wo, w_router, b_router, w_gu, b_gu, w_down, b_down,
              ln_g, ln_b):
    x = layer_norm(x, ln0_g, ln0_b)
    for l in range(DEPTH):
        mix = hybrid_mixer(x, w_in[l], conv_w[l], conv_b[l], dt_bias[l], a_log[l], d_skip[l],
                           ssd_norm_w[l], gm_ln_g[l], gm_ln_b[l], w_sp[l], b_sp[l],
                           p_ssd[l], p_gm[l], w_out[l])
        x = layer_norm(DN_ALPHA * x + mix, ln_g[l, 0], ln_b[l, 0])
        xa = cross_attn(x, mem, wq[l], wk[l], wv[l], wo[l])
        x = layer_norm(DN_ALPHA * x + xa, ln_g[l, 1], ln_b[l, 1])
        ff = moe(x, w_router[l], b_router[l], w_gu[l], b_gu[l], w_down[l], b_down[l])
        x = layer_norm(DN_ALPHA * x + ff, ln_g[l, 2], ln_b[l, 2])
    return x
```

```python
import functools
import math

import jax
import jax.numpy as jnp
from jax import lax
from jax.experimental import pallas as pl
from jax.experimental.pallas import tpu as pltpu

F32 = jnp.float32
BF16 = jnp.bfloat16

D_MODEL = 1024
DEPTH = 2
CHUNK = 128
SSD_HEADS = 16
SSD_HEAD_DIM = 64
SSD_GROUPS = 4
SSD_HPG = SSD_HEADS // SSD_GROUPS
SSD_STATE = 128
SSD_GROUP_W = SSD_HPG * SSD_HEAD_DIM
CONV_K = 4
GM_GROUPS = 8
GM_GROUP_DIM = D_MODEL // GM_GROUPS
XA_HEADS = 4
XA_HEAD_DIM = D_MODEL // XA_HEADS
N_EXPERTS = 32
TOP_K = 4
TOP_K_SHIFT = 2
D_EXPERT = D_MODEL
SWIGLU_LIMIT = 7.0
SWIGLU_ALPHA = 1.702
DN_ALPHA = (2 * DEPTH) ** 0.25
EPS = 1e-5

LANES = 128
SUBLANES = 8
VMEM_LIMIT = 56 * 1024 * 1024

ROW_BLOCK = 512
MM_BLOCK_M = 1024
MM_BLOCK_N = 1024
ROUTE_BLOCK = 256
EXPERT_TILE = 256
NEG_BIG = -1e30


def _cparams(*sem):
    return pltpu.CompilerParams(dimension_semantics=sem, vmem_limit_bytes=VMEM_LIMIT)


def _layer_norm(x, g, b):
    mu = jnp.mean(x, axis=-1, keepdims=True)
    xc = x - mu
    var = jnp.mean(xc * xc, axis=-1, keepdims=True)
    return xc * lax.rsqrt(var + EPS) * g + b


def _dot(a, b):
    return jnp.dot(a, b, preferred_element_type=F32)


def _dot_nt(a, b):
    return lax.dot_general(a, b, (((1,), (1,)), ((), ())), preferred_element_type=F32)


def _dot_tn(a, b):
    return lax.dot_general(a, b, (((0,), (0,)), ((), ())), preferred_element_type=F32)


def _split3(v):
    hi = v.astype(BF16)
    r1 = v - hi.astype(F32)
    mid = r1.astype(BF16)
    lo = (r1 - mid.astype(F32)).astype(BF16)
    return hi, mid, lo


def _dot_exact_rhs(sel, v):
    hi, mid, lo = _split3(v)
    return _dot(sel, hi) + _dot(sel, mid) + _dot(sel, lo)


def _dot_exact_lhs(v, sel):
    hi, mid, lo = _split3(v)
    return _dot(hi, sel) + _dot(mid, sel) + _dot(lo, sel)


def _sigmoid(x):
    return 1.0 / (1.0 + jnp.exp(-x))


def _gelu(x):
    return 0.5 * x * (1.0 + lax.erf(x * math.sqrt(0.5)))


def _softplus(x):
    return jnp.maximum(x, 0.0) + jnp.log1p(jnp.exp(-jnp.abs(x)))


def _ln_kernel(x_ref, g_ref, b_ref, of_ref, ob_ref):
    y = _layer_norm(x_ref[...], g_ref[...], b_ref[...])
    of_ref[...] = y
    ob_ref[...] = y.astype(BF16)


def _entry_ln(x, g, b):
    t, d = x.shape
    row = pl.BlockSpec((ROW_BLOCK, d), lambda i: (i, 0))
    vec = pl.BlockSpec((1, d), lambda i: (0, 0))
    return pl.pallas_call(
        _ln_kernel,
        grid=(t // ROW_BLOCK,),
        in_specs=[row, vec, vec],
        out_specs=[row, row],
        out_shape=[jax.ShapeDtypeStruct((t, d), F32), jax.ShapeDtypeStruct((t, d), BF16)],
        compiler_params=_cparams("arbitrary"),
        name="entry_ln",
    )(x, g.reshape(1, d), b.reshape(1, d))


def _mm_kernel(a_ref, w_ref, o_ref):
    o_ref[...] = _dot(a_ref[...], w_ref[...]).astype(o_ref.dtype)


def _matmul(a, w, out_dtype, bm, bn):
    m, k = a.shape
    n = w.shape[1]
    return pl.pallas_call(
        _mm_kernel,
        grid=(n // bn, m // bm),
        in_specs=[pl.BlockSpec((bm, k), lambda j, i: (i, 0)),
                  pl.BlockSpec((k, bn), lambda j, i: (0, j))],
        out_specs=pl.BlockSpec((bm, bn), lambda j, i: (i, j)),
        out_shape=jax.ShapeDtypeStruct((m, n), out_dtype),
        compiler_params=_cparams("arbitrary", "arbitrary"),
        name="matmul",
    )(a, w)


def _conv_silu(raw, tail_ref, w, b):
    x = raw.astype(F32)
    tail = tail_ref[...]
    rid = lax.broadcasted_iota(jnp.int32, tail.shape, 0)
    acc = x * w[CONV_K - 1:CONV_K] + b
    for j in range(1, CONV_K):
        xj = pltpu.roll(x, j, 0)
        head = jnp.where(rid < j, pltpu.roll(tail, j, 0), xj[0:SUBLANES])
        xj = jnp.concatenate([head, xj[SUBLANES:]], axis=0)
        acc = acc + xj * w[CONV_K - 1 - j:CONV_K - j]
    tail_ref[...] = x[CHUNK - SUBLANES:CHUNK]
    return acc * _sigmoid(acc)


def _mixer_kernel(z_ref, xs_ref, bc_ref, u_ref, v_ref, xb_ref, wdt_ref, cw_ref, cb_ref, dtb_ref,
                  alog_ref, dskip_ref, nw_ref, lng_ref, lnb_ref, wsp_ref, bsp_ref, hexp_ref,
                  yssd_ref, ygm_ref, tailx_ref, tailbc_ref, state_ref):
    @pl.when(pl.program_id(1) == 0)
    def _():
        tailx_ref[...] = jnp.zeros_like(tailx_ref)
        tailbc_ref[...] = jnp.zeros_like(tailbc_ref)
        state_ref[...] = jnp.zeros_like(state_ref)

    row = lax.broadcasted_iota(jnp.int32, (CHUNK, CHUNK), 0)
    col = lax.broadcasted_iota(jnp.int32, (CHUNK, CHUNK), 1)
    causal = col <= row
    tri = jnp.where(causal, 1.0, 0.0).astype(BF16)

    cw = cw_ref[...]
    cb = cb_ref[...]
    xs = _conv_silu(xs_ref[...], tailx_ref, cw[:, :D_MODEL], cb[:, :D_MODEL])
    bc = _conv_silu(bc_ref[...], tailbc_ref, cw[:, D_MODEL:], cb[:, D_MODEL:])
    gn = SSD_GROUPS * SSD_STATE

    dt = _softplus(_dot(xb_ref[...], wdt_ref[...]) + dtb_ref[...])
    a = -jnp.exp(alog_ref[...])
    cs = _dot_exact_rhs(tri, dt * a)
    cs_t = cs.T
    hexp = hexp_ref[...]
    dt_x = _dot_exact_lhs(dt, hexp)
    cs_x = _dot_exact_lhs(cs, hexp)
    tot_x = cs_x[CHUNK - 1:CHUNK]
    xdt = xs * dt_x
    xdt_b = xdt.astype(BF16)
    xdec_b = (xdt * jnp.exp(tot_x - cs_x)).astype(BF16)
    seg = lax.shift_right_logical(lax.broadcasted_iota(jnp.int32, (CHUNK, SSD_GROUP_W), 1),
                                  int(math.log2(SSD_HEAD_DIM)))

    y_parts = []
    for g in range(SSD_GROUPS):
        b_g = bc[:, g * SSD_STATE:(g + 1) * SSD_STATE].astype(BF16)
        c_g = bc[:, gn + g * SSD_STATE:gn + (g + 1) * SSD_STATE].astype(BF16)
        cols = slice(g * SSD_GROUP_W, (g + 1) * SSD_GROUP_W)
        cb_g = _dot_nt(c_g, b_g)
        st = state_ref[:, cols]
        y_g = _dot(c_g, st.astype(BF16)) * jnp.exp(cs_x[:, cols])
        for hh in range(SSD_HPG):
            h = g * SSD_HPG + hh
            diff = cs[:, h:h + 1] - cs_t[h:h + 1, :]
            m_h = (cb_g * jnp.exp(jnp.where(causal, diff, -jnp.inf))).astype(BF16)
            y_g = y_g + jnp.where(seg == hh, _dot(m_h, xdt_b[:, cols]), 0.0)
        state_ref[:, cols] = st * jnp.exp(tot_x[:, cols]) + _dot_tn(b_g, xdec_b[:, cols])
        y_parts.append(y_g)
    y = jnp.concatenate(y_parts, axis=1) + dskip_ref[...] * xs

    z = z_ref[...].astype(F32)
    y = y * (z * _sigmoid(z))
    n_parts = []
    for g in range(SSD_GROUPS):
        yg = y[:, g * SSD_GROUP_W:(g + 1) * SSD_GROUP_W]
        n_parts.append(yg * lax.rsqrt(jnp.mean(yg * yg, axis=-1, keepdims=True) + EPS))
    yssd_ref[...] = (jnp.concatenate(n_parts, axis=1) * nw_ref[...]).astype(BF16)

    u = _gelu(u_ref[...].astype(F32))
    v = _layer_norm(_gelu(v_ref[...].astype(F32)), lng_ref[...], lnb_ref[...]).astype(BF16)
    sv_parts = []
    for g in range(GM_GROUPS):
        w_g = jnp.where(causal, wsp_ref[g], 0.0).astype(BF16)
        sv_parts.append(_dot(w_g, v[:, g * GM_GROUP_DIM:(g + 1) * GM_GROUP_DIM]))
    ygm_ref[...] = (u * (jnp.concatenate(sv_parts, axis=1) + bsp_ref[...])).astype(BF16)


def _mixer(zx, xb, p, batch, seq):
    t, d = xb.shape
    nchunk = seq // CHUNK

    def blk(j):
        return pl.BlockSpec((CHUNK, d), lambda b, c, j=j: (b * nchunk + c, j))

    def const(shape):
        return pl.BlockSpec(shape, lambda b, c: (0,) * len(shape))

    out = pl.BlockSpec((CHUNK, d), lambda b, c: (b * nchunk + c, 0))
    return pl.pallas_call(
        _mixer_kernel,
        grid=(batch, nchunk),
        in_specs=[blk(0), blk(1), blk(2), blk(3), blk(4),
                  pl.BlockSpec((CHUNK, d), lambda b, c: (b * nchunk + c, 0)),
                  const((d, LANES)), const((CONV_K, 2 * d)), const((1, 2 * d)), const((1, LANES)),
                  const((1, LANES)), const((1, d)), const((1, d)), const((1, d)), const((1, d)),
                  const((GM_GROUPS, CHUNK, CHUNK)), const((CHUNK, d)), const((LANES, d))],
        out_specs=[out, out],
        out_shape=[jax.ShapeDtypeStruct((t, d), BF16), jax.ShapeDtypeStruct((t, d), BF16)],
        scratch_shapes=[pltpu.VMEM((SUBLANES, d), F32), pltpu.VMEM((SUBLANES, d), F32),
                        pltpu.VMEM((SSD_STATE, d), F32)],
        compiler_params=_cparams("arbitrary", "arbitrary"),
        name="mixer",
    )(zx, zx, zx, zx, zx, xb, p["w_dt"], p["conv_w"], p["conv_b"], p["dt_bias"], p["a_log"],
      p["d_skip"], p["ssd_norm_w"], p["gm_ln_g"], p["gm_ln_b"], p["w_sp"], p["b_sp"], p["head_expand"])


def _merge_kernel(ys_ref, yg_ref, gs_ref, gg_ref, x_ref, ps_ref, pg_ref, wo_ref, g_ref, b_ref,
                  of_ref, ob_ref):
    h = (_sigmoid(gs_ref[...].astype(F32)) * _dot(ys_ref[...], ps_ref[...])
         + _sigmoid(gg_ref[...].astype(F32)) * _dot(yg_ref[...], pg_ref[...]))
    mix = _dot(h.astype(BF16), wo_ref[...])
    y = _layer_norm(DN_ALPHA * x_ref[...] + mix, g_ref[...], b_ref[...])
    of_ref[...] = y
    ob_ref[...] = y.astype(BF16)


def _merge(y_ssd, y_gm, zx, xf, p):
    t, d = xf.shape
    row = pl.BlockSpec((ROW_BLOCK, d), lambda i: (i, 0))
    mat = pl.BlockSpec((d, d), lambda i: (0, 0))
    vec = pl.BlockSpec((1, d), lambda i: (0, 0))
    return pl.pallas_call(
        _merge_kernel,
        grid=(t // ROW_BLOCK,),
        in_specs=[row, row, pl.BlockSpec((ROW_BLOCK, d), lambda i: (i, 5)),
                  pl.BlockSpec((ROW_BLOCK, d), lambda i: (i, 6)), row, mat, mat, mat, vec, vec],
        out_specs=[row, row],
        out_shape=[jax.ShapeDtypeStruct((t, d), F32), jax.ShapeDtypeStruct((t, d), BF16)],
        compiler_params=_cparams("arbitrary"),
        name="merge",
    )(y_ssd, y_gm, zx, zx, xf, p["p_ssd"], p["p_gm"], p["w_out"], p["ln_g0"], p["ln_b0"])


def _attn_kernel(xb_ref, xf_ref, kv_ref, wq_ref, wo_ref, g_ref, b_ref, of_ref, ob_ref):
    q = _dot(xb_ref[...], wq_ref[...]).astype(BF16)
    kv = kv_ref[...]
    outs = []
    for h in range(XA_HEADS):
        cols = slice(h * XA_HEAD_DIM, (h + 1) * XA_HEAD_DIM)
        s = _dot_nt(q[:, cols], kv[:, cols]) * (XA_HEAD_DIM ** -0.5)
        e = jnp.exp(s - jnp.max(s, axis=-1, keepdims=True))
        p = (e / jnp.sum(e, axis=-1, keepdims=True)).astype(BF16)
        outs.append(_dot(p, kv[:, D_MODEL + h * XA_HEAD_DIM:D_MODEL + (h + 1) * XA_HEAD_DIM]))
    o = jnp.concatenate(outs, axis=1).astype(BF16)
    y = _layer_norm(DN_ALPHA * xf_ref[...] + _dot(o, wo_ref[...]), g_ref[...], b_ref[...])
    of_ref[...] = y
    ob_ref[...] = y.astype(BF16)


def _cross_attn(xb, xf, kv, p, batch, seq, mem_len):
    t, d = xf.shape
    nblk = seq // ROW_BLOCK
    row = pl.BlockSpec((ROW_BLOCK, d), lambda b, i: (b * nblk + i, 0))
    mat = pl.BlockSpec((d, d), lambda b, i: (0, 0))
    vec = pl.BlockSpec((1, d), lambda b, i: (0, 0))
    return pl.pallas_call(
        _attn_kernel,
        grid=(batch, nblk),
        in_specs=[row, row, pl.BlockSpec((mem_len, 2 * d), lambda b, i: (b, 0)), mat, mat, vec, vec],
        out_specs=[row, row],
        out_shape=[jax.ShapeDtypeStruct((t, d), F32), jax.ShapeDtypeStruct((t, d), BF16)],
        compiler_params=_cparams("arbitrary", "arbitrary"),
        name="cross_attn",
    )(xb, xf, kv, p["wq"], p["wo"], p["ln_g1"], p["ln_b1"])


def _lanes_from_cols(cols, shape):
    lane = lax.broadcasted_iota(jnp.int32, shape, 1)
    out = jnp.zeros(shape, cols[0].dtype)
    for k, c in enumerate(cols):
        out = jnp.where(lane == k, c, out)
    return out


def _router_kernel(xb_ref, wr_ref, br_ref, eid_ref, gate_ref, rank_ref, cnt_ref, carry_ref):
    @pl.when(pl.program_id(0) == 0)
    def _():
        carry_ref[...] = jnp.zeros_like(carry_ref)

    m = xb_ref.shape[0]
    logits = _dot(xb_ref[...], wr_ref[...]) + br_ref[...]
    lane = lax.broadcasted_iota(jnp.int32, logits.shape, 1)
    tops, ids, hots = [], [], []
    for _ in range(TOP_K):
        top = jnp.max(logits, axis=-1, keepdims=True)
        idx = jnp.min(jnp.where(logits == top, lane, LANES), axis=-1, keepdims=True)
        hot = lane == idx
        logits = jnp.where(hot, -jnp.inf, logits)
        tops.append(top)
        ids.append(idx)
        hots.append(hot)
    es = [jnp.exp(v - tops[0]) for v in tops]
    den = es[0] + es[1] + es[2] + es[3]

    hot_all = jnp.zeros(logits.shape, F32)
    for hot in hots:
        hot_all = hot_all + jnp.where(hot, 1.0, 0.0)
    r = lax.broadcasted_iota(jnp.int32, (m, m), 0)
    c = lax.broadcasted_iota(jnp.int32, (m, m), 1)
    before = jnp.where(c < r, 1.0, 0.0).astype(BF16)
    base = carry_ref[...] + _dot(before, hot_all.astype(BF16))
    ranks = [jnp.sum(jnp.where(hot, base, 0.0), axis=-1, keepdims=True) for hot in hots]
    carry_ref[...] = carry_ref[...] + jnp.sum(hot_all, axis=0, keepdims=True)

    eid_ref[...] = _lanes_from_cols(ids, logits.shape)
    gate_ref[...] = _lanes_from_cols([e / den for e in es], logits.shape)
    rank_ref[...] = _lanes_from_cols(ranks, logits.shape).astype(jnp.int32)
    cnt_ref[...] = carry_ref[...].astype(jnp.int32)


def _router(xb, p):
    t, d = xb.shape
    row = pl.BlockSpec((ROUTE_BLOCK, LANES), lambda i: (i, 0))
    one = pl.BlockSpec((1, LANES), lambda i: (0, 0))
    return pl.pallas_call(
        _router_kernel,
        grid=(t // ROUTE_BLOCK,),
        in_specs=[pl.BlockSpec((ROUTE_BLOCK, d), lambda i: (i, 0)),
                  pl.BlockSpec((d, LANES), lambda i: (0, 0)), one],
        out_specs=[row, row, row, one],
        out_shape=[jax.ShapeDtypeStruct((t, LANES), jnp.int32), jax.ShapeDtypeStruct((t, LANES), F32),
                   jax.ShapeDtypeStruct((t, LANES), jnp.int32), jax.ShapeDtypeStruct((1, LANES), jnp.int32)],
        scratch_shapes=[pltpu.VMEM((1, LANES), F32)],
        compiler_params=_cparams("arbitrary"),
        name="router",
    )(xb, p["w_router"], p["b_router"])


def _dispatch_kernel(pos_ref, x_hbm, init_hbm, o_hbm, sem):
    del init_hbm
    base = pl.program_id(0) * ROUTE_BLOCK

    def copy(i):
        tok = lax.shift_right_logical(i, TOP_K_SHIFT)
        return pltpu.make_async_copy(x_hbm.at[pl.ds(base + tok, 1)],
                                     o_hbm.at[pl.ds(pos_ref[i], 1)], sem)

    def start(i, carry):
        copy(i).start()
        return carry

    def wait(i, carry):
        copy(i).wait()
        return carry

    lax.fori_loop(0, ROUTE_BLOCK * TOP_K, start, 0)
    lax.fori_loop(0, ROUTE_BLOCK * TOP_K, wait, 0)


def _dispatch(xf, pos_flat, n_slots):
    t, d = xf.shape
    return pl.pallas_call(
        _dispatch_kernel,
        grid=(t // ROUTE_BLOCK,),
        in_specs=[pl.BlockSpec((ROUTE_BLOCK * TOP_K,), lambda i: (i,), memory_space=pltpu.SMEM),
                  pl.BlockSpec(memory_space=pl.ANY), pl.BlockSpec(memory_space=pl.ANY)],
        out_specs=pl.BlockSpec(memory_space=pl.ANY),
        out_shape=jax.ShapeDtypeStruct((n_slots, d), F32),
        scratch_shapes=[pltpu.SemaphoreType.DMA(())],
        input_output_aliases={2: 0},
        compiler_params=_cparams("arbitrary"),
        name="moe_dispatch",
    )(pos_flat, xf, jnp.zeros((n_slots, d), F32))


def _expert_kernel(te_ref, nu_ref, x_ref, wgu_ref, bgu_ref, wd_ref, bd_ref, o_ref):
    i = pl.program_id(0)

    @pl.when(i < nu_ref[0])
    def _():
        hgu = _dot(x_ref[...].astype(BF16), wgu_ref[0]) + bgu_ref[0]
        gate = jnp.minimum(hgu[:, :D_EXPERT], SWIGLU_LIMIT)
        up = jnp.clip(hgu[:, D_EXPERT:], -SWIGLU_LIMIT, SWIGLU_LIMIT)
        glu = gate * _sigmoid(SWIGLU_ALPHA * gate)
        o_ref[...] = _dot(((up + 1.0) * glu).astype(BF16), wd_ref[0]) + bd_ref[0]

    @pl.when(i >= nu_ref[0])
    def _():
        o_ref[...] = jnp.zeros_like(o_ref)


def _experts(xs, tile_expert, n_used, p):
    n_slots, d = xs.shape
    n_tiles = n_slots // EXPERT_TILE
    grid_spec = pltpu.PrefetchScalarGridSpec(
        num_scalar_prefetch=2,
        grid=(n_tiles,),
        in_specs=[
            pl.BlockSpec((EXPERT_TILE, d), lambda i, te, nu: (jnp.minimum(i, nu[0] - 1), 0)),
            pl.BlockSpec((1, d, 2 * D_EXPERT), lambda i, te, nu: (te[i], 0, 0)),
            pl.BlockSpec((1, 1, 2 * D_EXPERT), lambda i, te, nu: (te[i], 0, 0)),
            pl.BlockSpec((1, D_EXPERT, d), lambda i, te, nu: (te[i], 0, 0)),
            pl.BlockSpec((1, 1, d), lambda i, te, nu: (te[i], 0, 0)),
        ],
        out_specs=pl.BlockSpec((EXPERT_TILE, d), lambda i, te, nu: (i, 0)),
    )
    return pl.pallas_call(
        _expert_kernel,
        grid_spec=grid_spec,
        out_shape=jax.ShapeDtypeStruct((n_slots, d), F32),
        compiler_params=_cparams("arbitrary"),
        name="moe_experts",
    )(tile_expert, n_used, xs, p["w_gu"], p["b_gu"], p["w_down"], p["b_down"])


def _combine_kernel(pos_ref, y_hbm, gate_ref, x_ref, g_ref, b_ref, of_ref, ob_ref, buf, sem):
    def copy(i):
        tok = lax.shift_right_logical(i, TOP_K_SHIFT)
        k = jnp.bitwise_and(i, TOP_K - 1)
        return pltpu.make_async_copy(y_hbm.at[pl.ds(pos_ref[i], 1)],
                                     buf.at[k, pl.ds(tok, 1)], sem)

    def start(i, carry):
        copy(i).start()
        return carry

    def wait(i, carry):
        copy(i).wait()
        return carry

    lax.fori_loop(0, ROUTE_BLOCK * TOP_K, start, 0)
    lax.fori_loop(0, ROUTE_BLOCK * TOP_K, wait, 0)

    gate = gate_ref[...]
    acc = jnp.zeros(x_ref.shape, F32)
    for k in range(TOP_K):
        acc = acc + gate[:, k:k + 1] * buf[k]
    y = _layer_norm(DN_ALPHA * x_ref[...] + acc, g_ref[...], b_ref[...])
    of_ref[...] = y
    ob_ref[...] = y.astype(BF16)


def _combine(ys, pos_flat, gates, xf, p):
    t, d = xf.shape
    row = pl.BlockSpec((ROUTE_BLOCK, d), lambda i: (i, 0))
    vec = pl.BlockSpec((1, d), lambda i: (0, 0))
    return pl.pallas_call(
        _combine_kernel,
        grid=(t // ROUTE_BLOCK,),
        in_specs=[pl.BlockSpec((ROUTE_BLOCK * TOP_K,), lambda i: (i,), memory_space=pltpu.SMEM),
                  pl.BlockSpec(memory_space=pl.ANY),
                  pl.BlockSpec((ROUTE_BLOCK, LANES), lambda i: (i, 0)), row, vec, vec],
        out_specs=[row, row],
        out_shape=[jax.ShapeDtypeStruct((t, d), F32), jax.ShapeDtypeStruct((t, d), BF16)],
        scratch_shapes=[pltpu.VMEM((TOP_K, ROUTE_BLOCK, d), F32), pltpu.SemaphoreType.DMA(())],
        compiler_params=_cparams("arbitrary"),
        name="moe_combine",
    )(pos_flat, ys, gates, xf, p["ln_g2"], p["ln_b2"])


def _moe(xb, xf, p):
    t, d = xf.shape
    eid, gates, rank, counts = _router(xb, p)
    counts = counts[0, :N_EXPERTS]
    padded = (counts + EXPERT_TILE - 1) // EXPERT_TILE * EXPERT_TILE
    ends = jnp.cumsum(padded)
    starts = ends - padded
    pos = (starts[eid[:, :TOP_K]] + rank[:, :TOP_K]).reshape(-1).astype(jnp.int32)
    n_slots = t * TOP_K + N_EXPERTS * EXPERT_TILE
    n_tiles = n_slots // EXPERT_TILE
    n_used = (ends[-1] // EXPERT_TILE).astype(jnp.int32)
    tile_start = jnp.minimum(jnp.arange(n_tiles, dtype=jnp.int32), n_used - 1) * EXPERT_TILE
    tile_expert = jnp.searchsorted(ends, tile_start, side="right").astype(jnp.int32)

    xs = _dispatch(xf, pos, n_slots)
    ys = _experts(xs, tile_expert, n_used.reshape(1), p)
    return _combine(ys, pos, gates, xf, p)


def _layer_params(l, w_in, conv_w, conv_b, dt_bias, a_log, d_skip, ssd_norm_w, gm_ln_g, gm_ln_b,
                  w_sp, b_sp, p_ssd, p_gm, w_out, wq, wk, wv, wo, w_router, b_router, w_gu, b_gu,
                  w_down, b_down, ln_g, ln_b):
    d = D_MODEL
    conv_dim = conv_w.shape[-1]
    off_dt = d + conv_dim
    off_u = off_dt + SSD_HEADS
    wl = w_in[l]
    pad_h = LANES - SSD_HEADS
    pad_e = LANES - N_EXPERTS
    head_of_channel = jnp.arange(d, dtype=jnp.int32) // SSD_HEAD_DIM
    return {
        "w_main": jnp.concatenate([wl[:, :off_dt], wl[:, off_u:]], axis=1).astype(BF16),
        "w_dt": jnp.pad(wl[:, off_dt:off_u], ((0, 0), (0, pad_h))).astype(BF16),
        "conv_w": conv_w[l], "conv_b": conv_b[l].reshape(1, -1),
        "dt_bias": jnp.pad(dt_bias[l], (0, pad_h)).reshape(1, LANES),
        "a_log": jnp.pad(a_log[l], (0, pad_h)).reshape(1, LANES),
        "d_skip": d_skip[l][head_of_channel].reshape(1, d),
        "ssd_norm_w": ssd_norm_w[l].reshape(1, d),
        "gm_ln_g": gm_ln_g[l].reshape(1, d), "gm_ln_b": gm_ln_b[l].reshape(1, d),
        "w_sp": w_sp[l],
        "b_sp": jnp.repeat(b_sp[l].T, GM_GROUP_DIM, axis=1),
        "head_expand": (jnp.arange(LANES, dtype=jnp.int32)[:, None] == head_of_channel[None, :]).astype(BF16),
        "p_ssd": p_ssd[l].astype(BF16), "p_gm": p_gm[l].astype(BF16), "w_out": w_out[l].astype(BF16),
        "wq": wq[l].astype(BF16), "wo": wo[l].astype(BF16),
        "w_kv": jnp.concatenate([wk[l], wv[l]], axis=1).astype(BF16),
        "w_router": jnp.pad(w_router[l], ((0, 0), (0, pad_e))).astype(BF16),
        "b_router": jnp.pad(b_router[l], (0, pad_e), constant_values=NEG_BIG).reshape(1, LANES),
        "w_gu": w_gu[l].astype(BF16), "b_gu": b_gu[l].reshape(N_EXPERTS, 1, -1),
        "w_down": w_down[l].astype(BF16), "b_down": b_down[l].reshape(N_EXPERTS, 1, -1),
        "ln_g0": ln_g[l, 0].reshape(1, d), "ln_b0": ln_b[l, 0].reshape(1, d),
        "ln_g1": ln_g[l, 1].reshape(1, d), "ln_b1": ln_b[l, 1].reshape(1, d),
        "ln_g2": ln_g[l, 2].reshape(1, d), "ln_b2": ln_b[l, 2].reshape(1, d),
    }


def kernel(x, mem, ln0_g, ln0_b, w_in, conv_w, conv_b, dt_bias, a_log, d_skip, ssd_norm_w, gm_ln_g, gm_ln_b, w_sp, b_sp, p_ssd, p_gm, w_out, wq, wk, wv, wo, w_router, b_router, w_gu, b_gu, w_down, b_down, ln_g, ln_b):
    batch, seq, d = x.shape
    mem_len = mem.shape[1]
    assert d == D_MODEL and seq % ROW_BLOCK == 0 and seq % CHUNK == 0
    t = batch * seq
    memb = mem.reshape(batch * mem_len, d).astype(BF16)
    xf, xb = _entry_ln(x.reshape(t, d), ln0_g, ln0_b)
    for l in range(w_in.shape[0]):
        p = _layer_params(l, w_in, conv_w, conv_b, dt_bias, a_log, d_skip, ssd_norm_w, gm_ln_g,
                          gm_ln_b, w_sp, b_sp, p_ssd, p_gm, w_out, wq, wk, wv, wo, w_router,
                          b_router, w_gu, b_gu, w_down, b_down, ln_g, ln_b)
        zx = _matmul(xb, p["w_main"], BF16, min(MM_BLOCK_M, t), MM_BLOCK_N)
        y_ssd, y_gm = _mixer(zx, xb, p, batch, seq)
        xf, xb = _merge(y_ssd, y_gm, zx, xf, p)
        kv = _matmul(memb, p["w_kv"], BF16, min(MM_BLOCK_M, batch * mem_len), MM_BLOCK_N)
        xf, xb = _cross_attn(xb, xf, kv, p, batch, seq, mem_len)
        xf, xb = _moe(xb, xf, p)
    return xf.reshape(batch, seq, d)
```

```python
import functools
import math

import jax
import jax.numpy as jnp
from jax import lax
from jax.experimental import pallas as pl
from jax.experimental.pallas import tpu as pltpu

F32 = jnp.float32
BF16 = jnp.bfloat16

D_MODEL = 1024
DEPTH = 2
CHUNK = 128
SSD_HEADS = 16
SSD_HEAD_DIM = 64
SSD_GROUPS = 4
SSD_HPG = SSD_HEADS // SSD_GROUPS
SSD_STATE = 128
SSD_GROUP_W = SSD_HPG * SSD_HEAD_DIM
CONV_K = 4
GM_GROUPS = 8
GM_GROUP_DIM = D_MODEL // GM_GROUPS
XA_HEADS = 4
XA_HEAD_DIM = D_MODEL // XA_HEADS
N_EXPERTS = 32
TOP_K = 4
TOP_K_SHIFT = 2
D_EXPERT = D_MODEL
SWIGLU_LIMIT = 7.0
SWIGLU_ALPHA = 1.702
DN_ALPHA = (2 * DEPTH) ** 0.25
EPS = 1e-5

LANES = 128
SUBLANES = 8
VMEM_LIMIT = 56 * 1024 * 1024

ROW_BLOCK = 512
MM_BLOCK_M = 1024
MM_BLOCK_N = 1024
ROUTE_BLOCK = 256
EXPERT_TILE = 256
NEG_BIG = -1e30
TILE_LINES = EXPERT_TILE * SUBLANES
IDX_BLOCK = 1024
TILES_PER_IDX_BLOCK = IDX_BLOCK // EXPERT_TILE
DMA_UNROLL = 8


def _cparams(*sem):
    return pltpu.CompilerParams(dimension_semantics=sem, vmem_limit_bytes=VMEM_LIMIT)


def _layer_norm(x, g, b):
    mu = jnp.mean(x, axis=-1, keepdims=True)
    xc = x - mu
    var = jnp.mean(xc * xc, axis=-1, keepdims=True)
    return xc * lax.rsqrt(var + EPS) * g + b


def _dot(a, b):
    return jnp.dot(a, b, preferred_element_type=F32)


def _dot_nt(a, b):
    return lax.dot_general(a, b, (((1,), (1,)), ((), ())), preferred_element_type=F32)


def _dot_tn(a, b):
    return lax.dot_general(a, b, (((0,), (0,)), ((), ())), preferred_element_type=F32)


def _split3(v):
    hi = v.astype(BF16)
    r1 = v - hi.astype(F32)
    mid = r1.astype(BF16)
    lo = (r1 - mid.astype(F32)).astype(BF16)
    return hi, mid, lo


def _dot_exact_rhs(sel, v):
    hi, mid, lo = _split3(v)
    return _dot(sel, hi) + _dot(sel, mid) + _dot(sel, lo)


def _dot_exact_lhs(v, sel):
    hi, mid, lo = _split3(v)
    return _dot(hi, sel) + _dot(mid, sel) + _dot(lo, sel)


def _sigmoid(x):
    return 1.0 / (1.0 + jnp.exp(-x))


def _gelu(x):
    return 0.5 * x * (1.0 + lax.erf(x * math.sqrt(0.5)))


def _softplus(x):
    return jnp.maximum(x, 0.0) + jnp.log1p(jnp.exp(-jnp.abs(x)))


def _ln_kernel(x_ref, g_ref, b_ref, of_ref, ob_ref):
    y = _layer_norm(x_ref[...], g_ref[...], b_ref[...])
    of_ref[...] = y
    ob_ref[...] = y.astype(BF16)


def _entry_ln(x, g, b):
    t, d = x.shape
    row = pl.BlockSpec((ROW_BLOCK, d), lambda i: (i, 0))
    vec = pl.BlockSpec((1, d), lambda i: (0, 0))
    return pl.pallas_call(
        _ln_kernel,
        grid=(t // ROW_BLOCK,),
        in_specs=[row, vec, vec],
        out_specs=[row, row],
        out_shape=[jax.ShapeDtypeStruct((t, d), F32), jax.ShapeDtypeStruct((t, d), BF16)],
        compiler_params=_cparams("arbitrary"),
        name="entry_ln",
    )(x, g.reshape(1, d), b.reshape(1, d))


def _mm_kernel(a_ref, w_ref, o_ref):
    o_ref[...] = _dot(a_ref[...], w_ref[...]).astype(o_ref.dtype)


def _matmul(a, w, out_dtype, bm, bn):
    m, k = a.shape
    n = w.shape[1]
    return pl.pallas_call(
        _mm_kernel,
        grid=(n // bn, m // bm),
        in_specs=[pl.BlockSpec((bm, k), lambda j, i: (i, 0)),
                  pl.BlockSpec((k, bn), lambda j, i: (0, j))],
        out_specs=pl.BlockSpec((bm, bn), lambda j, i: (i, j)),
        out_shape=jax.ShapeDtypeStruct((m, n), out_dtype),
        compiler_params=_cparams("arbitrary", "arbitrary"),
        name="matmul",
    )(a, w)


def _conv_silu(raw, tail_ref, w, b):
    x = raw.astype(F32)
    tail = tail_ref[...]
    rid = lax.broadcasted_iota(jnp.int32, tail.shape, 0)
    acc = x * w[CONV_K - 1:CONV_K] + b
    for j in range(1, CONV_K):
        xj = pltpu.roll(x, j, 0)
        head = jnp.where(rid < j, pltpu.roll(tail, j, 0), xj[0:SUBLANES])
        xj = jnp.concatenate([head, xj[SUBLANES:]], axis=0)
        acc = acc + xj * w[CONV_K - 1 - j:CONV_K - j]
    tail_ref[...] = x[CHUNK - SUBLANES:CHUNK]
    return acc * _sigmoid(acc)


def _mixer_kernel(z_ref, xs_ref, bc_ref, u_ref, v_ref, xb_ref, wdt_ref, cw_ref, cb_ref, dtb_ref,
                  alog_ref, dskip_ref, nw_ref, lng_ref, lnb_ref, wsp_ref, bsp_ref, hexp_ref,
                  yssd_ref, ygm_ref, tailx_ref, tailbc_ref, state_ref):
    @pl.when(pl.program_id(1) == 0)
    def _():
        tailx_ref[...] = jnp.zeros_like(tailx_ref)
        tailbc_ref[...] = jnp.zeros_like(tailbc_ref)
        state_ref[...] = jnp.zeros_like(state_ref)

    row = lax.broadcasted_iota(jnp.int32, (CHUNK, CHUNK), 0)
    col = lax.broadcasted_iota(jnp.int32, (CHUNK, CHUNK), 1)
    causal = col <= row
    tri = jnp.where(causal, 1.0, 0.0).astype(BF16)

    cw = cw_ref[...]
    cb = cb_ref[...]
    xs = _conv_silu(xs_ref[...], tailx_ref, cw[:, :D_MODEL], cb[:, :D_MODEL])
    bc = _conv_silu(bc_ref[...], tailbc_ref, cw[:, D_MODEL:], cb[:, D_MODEL:])
    gn = SSD_GROUPS * SSD_STATE

    dt = _softplus(_dot(xb_ref[...], wdt_ref[...]) + dtb_ref[...])
    a = -jnp.exp(alog_ref[...])
    cs = _dot_exact_rhs(tri, dt * a)
    cs_t = cs.T
    hexp = hexp_ref[...]
    dt_x = _dot_exact_lhs(dt, hexp)
    cs_x = _dot_exact_lhs(cs, hexp)
    tot_x = cs_x[CHUNK - 1:CHUNK]
    xdt = xs * dt_x
    xdt_b = xdt.astype(BF16)
    xdec_b = (xdt * jnp.exp(tot_x - cs_x)).astype(BF16)
    seg = lax.shift_right_logical(lax.broadcasted_iota(jnp.int32, (CHUNK, SSD_GROUP_W), 1),
                                  int(math.log2(SSD_HEAD_DIM)))

    y_parts = []
    for g in range(SSD_GROUPS):
        b_g = bc[:, g * SSD_STATE:(g + 1) * SSD_STATE].astype(BF16)
        c_g = bc[:, gn + g * SSD_STATE:gn + (g + 1) * SSD_STATE].astype(BF16)
        cols = slice(g * SSD_GROUP_W, (g + 1) * SSD_GROUP_W)
        cb_g = _dot_nt(c_g, b_g)
        st = state_ref[:, cols]
        y_g = _dot(c_g, st.astype(BF16)) * jnp.exp(cs_x[:, cols])
        for hh in range(SSD_HPG):
            h = g * SSD_HPG + hh
            diff = cs[:, h:h + 1] - cs_t[h:h + 1, :]
            m_h = (cb_g * jnp.exp(jnp.where(causal, diff, -jnp.inf))).astype(BF16)
            y_g = y_g + jnp.where(seg == hh, _dot(m_h, xdt_b[:, cols]), 0.0)
        state_ref[:, cols] = st * jnp.exp(tot_x[:, cols]) + _dot_tn(b_g, xdec_b[:, cols])
        y_parts.append(y_g)
    y = jnp.concatenate(y_parts, axis=1) + dskip_ref[...] * xs

    z = z_ref[...].astype(F32)
    y = y * (z * _sigmoid(z))
    n_parts = []
    for g in range(SSD_GROUPS):
        yg = y[:, g * SSD_GROUP_W:(g + 1) * SSD_GROUP_W]
        n_parts.append(yg * lax.rsqrt(jnp.mean(yg * yg, axis=-1, keepdims=True) + EPS))
    yssd_ref[...] = (jnp.concatenate(n_parts, axis=1) * nw_ref[...]).astype(BF16)

    u = _gelu(u_ref[...].astype(F32))
    v = _layer_norm(_gelu(v_ref[...].astype(F32)), lng_ref[...], lnb_ref[...]).astype(BF16)
    sv_parts = []
    for g in range(GM_GROUPS):
        w_g = jnp.where(causal, wsp_ref[g], 0.0).astype(BF16)
        sv_parts.append(_dot(w_g, v[:, g * GM_GROUP_DIM:(g + 1) * GM_GROUP_DIM]))
    ygm_ref[...] = (u * (jnp.concatenate(sv_parts, axis=1) + bsp_ref[...])).astype(BF16)


def _mixer(zx, xb, p, batch, seq):
    t, d = xb.shape
    nchunk = seq // CHUNK

    def blk(j):
        return pl.BlockSpec((CHUNK, d), lambda b, c, j=j: (b * nchunk + c, j))

    def const(shape):
        return pl.BlockSpec(shape, lambda b, c: (0,) * len(shape))

    out = pl.BlockSpec((CHUNK, d), lambda b, c: (b * nchunk + c, 0))
    return pl.pallas_call(
        _mixer_kernel,
        grid=(batch, nchunk),
        in_specs=[blk(0), blk(1), blk(2), blk(3), blk(4),
                  pl.BlockSpec((CHUNK, d), lambda b, c: (b * nchunk + c, 0)),
                  const((d, LANES)), const((CONV_K, 2 * d)), const((1, 2 * d)), const((1, LANES)),
                  const((1, LANES)), const((1, d)), const((1, d)), const((1, d)), const((1, d)),
                  const((GM_GROUPS, CHUNK, CHUNK)), const((CHUNK, d)), const((LANES, d))],
        out_specs=[out, out],
        out_shape=[jax.ShapeDtypeStruct((t, d), BF16), jax.ShapeDtypeStruct((t, d), BF16)],
        scratch_shapes=[pltpu.VMEM((SUBLANES, d), F32), pltpu.VMEM((SUBLANES, d), F32),
                        pltpu.VMEM((SSD_STATE, d), F32)],
        compiler_params=_cparams("arbitrary", "arbitrary"),
        name="mixer",
    )(zx, zx, zx, zx, zx, xb, p["w_dt"], p["conv_w"], p["conv_b"], p["dt_bias"], p["a_log"],
      p["d_skip"], p["ssd_norm_w"], p["gm_ln_g"], p["gm_ln_b"], p["w_sp"], p["b_sp"], p["head_expand"])


def _merge_kernel(ys_ref, yg_ref, gs_ref, gg_ref, x_ref, ps_ref, pg_ref, wo_ref, g_ref, b_ref,
                  of_ref, ob_ref):
    h = (_sigmoid(gs_ref[...].astype(F32)) * _dot(ys_ref[...], ps_ref[...])
         + _sigmoid(gg_ref[...].astype(F32)) * _dot(yg_ref[...], pg_ref[...]))
    mix = _dot(h.astype(BF16), wo_ref[...])
    y = _layer_norm(DN_ALPHA * x_ref[...] + mix, g_ref[...], b_ref[...])
    of_ref[...] = y
    ob_ref[...] = y.astype(BF16)


def _merge(y_ssd, y_gm, zx, xf, p):
    t, d = xf.shape
    row = pl.BlockSpec((ROW_BLOCK, d), lambda i: (i, 0))
    mat = pl.BlockSpec((d, d), lambda i: (0, 0))
    vec = pl.BlockSpec((1, d), lambda i: (0, 0))
    return pl.pallas_call(
        _merge_kernel,
        grid=(t // ROW_BLOCK,),
        in_specs=[row, row, pl.BlockSpec((ROW_BLOCK, d), lambda i: (i, 5)),
                  pl.BlockSpec((ROW_BLOCK, d), lambda i: (i, 6)), row, mat, mat, mat, vec, vec],
        out_specs=[row, row],
        out_shape=[jax.ShapeDtypeStruct((t, d), F32), jax.ShapeDtypeStruct((t, d), BF16)],
        compiler_params=_cparams("arbitrary"),
        name="merge",
    )(y_ssd, y_gm, zx, zx, xf, p["p_ssd"], p["p_gm"], p["w_out"], p["ln_g0"], p["ln_b0"])


def _attn_kernel(xb_ref, xf_ref, kv_ref, wq_ref, wo_ref, g_ref, b_ref, of_ref, ob_ref, og_ref):
    q = _dot(xb_ref[...], wq_ref[...]).astype(BF16)
    kv = kv_ref[...]
    outs = []
    for h in range(XA_HEADS):
        cols = slice(h * XA_HEAD_DIM, (h + 1) * XA_HEAD_DIM)
        s = _dot_nt(q[:, cols], kv[:, cols]) * (XA_HEAD_DIM ** -0.5)
        e = jnp.exp(s - jnp.max(s, axis=-1, keepdims=True))
        p = (e / jnp.sum(e, axis=-1, keepdims=True)).astype(BF16)
        outs.append(_dot(p, kv[:, D_MODEL + h * XA_HEAD_DIM:D_MODEL + (h + 1) * XA_HEAD_DIM]))
    o = jnp.concatenate(outs, axis=1).astype(BF16)
    y = _layer_norm(DN_ALPHA * xf_ref[...] + _dot(o, wo_ref[...]), g_ref[...], b_ref[...])
    of_ref[...] = y
    ob_ref[...] = y.astype(BF16)
    _to_token_tiles(og_ref, 0, y)


def _cross_attn(xb, xf, kv, p, batch, seq, mem_len):
    t, d = xf.shape
    nblk = seq // ROW_BLOCK
    row = pl.BlockSpec((ROW_BLOCK, d), lambda b, i: (b * nblk + i, 0))
    mat = pl.BlockSpec((d, d), lambda b, i: (0, 0))
    vec = pl.BlockSpec((1, d), lambda b, i: (0, 0))
    return pl.pallas_call(
        _attn_kernel,
        grid=(batch, nblk),
        in_specs=[row, row, pl.BlockSpec((mem_len, 2 * d), lambda b, i: (b, 0)), mat, mat, vec, vec],
        out_specs=[row, row, pl.BlockSpec((ROW_BLOCK * SUBLANES, LANES), lambda b, i: (b * nblk + i, 0))],
        out_shape=[jax.ShapeDtypeStruct((t, d), F32), jax.ShapeDtypeStruct((t, d), BF16),
                   jax.ShapeDtypeStruct((t * SUBLANES, LANES), F32)],
        compiler_params=_cparams("arbitrary", "arbitrary"),
        name="cross_attn",
    )(xb, xf, kv, p["wq"], p["wo"], p["ln_g1"], p["ln_b1"])


def _lanes_from_cols(cols, shape):
    lane = lax.broadcasted_iota(jnp.int32, shape, 1)
    out = jnp.zeros(shape, cols[0].dtype)
    for k, c in enumerate(cols):
        out = jnp.where(lane == k, c, out)
    return out


def _router_kernel(xb_ref, wr_ref, br_ref, eid_ref, gate_ref, rank_ref, cnt_ref, carry_ref):
    @pl.when(pl.program_id(0) == 0)
    def _():
        carry_ref[...] = jnp.zeros_like(carry_ref)

    m = xb_ref.shape[0]
    logits = _dot(xb_ref[...], wr_ref[...]) + br_ref[...]
    lane = lax.broadcasted_iota(jnp.int32, logits.shape, 1)
    tops, ids, hots = [], [], []
    for _ in range(TOP_K):
        top = jnp.max(logits, axis=-1, keepdims=True)
        idx = jnp.min(jnp.where(logits == top, lane, LANES), axis=-1, keepdims=True)
        hot = lane == idx
        logits = jnp.where(hot, -jnp.inf, logits)
        tops.append(top)
        ids.append(idx)
        hots.append(hot)
    es = [jnp.exp(v - tops[0]) for v in tops]
    den = es[0] + es[1] + es[2] + es[3]

    hot_all = jnp.zeros(logits.shape, F32)
    for hot in hots:
        hot_all = hot_all + jnp.where(hot, 1.0, 0.0)
    r = lax.broadcasted_iota(jnp.int32, (m, m), 0)
    c = lax.broadcasted_iota(jnp.int32, (m, m), 1)
    before = jnp.where(c < r, 1.0, 0.0).astype(BF16)
    base = carry_ref[...] + _dot(before, hot_all.astype(BF16))
    ranks = [jnp.sum(jnp.where(hot, base, 0.0), axis=-1, keepdims=True) for hot in hots]
    carry_ref[...] = carry_ref[...] + jnp.sum(hot_all, axis=0, keepdims=True)

    eid_ref[...] = _lanes_from_cols(ids, logits.shape)
    gate_ref[...] = _lanes_from_cols([e / den for e in es], logits.shape)
    rank_ref[...] = _lanes_from_cols(ranks, logits.shape).astype(jnp.int32)
    cnt_ref[...] = carry_ref[...].astype(jnp.int32)


def _router(xb, p):
    t, d = xb.shape
    row = pl.BlockSpec((ROUTE_BLOCK, LANES), lambda i: (i, 0))
    one = pl.BlockSpec((1, LANES), lambda i: (0, 0))
    return pl.pallas_call(
        _router_kernel,
        grid=(t // ROUTE_BLOCK,),
        in_specs=[pl.BlockSpec((ROUTE_BLOCK, d), lambda i: (i, 0)),
                  pl.BlockSpec((d, LANES), lambda i: (0, 0)), one],
        out_specs=[row, row, row, one],
        out_shape=[jax.ShapeDtypeStruct((t, LANES), jnp.int32), jax.ShapeDtypeStruct((t, LANES), F32),
                   jax.ShapeDtypeStruct((t, LANES), jnp.int32), jax.ShapeDtypeStruct((1, LANES), jnp.int32)],
        scratch_shapes=[pltpu.VMEM((1, LANES), F32)],
        compiler_params=_cparams("arbitrary"),
        name="router",
    )(xb, p["w_router"], p["b_router"])


def _to_token_tiles(ref, start, y):
    m = y.shape[0]
    for j in range(SUBLANES):
        ref[pl.ds(start + j, m, stride=SUBLANES), :] = y[:, j * LANES:(j + 1) * LANES]


def _from_token_tiles(ref, start, m):
    return jnp.concatenate([ref[pl.ds(start + j, m, stride=SUBLANES), :] for j in range(SUBLANES)], axis=1)


def _expert_kernel(te_ref, gcur_ref, gnxt_ref, scur_ref, xg_hbm, wgu_ref, bgu_ref, wd_ref, bd_ref,
                   y_hbm, gbuf, obuf, gsem, ssem):
    del te_ref
    i = pl.program_id(0)
    last = pl.num_programs(0) - 1
    slot = jnp.bitwise_and(i, 1)
    other = 1 - slot

    def start_rows(idx_ref, tile, src, dst, sem, gather):
        off = jnp.bitwise_and(tile, TILES_PER_IDX_BLOCK - 1) * EXPERT_TILE

        def body(c, carry):
            for u in range(DMA_UNROLL):
                r = c * DMA_UNROLL + u
                far = pl.ds(pl.multiple_of(idx_ref[off + r] * SUBLANES, SUBLANES), SUBLANES)
                near = pl.ds(pl.multiple_of(r * SUBLANES, SUBLANES), SUBLANES)
                if gather:
                    pltpu.make_async_copy(src.at[far], dst.at[near], sem).start(priority=u % 2)
                else:
                    pltpu.make_async_copy(src.at[near], dst.at[far], sem).start(priority=u % 2)
            return carry

        lax.fori_loop(0, EXPERT_TILE // DMA_UNROLL, body, 0)

    def wait_tile(hbm, buf, sem, gather):
        whole = hbm.at[pl.ds(0, TILE_LINES)]
        if gather:
            pltpu.make_async_copy(whole, buf, sem).wait()
        else:
            pltpu.make_async_copy(buf, whole, sem).wait()

    def half(buf, s):
        return buf.at[pl.ds(pl.multiple_of(s * TILE_LINES, TILE_LINES), TILE_LINES)]

    @pl.when(i == 0)
    def _():
        start_rows(gcur_ref, i, xg_hbm, half(gbuf, 0), gsem.at[0], True)

    wait_tile(xg_hbm, half(gbuf, slot), gsem.at[slot], True)
    start_rows(gnxt_ref, jnp.minimum(i + 1, last), xg_hbm, half(gbuf, other), gsem.at[other], True)

    x = _from_token_tiles(gbuf, slot * TILE_LINES, EXPERT_TILE).astype(BF16)
    hgu = _dot(x, wgu_ref[0]) + bgu_ref[0]
    gate = jnp.minimum(hgu[:, :D_EXPERT], SWIGLU_LIMIT)
    up = jnp.clip(hgu[:, D_EXPERT:], -SWIGLU_LIMIT, SWIGLU_LIMIT)
    glu = gate * _sigmoid(SWIGLU_ALPHA * gate)
    y = _dot(((up + 1.0) * glu).astype(BF16), wd_ref[0]) + bd_ref[0]
    _to_token_tiles(obuf, slot * TILE_LINES, y)

    @pl.when(i > 0)
    def _():
        wait_tile(y_hbm, half(obuf, other), ssem.at[other], False)

    start_rows(scur_ref, i, half(obuf, slot), y_hbm, ssem.at[slot], False)

    @pl.when(i == last)
    def _():
        wait_tile(xg_hbm, half(gbuf, other), gsem.at[other], True)
        wait_tile(y_hbm, half(obuf, slot), ssem.at[slot], False)


def _experts(xg, tile_expert, gather_tok, scatter_slot, n_out_slots, p):
    n_tiles = tile_expert.shape[0]
    d = D_MODEL

    def idx_spec(shift):
        return pl.BlockSpec(
            (IDX_BLOCK,), lambda i, te: (jnp.minimum(i + shift, n_tiles - 1) // TILES_PER_IDX_BLOCK,),
            memory_space=pltpu.SMEM)

    grid_spec = pltpu.PrefetchScalarGridSpec(
        num_scalar_prefetch=1,
        grid=(n_tiles,),
        in_specs=[
            idx_spec(0), idx_spec(1), idx_spec(0),
            pl.BlockSpec(memory_space=pl.ANY),
            pl.BlockSpec((1, d, 2 * D_EXPERT), lambda i, te: (te[i], 0, 0)),
            pl.BlockSpec((1, 1, 2 * D_EXPERT), lambda i, te: (te[i], 0, 0)),
            pl.BlockSpec((1, D_EXPERT, d), lambda i, te: (te[i], 0, 0)),
            pl.BlockSpec((1, 1, d), lambda i, te: (te[i], 0, 0)),
        ],
        out_specs=pl.BlockSpec(memory_space=pl.ANY),
        scratch_shapes=[pltpu.VMEM((2 * TILE_LINES, LANES), F32), pltpu.VMEM((2 * TILE_LINES, LANES), F32),
                        pltpu.SemaphoreType.DMA((2,)), pltpu.SemaphoreType.DMA((2,))],
    )
    return pl.pallas_call(
        _expert_kernel,
        grid_spec=grid_spec,
        out_shape=jax.ShapeDtypeStruct((n_out_slots * SUBLANES, LANES), F32),
        compiler_params=_cparams("arbitrary"),
        name="moe_experts",
    )(tile_expert, gather_tok, gather_tok, scatter_slot, xg, p["w_gu"], p["b_gu"], p["w_down"], p["b_down"])


def _combine_kernel(y0_ref, y1_ref, y2_ref, y3_ref, gate_ref, x_ref, g_ref, b_ref, of_ref, ob_ref):
    gate = gate_ref[...]
    m = x_ref.shape[0]
    acc = jnp.zeros(x_ref.shape, F32)
    for k, y_ref in enumerate((y0_ref, y1_ref, y2_ref, y3_ref)):
        acc = acc + gate[:, k:k + 1] * _from_token_tiles(y_ref, 0, m)
    y = _layer_norm(DN_ALPHA * x_ref[...] + acc, g_ref[...], b_ref[...])
    of_ref[...] = y
    ob_ref[...] = y.astype(BF16)


def _combine(ytk, gates, xf, p):
    t, d = xf.shape
    nblk = t // ROUTE_BLOCK
    row = pl.BlockSpec((ROUTE_BLOCK, d), lambda i: (i, 0))
    vec = pl.BlockSpec((1, d), lambda i: (0, 0))

    def yk(k):
        return pl.BlockSpec((ROUTE_BLOCK * SUBLANES, LANES), lambda i, k=k: (k * nblk + i, 0))

    return pl.pallas_call(
        _combine_kernel,
        grid=(nblk,),
        in_specs=[yk(0), yk(1), yk(2), yk(3), pl.BlockSpec((ROUTE_BLOCK, LANES), lambda i: (i, 0)), row, vec, vec],
        out_specs=[row, row],
        out_shape=[jax.ShapeDtypeStruct((t, d), F32), jax.ShapeDtypeStruct((t, d), BF16)],
        compiler_params=_cparams("arbitrary"),
        name="moe_combine",
    )(ytk, ytk, ytk, ytk, gates, xf, p["ln_g2"], p["ln_b2"])


def _moe(xb, xf, xg, p):
    t, d = xf.shape
    eid, gates, rank, counts = _router(xb, p)
    counts = counts[0, :N_EXPERTS]
    padded = (counts + EXPERT_TILE - 1) // EXPERT_TILE * EXPERT_TILE
    ends = jnp.cumsum(padded)
    starts = ends - padded
    pos = (starts[eid[:, :TOP_K]] + rank[:, :TOP_K]).reshape(-1)
    n_slots = t * TOP_K + N_EXPERTS * EXPERT_TILE
    n_tiles = n_slots // EXPERT_TILE
    assert n_tiles % TILES_PER_IDX_BLOCK == 0
    n_used = ends[-1] // EXPERT_TILE
    tile_start = jnp.minimum(jnp.arange(n_tiles, dtype=jnp.int32), n_used - 1) * EXPERT_TILE
    tile_expert = jnp.sum((ends[None, :] <= tile_start[:, None]).astype(jnp.int32), axis=1)

    pair = jnp.arange(t * TOP_K, dtype=jnp.int32)
    slot_pair = jnp.full((n_slots,), -1, jnp.int32).at[pos].set(pair, unique_indices=True)
    valid = slot_pair >= 0
    tok = lax.shift_right_logical(jnp.maximum(slot_pair, 0), TOP_K_SHIFT)
    k = jnp.bitwise_and(slot_pair, TOP_K - 1)
    dump = t * TOP_K + jnp.arange(n_slots, dtype=jnp.int32) % (2 * EXPERT_TILE)
    gather_tok = jnp.where(valid, tok, 0)
    scatter_slot = jnp.where(valid, k * t + tok, dump)

    ytk = _experts(xg, tile_expert, gather_tok, scatter_slot, t * TOP_K + 2 * EXPERT_TILE, p)
    return _combine(ytk, gates, xf, p)


def _layer_params(l, w_in, conv_w, conv_b, dt_bias, a_log, d_skip, ssd_norm_w, gm_ln_g, gm_ln_b,
                  w_sp, b_sp, p_ssd, p_gm, w_out, wq, wk, wv, wo, w_router, b_router, w_gu, b_gu,
                  w_down, b_down, ln_g, ln_b):
    d = D_MODEL
    conv_dim = conv_w.shape[-1]
    off_dt = d + conv_dim
    off_u = off_dt + SSD_HEADS
    wl = w_in[l]
    pad_h = LANES - SSD_HEADS
    pad_e = LANES - N_EXPERTS
    head_of_channel = jnp.arange(d, dtype=jnp.int32) // SSD_HEAD_DIM
    return {
        "w_main": jnp.concatenate([wl[:, :off_dt], wl[:, off_u:]], axis=1).astype(BF16),
        "w_dt": jnp.pad(wl[:, off_dt:off_u], ((0, 0), (0, pad_h))).astype(BF16),
        "conv_w": conv_w[l], "conv_b": conv_b[l].reshape(1, -1),
        "dt_bias": jnp.pad(dt_bias[l], (0, pad_h)).reshape(1, LANES),
        "a_log": jnp.pad(a_log[l], (0, pad_h)).reshape(1, LANES),
        "d_skip": d_skip[l][head_of_channel].reshape(1, d),
        "ssd_norm_w": ssd_norm_w[l].reshape(1, d),
        "gm_ln_g": gm_ln_g[l].reshape(1, d), "gm_ln_b": gm_ln_b[l].reshape(1, d),
        "w_sp": w_sp[l],
        "b_sp": jnp.repeat(b_sp[l].T, GM_GROUP_DIM, axis=1),
        "head_expand": (jnp.arange(LANES, dtype=jnp.int32)[:, None] == head_of_channel[None, :]).astype(BF16),
        "p_ssd": p_ssd[l].astype(BF16), "p_gm": p_gm[l].astype(BF16), "w_out": w_out[l].astype(BF16),
        "wq": wq[l].astype(BF16), "wo": wo[l].astype(BF16),
        "w_kv": jnp.concatenate([wk[l], wv[l]], axis=1).astype(BF16),
        "w_router": jnp.pad(w_router[l], ((0, 0), (0, pad_e))).astype(BF16),
        "b_router": jnp.pad(b_router[l], (0, pad_e), constant_values=NEG_BIG).reshape(1, LANES),
        "w_gu": w_gu[l].astype(BF16), "b_gu": b_gu[l].reshape(N_EXPERTS, 1, -1),
        "w_down": w_down[l].astype(BF16), "b_down": b_down[l].reshape(N_EXPERTS, 1, -1),
        "ln_g0": ln_g[l, 0].reshape(1, d), "ln_b0": ln_b[l, 0].reshape(1, d),
        "ln_g1": ln_g[l, 1].reshape(1, d), "ln_b1": ln_b[l, 1].reshape(1, d),
        "ln_g2": ln_g[l, 2].reshape(1, d), "ln_b2": ln_b[l, 2].reshape(1, d),
    }


def kernel(x, mem, ln0_g, ln0_b, w_in, conv_w, conv_b, dt_bias, a_log, d_skip, ssd_norm_w, gm_ln_g, gm_ln_b, w_sp, b_sp, p_ssd, p_gm, w_out, wq, wk, wv, wo, w_router, b_router, w_gu, b_gu, w_down, b_down, ln_g, ln_b):
    batch, seq, d = x.shape
    mem_len = mem.shape[1]
    assert d == D_MODEL and seq % ROW_BLOCK == 0 and seq % CHUNK == 0
    t = batch * seq
    memb = mem.reshape(batch * mem_len, d).astype(BF16)
    xf, xb = _entry_ln(x.reshape(t, d), ln0_g, ln0_b)
    for l in range(w_in.shape[0]):
        p = _layer_params(l, w_in, conv_w, conv_b, dt_bias, a_log, d_skip, ssd_norm_w, gm_ln_g,
                          gm_ln_b, w_sp, b_sp, p_ssd, p_gm, w_out, wq, wk, wv, wo, w_router,
                          b_router, w_gu, b_gu, w_down, b_down, ln_g, ln_b)
        zx = _matmul(xb, p["w_main"], BF16, min(MM_BLOCK_M, t), MM_BLOCK_N)
        y_ssd, y_gm = _mixer(zx, xb, p, batch, seq)
        xf, xb = _merge(y_ssd, y_gm, zx, xf, p)
        kv = _matmul(memb, p["w_kv"], BF16, min(MM_BLOCK_M, batch * mem_len), MM_BLOCK_N)
        xf, xb, xg = _cross_attn(xb, xf, kv, p, batch, seq, mem_len)
        xf, xb = _moe(xb, xf, xg, p)
    return xf.reshape(batch, seq, d)
```

```python
import functools
import math

import jax
import jax.numpy as jnp
from jax import lax
from jax.experimental import pallas as pl
from jax.experimental.pallas import tpu as pltpu

F32 = jnp.float32
BF16 = jnp.bfloat16

D_MODEL = 1024
DEPTH = 2
CHUNK = 128
SSD_HEADS = 16
SSD_HEAD_DIM = 64
SSD_GROUPS = 4
SSD_HPG = SSD_HEADS // SSD_GROUPS
SSD_STATE = 128
SSD_GROUP_W = SSD_HPG * SSD_HEAD_DIM
CONV_K = 4
GM_GROUPS = 8
GM_GROUP_DIM = D_MODEL // GM_GROUPS
XA_HEADS = 4
XA_HEAD_DIM = D_MODEL // XA_HEADS
N_EXPERTS = 32
TOP_K = 4
TOP_K_SHIFT = 2
D_EXPERT = D_MODEL
SWIGLU_LIMIT = 7.0
SWIGLU_ALPHA = 1.702
DN_ALPHA = (2 * DEPTH) ** 0.25
EPS = 1e-5

LANES = 128
SUBLANES = 8
VMEM_LIMIT = 56 * 1024 * 1024

ROW_BLOCK = 512
MM_BLOCK_M = 1024
MM_BLOCK_N = 1024
ROUTE_BLOCK = 512
EXPERT_TILE = 256
NEG_BIG = -1e30
TILE_LINES = EXPERT_TILE * SUBLANES
PAIRS_PER_BLOCK = ROUTE_BLOCK * TOP_K
STAGE_LINES = PAIRS_PER_BLOCK * SUBLANES
ROW_UNROLL = 4


def _cparams(*sem):
    return pltpu.CompilerParams(dimension_semantics=sem, vmem_limit_bytes=VMEM_LIMIT)


def _layer_norm(x, g, b):
    mu = jnp.mean(x, axis=-1, keepdims=True)
    xc = x - mu
    var = jnp.mean(xc * xc, axis=-1, keepdims=True)
    return xc * lax.rsqrt(var + EPS) * g + b


def _dot(a, b):
    return jnp.dot(a, b, preferred_element_type=F32)


def _dot_nt(a, b):
    return lax.dot_general(a, b, (((1,), (1,)), ((), ())), preferred_element_type=F32)


def _dot_tn(a, b):
    return lax.dot_general(a, b, (((0,), (0,)), ((), ())), preferred_element_type=F32)


def _split3(v):
    hi = v.astype(BF16)
    r1 = v - hi.astype(F32)
    mid = r1.astype(BF16)
    lo = (r1 - mid.astype(F32)).astype(BF16)
    return hi, mid, lo


def _dot_exact_rhs(sel, v):
    hi, mid, lo = _split3(v)
    return _dot(sel, hi) + _dot(sel, mid) + _dot(sel, lo)


def _dot_exact_lhs(v, sel):
    hi, mid, lo = _split3(v)
    return _dot(hi, sel) + _dot(mid, sel) + _dot(lo, sel)


def _sigmoid(x):
    return 1.0 / (1.0 + jnp.exp(-x))


def _gelu(x):
    return 0.5 * x * (1.0 + lax.erf(x * math.sqrt(0.5)))


def _softplus(x):
    return jnp.maximum(x, 0.0) + jnp.log1p(jnp.exp(-jnp.abs(x)))


def _ln_kernel(x_ref, g_ref, b_ref, of_ref, ob_ref):
    y = _layer_norm(x_ref[...], g_ref[...], b_ref[...])
    of_ref[...] = y
    ob_ref[...] = y.astype(BF16)


def _entry_ln(x, g, b):
    t, d = x.shape
    row = pl.BlockSpec((ROW_BLOCK, d), lambda i: (i, 0))
    vec = pl.BlockSpec((1, d), lambda i: (0, 0))
    return pl.pallas_call(
        _ln_kernel,
        grid=(t // ROW_BLOCK,),
        in_specs=[row, vec, vec],
        out_specs=[row, row],
        out_shape=[jax.ShapeDtypeStruct((t, d), F32), jax.ShapeDtypeStruct((t, d), BF16)],
        compiler_params=_cparams("arbitrary"),
        name="entry_ln",
    )(x, g.reshape(1, d), b.reshape(1, d))


def _mm_kernel(a_ref, w_ref, o_ref):
    o_ref[...] = _dot(a_ref[...], w_ref[...]).astype(o_ref.dtype)


def _matmul(a, w, out_dtype, bm, bn):
    m, k = a.shape
    n = w.shape[1]
    return pl.pallas_call(
        _mm_kernel,
        grid=(n // bn, m // bm),
        in_specs=[pl.BlockSpec((bm, k), lambda j, i: (i, 0)),
                  pl.BlockSpec((k, bn), lambda j, i: (0, j))],
        out_specs=pl.BlockSpec((bm, bn), lambda j, i: (i, j)),
        out_shape=jax.ShapeDtypeStruct((m, n), out_dtype),
        compiler_params=_cparams("arbitrary", "arbitrary"),
        name="matmul",
    )(a, w)


def _conv_silu(raw, tail_ref, w, b):
    x = raw.astype(F32)
    tail = tail_ref[...]
    rid = lax.broadcasted_iota(jnp.int32, tail.shape, 0)
    acc = x * w[CONV_K - 1:CONV_K] + b
    for j in range(1, CONV_K):
        xj = pltpu.roll(x, j, 0)
        head = jnp.where(rid < j, pltpu.roll(tail, j, 0), xj[0:SUBLANES])
        xj = jnp.concatenate([head, xj[SUBLANES:]], axis=0)
        acc = acc + xj * w[CONV_K - 1 - j:CONV_K - j]
    tail_ref[...] = x[CHUNK - SUBLANES:CHUNK]
    return acc * _sigmoid(acc)


def _mixer_kernel(z_ref, xs_ref, bc_ref, u_ref, v_ref, xb_ref, wdt_ref, cw_ref, cb_ref, dtb_ref,
                  alog_ref, dskip_ref, nw_ref, lng_ref, lnb_ref, wsp_ref, bsp_ref, hexp_ref,
                  yssd_ref, ygm_ref, tailx_ref, tailbc_ref, state_ref):
    @pl.when(pl.program_id(1) == 0)
    def _():
        tailx_ref[...] = jnp.zeros_like(tailx_ref)
        tailbc_ref[...] = jnp.zeros_like(tailbc_ref)
        state_ref[...] = jnp.zeros_like(state_ref)

    row = lax.broadcasted_iota(jnp.int32, (CHUNK, CHUNK), 0)
    col = lax.broadcasted_iota(jnp.int32, (CHUNK, CHUNK), 1)
    causal = col <= row
    tri = jnp.where(causal, 1.0, 0.0).astype(BF16)

    cw = cw_ref[...]
    cb = cb_ref[...]
    xs = _conv_silu(xs_ref[...], tailx_ref, cw[:, :D_MODEL], cb[:, :D_MODEL])
    bc = _conv_silu(bc_ref[...], tailbc_ref, cw[:, D_MODEL:], cb[:, D_MODEL:])
    gn = SSD_GROUPS * SSD_STATE

    dt = _softplus(_dot(xb_ref[...], wdt_ref[...]) + dtb_ref[...])
    a = -jnp.exp(alog_ref[...])
    cs = _dot_exact_rhs(tri, dt * a)
    cs_t = cs.T
    hexp = hexp_ref[...]
    dt_x = _dot_exact_lhs(dt, hexp)
    cs_x = _dot_exact_lhs(cs, hexp)
    tot_x = cs_x[CHUNK - 1:CHUNK]
    xdt = xs * dt_x
    xdt_b = xdt.astype(BF16)
    xdec_b = (xdt * jnp.exp(tot_x - cs_x)).astype(BF16)
    seg = lax.shift_right_logical(lax.broadcasted_iota(jnp.int32, (CHUNK, SSD_GROUP_W), 1),
                                  int(math.log2(SSD_HEAD_DIM)))

    y_parts = []
    for g in range(SSD_GROUPS):
        b_g = bc[:, g * SSD_STATE:(g + 1) * SSD_STATE].astype(BF16)
        c_g = bc[:, gn + g * SSD_STATE:gn + (g + 1) * SSD_STATE].astype(BF16)
        cols = slice(g * SSD_GROUP_W, (g + 1) * SSD_GROUP_W)
        cb_g = _dot_nt(c_g, b_g)
        st = state_ref[:, cols]
        y_g = _dot(c_g, st.astype(BF16)) * jnp.exp(cs_x[:, cols])
        for hh in range(SSD_HPG):
            h = g * SSD_HPG + hh
            diff = cs[:, h:h + 1] - cs_t[h:h + 1, :]
            m_h = (cb_g * jnp.exp(jnp.where(causal, diff, -jnp.inf))).astype(BF16)
            y_g = y_g + jnp.where(seg == hh, _dot(m_h, xdt_b[:, cols]), 0.0)
        state_ref[:, cols] = st * jnp.exp(tot_x[:, cols]) + _dot_tn(b_g, xdec_b[:, cols])
        y_parts.append(y_g)
    y = jnp.concatenate(y_parts, axis=1) + dskip_ref[...] * xs

    z = z_ref[...].astype(F32)
    y = y * (z * _sigmoid(z))
    n_parts = []
    for g in range(SSD_GROUPS):
        yg = y[:, g * SSD_GROUP_W:(g + 1) * SSD_GROUP_W]
        n_parts.append(yg * lax.rsqrt(jnp.mean(yg * yg, axis=-1, keepdims=True) + EPS))
    yssd_ref[...] = (jnp.concatenate(n_parts, axis=1) * nw_ref[...]).astype(BF16)

    u = _gelu(u_ref[...].astype(F32))
    v = _layer_norm(_gelu(v_ref[...].astype(F32)), lng_ref[...], lnb_ref[...]).astype(BF16)
    sv_parts = []
    for g in range(GM_GROUPS):
        w_g = jnp.where(causal, wsp_ref[g], 0.0).astype(BF16)
        sv_parts.append(_dot(w_g, v[:, g * GM_GROUP_DIM:(g + 1) * GM_GROUP_DIM]))
    ygm_ref[...] = (u * (jnp.concatenate(sv_parts, axis=1) + bsp_ref[...])).astype(BF16)


def _mixer(zx, xb, p, batch, seq):
    t, d = xb.shape
    nchunk = seq // CHUNK

    def blk(j):
        return pl.BlockSpec((CHUNK, d), lambda b, c, j=j: (b * nchunk + c, j))

    def const(shape):
        return pl.BlockSpec(shape, lambda b, c: (0,) * len(shape))

    out = pl.BlockSpec((CHUNK, d), lambda b, c: (b * nchunk + c, 0))
    return pl.pallas_call(
        _mixer_kernel,
        grid=(batch, nchunk),
        in_specs=[blk(0), blk(1), blk(2), blk(3), blk(4),
                  pl.BlockSpec((CHUNK, d), lambda b, c: (b * nchunk + c, 0)),
                  const((d, LANES)), const((CONV_K, 2 * d)), const((1, 2 * d)), const((1, LANES)),
                  const((1, LANES)), const((1, d)), const((1, d)), const((1, d)), const((1, d)),
                  const((GM_GROUPS, CHUNK, CHUNK)), const((CHUNK, d)), const((LANES, d))],
        out_specs=[out, out],
        out_shape=[jax.ShapeDtypeStruct((t, d), BF16), jax.ShapeDtypeStruct((t, d), BF16)],
        scratch_shapes=[pltpu.VMEM((SUBLANES, d), F32), pltpu.VMEM((SUBLANES, d), F32),
                        pltpu.VMEM((SSD_STATE, d), F32)],
        compiler_params=_cparams("arbitrary", "arbitrary"),
        name="mixer",
    )(zx, zx, zx, zx, zx, xb, p["w_dt"], p["conv_w"], p["conv_b"], p["dt_bias"], p["a_log"],
      p["d_skip"], p["ssd_norm_w"], p["gm_ln_g"], p["gm_ln_b"], p["w_sp"], p["b_sp"], p["head_expand"])


def _merge_kernel(ys_ref, yg_ref, gs_ref, gg_ref, x_ref, ps_ref, pg_ref, wo_ref, g_ref, b_ref,
                  of_ref, ob_ref):
    h = (_sigmoid(gs_ref[...].astype(F32)) * _dot(ys_ref[...], ps_ref[...])
         + _sigmoid(gg_ref[...].astype(F32)) * _dot(yg_ref[...], pg_ref[...]))
    mix = _dot(h.astype(BF16), wo_ref[...])
    y = _layer_norm(DN_ALPHA * x_ref[...] + mix, g_ref[...], b_ref[...])
    of_ref[...] = y
    ob_ref[...] = y.astype(BF16)


def _merge(y_ssd, y_gm, zx, xf, p):
    t, d = xf.shape
    row = pl.BlockSpec((ROW_BLOCK, d), lambda i: (i, 0))
    mat = pl.BlockSpec((d, d), lambda i: (0, 0))
    vec = pl.BlockSpec((1, d), lambda i: (0, 0))
    return pl.pallas_call(
        _merge_kernel,
        grid=(t // ROW_BLOCK,),
        in_specs=[row, row, pl.BlockSpec((ROW_BLOCK, d), lambda i: (i, 5)),
                  pl.BlockSpec((ROW_BLOCK, d), lambda i: (i, 6)), row, mat, mat, mat, vec, vec],
        out_specs=[row, row],
        out_shape=[jax.ShapeDtypeStruct((t, d), F32), jax.ShapeDtypeStruct((t, d), BF16)],
        compiler_params=_cparams("arbitrary"),
        name="merge",
    )(y_ssd, y_gm, zx, zx, xf, p["p_ssd"], p["p_gm"], p["w_out"], p["ln_g0"], p["ln_b0"])


def _attn_kernel(xb_ref, xf_ref, kv_ref, wq_ref, wo_ref, g_ref, b_ref, of_ref, ob_ref, og_ref):
    q = _dot(xb_ref[...], wq_ref[...]).astype(BF16)
    kv = kv_ref[...]
    outs = []
    for h in range(XA_HEADS):
        cols = slice(h * XA_HEAD_DIM, (h + 1) * XA_HEAD_DIM)
        s = _dot_nt(q[:, cols], kv[:, cols]) * (XA_HEAD_DIM ** -0.5)
        e = jnp.exp(s - jnp.max(s, axis=-1, keepdims=True))
        p = (e / jnp.sum(e, axis=-1, keepdims=True)).astype(BF16)
        outs.append(_dot(p, kv[:, D_MODEL + h * XA_HEAD_DIM:D_MODEL + (h + 1) * XA_HEAD_DIM]))
    o = jnp.concatenate(outs, axis=1).astype(BF16)
    y = _layer_norm(DN_ALPHA * xf_ref[...] + _dot(o, wo_ref[...]), g_ref[...], b_ref[...])
    of_ref[...] = y
    ob_ref[...] = y.astype(BF16)
    _to_token_tiles(og_ref, 0, y)


def _cross_attn(xb, xf, kv, p, batch, seq, mem_len):
    t, d = xf.shape
    nblk = seq // ROW_BLOCK
    row = pl.BlockSpec((ROW_BLOCK, d), lambda b, i: (b * nblk + i, 0))
    mat = pl.BlockSpec((d, d), lambda b, i: (0, 0))
    vec = pl.BlockSpec((1, d), lambda b, i: (0, 0))
    return pl.pallas_call(
        _attn_kernel,
        grid=(batch, nblk),
        in_specs=[row, row, pl.BlockSpec((mem_len, 2 * d), lambda b, i: (b, 0)), mat, mat, vec, vec],
        out_specs=[row, row, pl.BlockSpec((ROW_BLOCK * SUBLANES, LANES), lambda b, i: (b * nblk + i, 0))],
        out_shape=[jax.ShapeDtypeStruct((t, d), F32), jax.ShapeDtypeStruct((t, d), BF16),
                   jax.ShapeDtypeStruct((t * SUBLANES, LANES), F32)],
        compiler_params=_cparams("arbitrary", "arbitrary"),
        name="cross_attn",
    )(xb, xf, kv, p["wq"], p["wo"], p["ln_g1"], p["ln_b1"])


def _lanes_from_cols(cols, shape):
    lane = lax.broadcasted_iota(jnp.int32, shape, 1)
    out = jnp.zeros(shape, cols[0].dtype)
    for k, c in enumerate(cols):
        out = jnp.where(lane == k, c, out)
    return out


def _router_kernel(xb_ref, wr_ref, br_ref, gate_ref, lpos_ref, bcnt_ref, bbase_ref, bstart_ref, carry_ref):
    @pl.when(pl.program_id(0) == 0)
    def _():
        carry_ref[...] = jnp.zeros_like(carry_ref)

    m = xb_ref.shape[0]
    logits = _dot(xb_ref[...], wr_ref[...]) + br_ref[...]
    lane = lax.broadcasted_iota(jnp.int32, logits.shape, 1)
    tops, ids, hots = [], [], []
    for _ in range(TOP_K):
        top = jnp.max(logits, axis=-1, keepdims=True)
        idx = jnp.min(jnp.where(logits == top, lane, LANES), axis=-1, keepdims=True)
        hot = lane == idx
        logits = jnp.where(hot, -jnp.inf, logits)
        tops.append(top)
        ids.append(idx)
        hots.append(hot)
    es = [jnp.exp(v - tops[0]) for v in tops]
    den = es[0] + es[1] + es[2] + es[3]

    hot_all = jnp.zeros(logits.shape, F32)
    for hot in hots:
        hot_all = hot_all + jnp.where(hot, 1.0, 0.0)
    r = lax.broadcasted_iota(jnp.int32, (m, m), 0)
    c = lax.broadcasted_iota(jnp.int32, (m, m), 1)
    before = jnp.where(c < r, 1.0, 0.0).astype(BF16)
    cnt = jnp.sum(hot_all, axis=0, keepdims=True)
    er = lax.broadcasted_iota(jnp.int32, (LANES, LANES), 0)
    ec = lax.broadcasted_iota(jnp.int32, (LANES, LANES), 1)
    lower_experts = jnp.where(er < ec, 1.0, 0.0).astype(BF16)
    lstart = _dot_exact_lhs(jnp.broadcast_to(cnt, (SUBLANES, LANES)), lower_experts)[0:1]
    local = lstart + _dot(before, hot_all.astype(BF16))
    lpos = [jnp.sum(jnp.where(hot, local, 0.0), axis=-1, keepdims=True) for hot in hots]

    gate_ref[...] = _lanes_from_cols([e / den for e in es], logits.shape)
    lpos_ref[...] = _lanes_from_cols(lpos, logits.shape).astype(jnp.int32)
    bcnt_ref[0] = cnt.astype(jnp.int32)
    bbase_ref[0] = carry_ref[...].astype(jnp.int32)
    bstart_ref[0] = lstart.astype(jnp.int32)
    carry_ref[...] = carry_ref[...] + cnt


def _router(xb, p):
    t, d = xb.shape
    nblk = t // ROUTE_BLOCK
    row = pl.BlockSpec((ROUTE_BLOCK, LANES), lambda i: (i, 0))
    one = pl.BlockSpec((1, LANES), lambda i: (0, 0))
    per_block = pl.BlockSpec((1, 1, LANES), lambda i: (i, 0, 0))
    per_block_shape = jax.ShapeDtypeStruct((nblk, 1, LANES), jnp.int32)
    return pl.pallas_call(
        _router_kernel,
        grid=(nblk,),
        in_specs=[pl.BlockSpec((ROUTE_BLOCK, d), lambda i: (i, 0)),
                  pl.BlockSpec((d, LANES), lambda i: (0, 0)), one],
        out_specs=[row, row, per_block, per_block, per_block],
        out_shape=[jax.ShapeDtypeStruct((t, LANES), F32), jax.ShapeDtypeStruct((t, LANES), jnp.int32),
                   per_block_shape, per_block_shape, per_block_shape],
        scratch_shapes=[pltpu.VMEM((1, LANES), F32)],
        compiler_params=_cparams("arbitrary"),
        name="router",
    )(xb, p["w_router"], p["b_router"])


def _to_token_tiles(ref, start, y):
    m = y.shape[0]
    for j in range(SUBLANES):
        ref[pl.ds(start + j, m, stride=SUBLANES), :] = y[:, j * LANES:(j + 1) * LANES]


def _from_token_tiles(ref, start, m):
    return jnp.concatenate([ref[pl.ds(start + j, m, stride=SUBLANES), :] for j in range(SUBLANES)], axis=1)


def _range_copies(n, near_ref, near_row, far_hbm, far_row, sem, to_far):
    p = ROUTE_BLOCK
    while p >= 1:
        done = jnp.bitwise_and(n, -2 * p)

        @pl.when(jnp.bitwise_and(n, p) != 0)
        def _(p=p, done=done):
            near = near_ref.at[pl.ds(pl.multiple_of((near_row + done) * SUBLANES, SUBLANES), p * SUBLANES)]
            far = far_hbm.at[pl.ds(pl.multiple_of((far_row + done) * SUBLANES, SUBLANES), p * SUBLANES)]
            if to_far:
                pltpu.make_async_copy(near, far, sem).start()
            else:
                pltpu.make_async_copy(far, near, sem).start()

        p //= 2


def _stage_half(stage_ref, s):
    return stage_ref.at[pl.ds(pl.multiple_of(s * STAGE_LINES, STAGE_LINES), STAGE_LINES)]


def _dispatch_kernel(cnt_ref, lstart_ref, gstart_ref, padstart_ref, lpos_ref, xg_ref, xs_hbm,
                     stage_ref, zero_ref, sem, zsem):
    b = pl.program_id(0)
    last = pl.num_programs(0) - 1
    slot = jnp.bitwise_and(b, 1)

    def wait_half(s):
        pltpu.make_async_copy(_stage_half(stage_ref, s), xs_hbm.at[pl.ds(0, STAGE_LINES)], sem.at[s]).wait()

    @pl.when(b == 0)
    def _():
        zero_ref[...] = jnp.zeros_like(zero_ref)

        def zfill(e, carry):
            dst = xs_hbm.at[pl.ds(pl.multiple_of(padstart_ref[e] * SUBLANES, SUBLANES), TILE_LINES)]
            copy = pltpu.make_async_copy(zero_ref, dst, zsem)
            copy.start()
            copy.wait()
            return carry

        lax.fori_loop(0, N_EXPERTS, zfill, 0)

        def tail_copy(i):
            dst = xs_hbm.at[pl.ds(pl.multiple_of(i * TILE_LINES, TILE_LINES), TILE_LINES)]
            return pltpu.make_async_copy(zero_ref, dst, zsem)

        def tail_start(i, carry):
            tail_copy(i).start()
            return carry

        def tail_wait(i, carry):
            tail_copy(i).wait()
            return carry

        n_all = xs_hbm.shape[0] // TILE_LINES
        lax.fori_loop(padstart_ref[N_EXPERTS], n_all, tail_start, 0)
        lax.fori_loop(padstart_ref[N_EXPERTS], n_all, tail_wait, 0)

    base = slot * STAGE_LINES

    def fill(c, carry):
        for u in range(ROW_UNROLL):
            t = c * ROW_UNROLL + u
            tile = xg_ref[pl.ds(pl.multiple_of(t * SUBLANES, SUBLANES), SUBLANES), :]
            for k in range(TOP_K):
                dst = base + lpos_ref[t * TOP_K + k] * SUBLANES
                stage_ref[pl.ds(pl.multiple_of(dst, SUBLANES), SUBLANES), :] = tile
        return carry

    lax.fori_loop(0, ROUTE_BLOCK // ROW_UNROLL, fill, 0)

    def ranges(e, carry):
        j = b * N_EXPERTS + e
        _range_copies(cnt_ref[j], stage_ref, slot * PAIRS_PER_BLOCK + lstart_ref[j], xs_hbm, gstart_ref[j],
                      sem.at[slot], True)
        return carry

    lax.fori_loop(0, N_EXPERTS, ranges, 0)

    @pl.when(b > 0)
    def _():
        wait_half(1 - slot)

    @pl.when(b == last)
    def _():
        wait_half(slot)


def _dispatch(xg, lpos_flat, cnt, lstart, gstart, padstart, n_slots):
    nblk = xg.shape[0] // (ROUTE_BLOCK * SUBLANES)
    grid_spec = pltpu.PrefetchScalarGridSpec(
        num_scalar_prefetch=4,
        grid=(nblk,),
        in_specs=[pl.BlockSpec((PAIRS_PER_BLOCK,), lambda b, *_: (b,), memory_space=pltpu.SMEM),
                  pl.BlockSpec((ROUTE_BLOCK * SUBLANES, LANES), lambda b, *_: (b, 0))],
        out_specs=pl.BlockSpec(memory_space=pl.ANY),
        scratch_shapes=[pltpu.VMEM((2 * STAGE_LINES, LANES), F32), pltpu.VMEM((TILE_LINES, LANES), F32),
                        pltpu.SemaphoreType.DMA((2,)), pltpu.SemaphoreType.DMA(())],
    )
    return pl.pallas_call(
        _dispatch_kernel,
        grid_spec=grid_spec,
        out_shape=jax.ShapeDtypeStruct((n_slots * SUBLANES, LANES), F32),
        compiler_params=_cparams("arbitrary"),
        name="moe_dispatch",
    )(cnt, lstart, gstart, padstart, lpos_flat, xg)


def _expert_kernel(te_ref, nu_ref, x_ref, wgu_ref, bgu_ref, wd_ref, bd_ref, o_ref):
    del te_ref
    i = pl.program_id(0)

    @pl.when(i < nu_ref[0])
    def _():
        x = _from_token_tiles(x_ref, 0, EXPERT_TILE).astype(BF16)
        hgu = _dot(x, wgu_ref[0]) + bgu_ref[0]
        gate = jnp.minimum(hgu[:, :D_EXPERT], SWIGLU_LIMIT)
        up = jnp.clip(hgu[:, D_EXPERT:], -SWIGLU_LIMIT, SWIGLU_LIMIT)
        glu = gate * _sigmoid(SWIGLU_ALPHA * gate)
        _to_token_tiles(o_ref, 0, _dot(((up + 1.0) * glu).astype(BF16), wd_ref[0]) + bd_ref[0])

    @pl.when(i >= nu_ref[0])
    def _():
        o_ref[...] = jnp.zeros_like(o_ref)


def _experts(xs, tile_expert, n_used, n_tiles, p):
    d = D_MODEL
    grid_spec = pltpu.PrefetchScalarGridSpec(
        num_scalar_prefetch=2,
        grid=(n_tiles,),
        in_specs=[
            pl.BlockSpec((TILE_LINES, LANES), lambda i, te, nu: (jnp.minimum(i, nu[0] - 1), 0)),
            pl.BlockSpec((1, d, 2 * D_EXPERT), lambda i, te, nu: (te[i], 0, 0)),
            pl.BlockSpec((1, 1, 2 * D_EXPERT), lambda i, te, nu: (te[i], 0, 0)),
            pl.BlockSpec((1, D_EXPERT, d), lambda i, te, nu: (te[i], 0, 0)),
            pl.BlockSpec((1, 1, d), lambda i, te, nu: (te[i], 0, 0)),
        ],
        out_specs=pl.BlockSpec((TILE_LINES, LANES), lambda i, te, nu: (i, 0)),
    )
    return pl.pallas_call(
        _expert_kernel,
        grid_spec=grid_spec,
        out_shape=jax.ShapeDtypeStruct((n_tiles * TILE_LINES, LANES), F32),
        compiler_params=_cparams("arbitrary"),
        name="moe_experts",
    )(tile_expert, n_used, xs, p["w_gu"], p["b_gu"], p["w_down"], p["b_down"])


def _combine_kernel(cnt_ref, lstart_ref, gstart_ref, lpos_ref, gate_ref, ys_hbm, x_ref, g_ref, b_ref,
                    of_ref, ob_ref, stage_ref, acc_ref, sem):
    b = pl.program_id(0)
    last = pl.num_programs(0) - 1
    slot = jnp.bitwise_and(b, 1)

    def fetch(blk, s):
        def ranges(e, carry):
            j = blk * N_EXPERTS + e
            _range_copies(cnt_ref[j], stage_ref, s * PAIRS_PER_BLOCK + lstart_ref[j], ys_hbm, gstart_ref[j],
                          sem.at[s], False)
            return carry

        lax.fori_loop(0, N_EXPERTS, ranges, 0)

    @pl.when(b == 0)
    def _():
        fetch(b, slot)

    @pl.when(b < last)
    def _():
        fetch(b + 1, 1 - slot)

    pltpu.make_async_copy(ys_hbm.at[pl.ds(0, STAGE_LINES)], _stage_half(stage_ref, slot), sem.at[slot]).wait()

    base = slot * STAGE_LINES

    def gather(c, carry):
        for u in range(ROW_UNROLL):
            t = c * ROW_UNROLL + u
            tile = jnp.zeros((SUBLANES, LANES), F32)
            for k in range(TOP_K):
                src = base + lpos_ref[t * TOP_K + k] * SUBLANES
                tile = tile + gate_ref[t * TOP_K + k] * stage_ref[pl.ds(pl.multiple_of(src, SUBLANES), SUBLANES), :]
            acc_ref[pl.ds(pl.multiple_of(t * SUBLANES, SUBLANES), SUBLANES), :] = tile
        return carry

    lax.fori_loop(0, ROUTE_BLOCK // ROW_UNROLL, gather, 0)

    y = _layer_norm(DN_ALPHA * x_ref[...] + _from_token_tiles(acc_ref, 0, ROUTE_BLOCK), g_ref[...], b_ref[...])
    of_ref[...] = y
    ob_ref[...] = y.astype(BF16)


def _combine(ys, lpos_flat, gate_flat, cnt, lstart, gstart, xf, p):
    t, d = xf.shape
    nblk = t // ROUTE_BLOCK
    row = pl.BlockSpec((ROUTE_BLOCK, d), lambda b, *_: (b, 0))
    vec = pl.BlockSpec((1, d), lambda b, *_: (0, 0))
    pairs = pl.BlockSpec((PAIRS_PER_BLOCK,), lambda b, *_: (b,), memory_space=pltpu.SMEM)
    grid_spec = pltpu.PrefetchScalarGridSpec(
        num_scalar_prefetch=3,
        grid=(nblk,),
        in_specs=[pairs, pairs, pl.BlockSpec(memory_space=pl.ANY), row, vec, vec],
        out_specs=[row, row],
        scratch_shapes=[pltpu.VMEM((2 * STAGE_LINES, LANES), F32),
                        pltpu.VMEM((ROUTE_BLOCK * SUBLANES, LANES), F32), pltpu.SemaphoreType.DMA((2,))],
    )
    return pl.pallas_call(
        _combine_kernel,
        grid_spec=grid_spec,
        out_shape=[jax.ShapeDtypeStruct((t, d), F32), jax.ShapeDtypeStruct((t, d), BF16)],
        compiler_params=_cparams("arbitrary"),
        name="moe_combine",
    )(cnt, lstart, gstart, lpos_flat, gate_flat, ys, xf, p["ln_g2"], p["ln_b2"])


def _moe(xb, xf, xg, p):
    t, d = xf.shape
    gates, lpos, bcnt, bbase, bstart = _router(xb, p)
    bcnt = bcnt[:, 0, :N_EXPERTS]
    counts = jnp.sum(bcnt, axis=0)
    padded = (counts + EXPERT_TILE - 1) // EXPERT_TILE * EXPERT_TILE
    ends = jnp.cumsum(padded)
    starts = ends - padded
    gstart = (starts[None, :] + bbase[:, 0, :N_EXPERTS]).reshape(-1)
    lstart = bstart[:, 0, :N_EXPERTS].reshape(-1)
    n_tiles = (t * TOP_K) // EXPERT_TILE + N_EXPERTS
    n_used = ends[-1] // EXPERT_TILE
    tile_start = jnp.minimum(jnp.arange(n_tiles, dtype=jnp.int32), n_used - 1) * EXPERT_TILE
    tile_expert = jnp.sum((ends[None, :] <= tile_start[:, None]).astype(jnp.int32), axis=1)
    lpos_flat = lpos[:, :TOP_K].reshape(-1)
    gate_flat = gates[:, :TOP_K].reshape(-1)

    fill_from = jnp.concatenate([starts + counts, n_used.reshape(1)])
    xs = _dispatch(xg, lpos_flat, bcnt.reshape(-1), lstart, gstart, fill_from, (n_tiles + 1) * EXPERT_TILE)
    ys = _experts(xs, tile_expert, n_used.reshape(1), n_tiles, p)
    return _combine(ys, lpos_flat, gate_flat, bcnt.reshape(-1), lstart, gstart, xf, p)


def _layer_params(l, w_in, conv_w, conv_b, dt_bias, a_log, d_skip, ssd_norm_w, gm_ln_g, gm_ln_b,
                  w_sp, b_sp, p_ssd, p_gm, w_out, wq, wk, wv, wo, w_router, b_router, w_gu, b_gu,
                  w_down, b_down, ln_g, ln_b):
    d = D_MODEL
    conv_dim = conv_w.shape[-1]
    off_dt = d + conv_dim
    off_u = off_dt + SSD_HEADS
    wl = w_in[l]
    pad_h = LANES - SSD_HEADS
    pad_e = LANES - N_EXPERTS
    head_of_channel = jnp.arange(d, dtype=jnp.int32) // SSD_HEAD_DIM
    return {
        "w_main": jnp.concatenate([wl[:, :off_dt], wl[:, off_u:]], axis=1).astype(BF16),
        "w_dt": jnp.pad(wl[:, off_dt:off_u], ((0, 0), (0, pad_h))).astype(BF16),
        "conv_w": conv_w[l], "conv_b": conv_b[l].reshape(1, -1),
        "dt_bias": jnp.pad(dt_bias[l], (0, pad_h)).reshape(1, LANES),
        "a_log": jnp.pad(a_log[l], (0, pad_h)).reshape(1, LANES),
        "d_skip": d_skip[l][head_of_channel].reshape(1, d),
        "ssd_norm_w": ssd_norm_w[l].reshape(1, d),
        "gm_ln_g": gm_ln_g[l].reshape(1, d), "gm_ln_b": gm_ln_b[l].reshape(1, d),
        "w_sp": w_sp[l],
        "b_sp": jnp.repeat(b_sp[l].T, GM_GROUP_DIM, axis=1),
        "head_expand": (jnp.arange(LANES, dtype=jnp.int32)[:, None] == head_of_channel[None, :]).astype(BF16),
        "p_ssd": p_ssd[l].astype(BF16), "p_gm": p_gm[l].astype(BF16), "w_out": w_out[l].astype(BF16),
        "wq": wq[l].astype(BF16), "wo": wo[l].astype(BF16),
        "w_kv": jnp.concatenate([wk[l], wv[l]], axis=1).astype(BF16),
        "w_router": jnp.pad(w_router[l], ((0, 0), (0, pad_e))).astype(BF16),
        "b_router": jnp.pad(b_router[l], (0, pad_e), constant_values=NEG_BIG).reshape(1, LANES),
        "w_gu": w_gu[l].astype(BF16), "b_gu": b_gu[l].reshape(N_EXPERTS, 1, -1),
        "w_down": w_down[l].astype(BF16), "b_down": b_down[l].reshape(N_EXPERTS, 1, -1),
        "ln_g0": ln_g[l, 0].reshape(1, d), "ln_b0": ln_b[l, 0].reshape(1, d),
        "ln_g1": ln_g[l, 1].reshape(1, d), "ln_b1": ln_b[l, 1].reshape(1, d),
        "ln_g2": ln_g[l, 2].reshape(1, d), "ln_b2": ln_b[l, 2].reshape(1, d),
    }


def kernel(x, mem, ln0_g, ln0_b, w_in, conv_w, conv_b, dt_bias, a_log, d_skip, ssd_norm_w, gm_ln_g, gm_ln_b, w_sp, b_sp, p_ssd, p_gm, w_out, wq, wk, wv, wo, w_router, b_router, w_gu, b_gu, w_down, b_down, ln_g, ln_b):
    batch, seq, d = x.shape
    mem_len = mem.shape[1]
    assert d == D_MODEL and seq % ROW_BLOCK == 0 and seq % CHUNK == 0
    t = batch * seq
    memb = mem.reshape(batch * mem_len, d).astype(BF16)
    xf, xb = _entry_ln(x.reshape(t, d), ln0_g, ln0_b)
    for l in range(w_in.shape[0]):
        p = _layer_params(l, w_in, conv_w, conv_b, dt_bias, a_log, d_skip, ssd_norm_w, gm_ln_g,
                          gm_ln_b, w_sp, b_sp, p_ssd, p_gm, w_out, wq, wk, wv, wo, w_router,
                          b_router, w_gu, b_gu, w_down, b_down, ln_g, ln_b)
        zx = _matmul(xb, p["w_main"], BF16, min(MM_BLOCK_M, t), MM_BLOCK_N)
        y_ssd, y_gm = _mixer(zx, xb, p, batch, seq)
        xf, xb = _merge(y_ssd, y_gm, zx, xf, p)
        kv = _matmul(memb, p["w_kv"], BF16, min(MM_BLOCK_M, batch * mem_len), MM_BLOCK_N)
        xf, xb, xg = _cross_attn(xb, xf, kv, p, batch, seq, mem_len)
        xf, xb = _moe(xb, xf, xg, p)
    return xf.reshape(batch, seq, d)
```

```python
import math

import jax
import jax.numpy as jnp
from jax import lax
from jax.experimental import pallas as pl
from jax.experimental.pallas import tpu as pltpu

F32 = jnp.float32
BF16 = jnp.bfloat16

D_MODEL = 1024
DEPTH = 2
CHUNK = 128
SSD_HEADS = 16
SSD_HEAD_DIM = 64
SSD_GROUPS = 4
SSD_HPG = SSD_HEADS // SSD_GROUPS
SSD_STATE = 128
SSD_GROUP_W = SSD_HPG * SSD_HEAD_DIM
CONV_K = 4
GM_GROUPS = 8
GM_GROUP_DIM = D_MODEL // GM_GROUPS
XA_HEADS = 4
XA_HEAD_DIM = D_MODEL // XA_HEADS
N_EXPERTS = 32
TOP_K = 4
D_EXPERT = D_MODEL
SWIGLU_LIMIT = 7.0
SWIGLU_ALPHA = 1.702
DN_ALPHA = (2 * DEPTH) ** 0.25
EPS = 1e-5

LANES = 128
SUBLANES = 8
VMEM_LIMIT = 56 * 1024 * 1024

ROW_BLOCK = 512
MM_BLOCK_M = 1024
MM_BLOCK_N = 1024
CAST_ROWS = 128
ROUTE_BLOCK = 512
EXPERT_TILE = 512
NEG_BIG = -1e30
TILE_LINES = EXPERT_TILE * SUBLANES
PAIRS_PER_BLOCK = ROUTE_BLOCK * TOP_K
ROW_UNROLL = 4


def _cparams(*sem):
    return pltpu.CompilerParams(dimension_semantics=sem, vmem_limit_bytes=VMEM_LIMIT)


def _layer_norm(x, g, b):
    mu = jnp.mean(x, axis=-1, keepdims=True)
    xc = x - mu
    var = jnp.mean(xc * xc, axis=-1, keepdims=True)
    return xc * lax.rsqrt(var + EPS) * g + b


def _dot(a, b):
    return jnp.dot(a, b, preferred_element_type=F32)


def _dot_nt(a, b):
    return lax.dot_general(a, b, (((1,), (1,)), ((), ())), preferred_element_type=F32)


def _dot_tn(a, b):
    return lax.dot_general(a, b, (((0,), (0,)), ((), ())), preferred_element_type=F32)


def _split3(v):
    hi = v.astype(BF16)
    r1 = v - hi.astype(F32)
    mid = r1.astype(BF16)
    lo = (r1 - mid.astype(F32)).astype(BF16)
    return hi, mid, lo


def _dot_exact_rhs(sel, v):
    hi, mid, lo = _split3(v)
    return _dot(sel, hi) + _dot(sel, mid) + _dot(sel, lo)


def _dot_exact_lhs(v, sel):
    hi, mid, lo = _split3(v)
    return _dot(hi, sel) + _dot(mid, sel) + _dot(lo, sel)


def _sigmoid(x):
    return 1.0 / (1.0 + jnp.exp(-x))


def _gelu(x):
    return 0.5 * x * (1.0 + lax.erf(x * math.sqrt(0.5)))


def _softplus(x):
    return jnp.maximum(x, 0.0) + jnp.log1p(jnp.exp(-jnp.abs(x)))


def _cast_weight(dst_ref, src_ref):
    def body(c, carry):
        rows = pl.ds(pl.multiple_of(c * CAST_ROWS, CAST_ROWS), CAST_ROWS)
        dst_ref[rows, :] = src_ref[rows, :].astype(BF16)
        return carry

    lax.fori_loop(0, src_ref.shape[0] // CAST_ROWS, body, 0)


def _layer_weight(l, k, n):
    return pl.BlockSpec((1, k, n), lambda *_: (l, 0, 0), pipeline_mode=pl.Buffered(1))


def _ln_kernel(x_ref, g_ref, b_ref, of_ref, ob_ref):
    y = _layer_norm(x_ref[...], g_ref[...], b_ref[...])
    of_ref[...] = y
    ob_ref[...] = y.astype(BF16)


def _entry_ln(x, g, b):
    t, d = x.shape
    row = pl.BlockSpec((ROW_BLOCK, d), lambda i: (i, 0))
    vec = pl.BlockSpec((1, d), lambda i: (0, 0))
    return pl.pallas_call(
        _ln_kernel,
        grid=(t // ROW_BLOCK,),
        in_specs=[row, vec, vec],
        out_specs=[row, row],
        out_shape=[jax.ShapeDtypeStruct((t, d), F32), jax.ShapeDtypeStruct((t, d), BF16)],
        compiler_params=_cparams("arbitrary"),
        name="entry_ln",
    )(x, g.reshape(1, d), b.reshape(1, d))


def _mm_kernel(a_ref, w_ref, o_ref, wb_ref):
    @pl.when(pl.program_id(1) == 0)
    def _():
        _cast_weight(wb_ref, w_ref.at[0])

    o_ref[...] = _dot(a_ref[...], wb_ref[...]).astype(o_ref.dtype)


def _matmul(a, w, l, n, bm):
    m, k = a.shape
    bn = MM_BLOCK_N
    return pl.pallas_call(
        _mm_kernel,
        grid=(n // bn, m // bm),
        in_specs=[pl.BlockSpec((bm, k), lambda j, i: (i, 0)),
                  pl.BlockSpec((1, k, bn), lambda j, i: (l, 0, j))],
        out_specs=pl.BlockSpec((bm, bn), lambda j, i: (i, j)),
        out_shape=jax.ShapeDtypeStruct((m, n), BF16),
        scratch_shapes=[pltpu.VMEM((k, bn), BF16)],
        compiler_params=_cparams("arbitrary", "arbitrary"),
        name="matmul",
    )(a, w)


def _conv_silu(raw, tail_ref, w, b):
    x = raw.astype(F32)
    tail = tail_ref[...]
    rid = lax.broadcasted_iota(jnp.int32, tail.shape, 0)
    acc = x * w[CONV_K - 1:CONV_K] + b
    for j in range(1, CONV_K):
        xj = pltpu.roll(x, j, 0)
        head = jnp.where(rid < j, pltpu.roll(tail, j, 0), xj[0:SUBLANES])
        xj = jnp.concatenate([head, xj[SUBLANES:]], axis=0)
        acc = acc + xj * w[CONV_K - 1 - j:CONV_K - j]
    tail_ref[...] = x[CHUNK - SUBLANES:CHUNK]
    return acc * _sigmoid(acc)


def _mixer_kernel(z_ref, xs_ref, bc_ref, u_ref, v_ref, xb_ref, wdt_ref, cw_ref, cb_ref, dtb_ref,
                  alog_ref, dskip_ref, nw_ref, lng_ref, lnb_ref, wsp_ref, bsp_ref, hexp_ref,
                  yssd_ref, ygm_ref, tailx_ref, tailbc_ref, state_ref):
    @pl.when(pl.program_id(1) == 0)
    def _():
        tailx_ref[...] = jnp.zeros_like(tailx_ref)
        tailbc_ref[...] = jnp.zeros_like(tailbc_ref)
        state_ref[...] = jnp.zeros_like(state_ref)

    row = lax.broadcasted_iota(jnp.int32, (CHUNK, CHUNK), 0)
    col = lax.broadcasted_iota(jnp.int32, (CHUNK, CHUNK), 1)
    causal = col <= row
    tri = jnp.where(causal, 1.0, 0.0).astype(BF16)

    cw = cw_ref[...]
    cb = cb_ref[...]
    xs = _conv_silu(xs_ref[...], tailx_ref, cw[:, :D_MODEL], cb[:, :D_MODEL])
    bc = _conv_silu(bc_ref[...], tailbc_ref, cw[:, D_MODEL:], cb[:, D_MODEL:])
    gn = SSD_GROUPS * SSD_STATE

    dt = _softplus(_dot(xb_ref[...], wdt_ref[...]) + dtb_ref[...])
    a = -jnp.exp(alog_ref[...])
    cs = _dot_exact_rhs(tri, dt * a)
    cs_t = cs.T
    hexp = hexp_ref[...]
    dt_x = _dot_exact_lhs(dt, hexp)
    cs_x = _dot_exact_lhs(cs, hexp)
    tot_x = cs_x[CHUNK - 1:CHUNK]
    xdt = xs * dt_x
    xdt_b = xdt.astype(BF16)
    xdec_b = (xdt * jnp.exp(tot_x - cs_x)).astype(BF16)
    seg = lax.shift_right_logical(lax.broadcasted_iota(jnp.int32, (CHUNK, SSD_GROUP_W), 1),
                                  int(math.log2(SSD_HEAD_DIM)))

    y_parts = []
    for g in range(SSD_GROUPS):
        b_g = bc[:, g * SSD_STATE:(g + 1) * SSD_STATE].astype(BF16)
        c_g = bc[:, gn + g * SSD_STATE:gn + (g + 1) * SSD_STATE].astype(BF16)
        cols = slice(g * SSD_GROUP_W, (g + 1) * SSD_GROUP_W)
        cb_g = _dot_nt(c_g, b_g)
        st = state_ref[:, cols]
        y_g = _dot(c_g, st.astype(BF16)) * jnp.exp(cs_x[:, cols])
        for hh in range(SSD_HPG):
            h = g * SSD_HPG + hh
            diff = cs[:, h:h + 1] - cs_t[h:h + 1, :]
            m_h = (cb_g * jnp.exp(jnp.where(causal, diff, -jnp.inf))).astype(BF16)
            y_g = y_g + jnp.where(seg == hh, _dot(m_h, xdt_b[:, cols]), 0.0)
        state_ref[:, cols] = st * jnp.exp(tot_x[:, cols]) + _dot_tn(b_g, xdec_b[:, cols])
        y_parts.append(y_g)
    y = jnp.concatenate(y_parts, axis=1) + dskip_ref[...] * xs

    z = z_ref[...].astype(F32)
    y = y * (z * _sigmoid(z))
    n_parts = []
    for g in range(SSD_GROUPS):
        yg = y[:, g * SSD_GROUP_W:(g + 1) * SSD_GROUP_W]
        n_parts.append(yg * lax.rsqrt(jnp.mean(yg * yg, axis=-1, keepdims=True) + EPS))
    yssd_ref[...] = (jnp.concatenate(n_parts, axis=1) * nw_ref[...]).astype(BF16)

    u = _gelu(u_ref[...].astype(F32))
    v = _layer_norm(_gelu(v_ref[...].astype(F32)), lng_ref[...], lnb_ref[...]).astype(BF16)
    sv_parts = []
    for g in range(GM_GROUPS):
        w_g = jnp.where(causal, wsp_ref[g], 0.0).astype(BF16)
        sv_parts.append(_dot(w_g, v[:, g * GM_GROUP_DIM:(g + 1) * GM_GROUP_DIM]))
    ygm_ref[...] = (u * (jnp.concatenate(sv_parts, axis=1) + bsp_ref[...])).astype(BF16)


def _mixer(zxa, zxb, xb, p, batch, seq):
    t, d = xb.shape
    nchunk = seq // CHUNK

    def blk(j):
        return pl.BlockSpec((CHUNK, d), lambda b, c, j=j: (b * nchunk + c, j))

    def const(shape):
        return pl.BlockSpec(shape, lambda b, c: (0,) * len(shape))

    out = pl.BlockSpec((CHUNK, d), lambda b, c: (b * nchunk + c, 0))
    return pl.pallas_call(
        _mixer_kernel,
        grid=(batch, nchunk),
        in_specs=[blk(0), blk(1), blk(2), blk(0), blk(1),
                  pl.BlockSpec((CHUNK, d), lambda b, c: (b * nchunk + c, 0)),
                  const((d, LANES)), const((CONV_K, 2 * d)), const((1, 2 * d)), const((1, LANES)),
                  const((1, LANES)), const((1, d)), const((1, d)), const((1, d)), const((1, d)),
                  const((GM_GROUPS, CHUNK, CHUNK)), const((CHUNK, d)), const((LANES, d))],
        out_specs=[out, out],
        out_shape=[jax.ShapeDtypeStruct((t, d), BF16), jax.ShapeDtypeStruct((t, d), BF16)],
        scratch_shapes=[pltpu.VMEM((SUBLANES, d), F32), pltpu.VMEM((SUBLANES, d), F32),
                        pltpu.VMEM((SSD_STATE, d), F32)],
        compiler_params=_cparams("arbitrary", "arbitrary"),
        name="mixer",
    )(zxa, zxa, zxa, zxb, zxb, xb, p["w_dt"], p["conv_w"], p["conv_b"], p["dt_bias"], p["a_log"],
      p["d_skip"], p["ssd_norm_w"], p["gm_ln_g"], p["gm_ln_b"], p["w_sp"], p["b_sp"], p["head_expand"])


def _merge_kernel(ys_ref, yg_ref, gs_ref, gg_ref, x_ref, ps_ref, pg_ref, wo_ref, g_ref, b_ref,
                  of_ref, ob_ref, psb_ref, pgb_ref, wob_ref):
    @pl.when(pl.program_id(0) == 0)
    def _():
        _cast_weight(psb_ref, ps_ref.at[0])
        _cast_weight(pgb_ref, pg_ref.at[0])
        _cast_weight(wob_ref, wo_ref.at[0])

    h = (_sigmoid(gs_ref[...].astype(F32)) * _dot(ys_ref[...], psb_ref[...])
         + _sigmoid(gg_ref[...].astype(F32)) * _dot(yg_ref[...], pgb_ref[...]))
    mix = _dot(h.astype(BF16), wob_ref[...])
    y = _layer_norm(DN_ALPHA * x_ref[...] + mix, g_ref[...], b_ref[...])
    of_ref[...] = y
    ob_ref[...] = y.astype(BF16)


def _merge(y_ssd, y_gm, zxb, xf, l, w, p):
    t, d = xf.shape
    row = pl.BlockSpec((ROW_BLOCK, d), lambda i: (i, 0))
    mat = _layer_weight(l, d, d)
    vec = pl.BlockSpec((1, d), lambda i: (0, 0))
    return pl.pallas_call(
        _merge_kernel,
        grid=(t // ROW_BLOCK,),
        in_specs=[row, row, pl.BlockSpec((ROW_BLOCK, d), lambda i: (i, 2)),
                  pl.BlockSpec((ROW_BLOCK, d), lambda i: (i, 3)), row, mat, mat, mat, vec, vec],
        out_specs=[row, row],
        out_shape=[jax.ShapeDtypeStruct((t, d), F32), jax.ShapeDtypeStruct((t, d), BF16)],
        scratch_shapes=[pltpu.VMEM((d, d), BF16)] * 3,
        compiler_params=_cparams("arbitrary"),
        name="merge",
    )(y_ssd, y_gm, zxb, zxb, xf, w["p_ssd"], w["p_gm"], w["w_out"], p["ln_g0"], p["ln_b0"])


def _to_token_tiles(ref, y):
    m = y.shape[0]
    for j in range(SUBLANES):
        ref[pl.ds(j, m, stride=SUBLANES), :] = y[:, j * LANES:(j + 1) * LANES]


def _from_token_tiles(ref, m):
    return jnp.concatenate([ref[pl.ds(j, m, stride=SUBLANES), :] for j in range(SUBLANES)], axis=1)


def _attn_kernel(xb_ref, xf_ref, kv_ref, wq_ref, wo_ref, g_ref, b_ref, of_ref, ob_ref, og_ref,
                 wqb_ref, wob_ref):
    @pl.when(jnp.logical_and(pl.program_id(0) == 0, pl.program_id(1) == 0))
    def _():
        _cast_weight(wqb_ref, wq_ref.at[0])
        _cast_weight(wob_ref, wo_ref.at[0])

    q = _dot(xb_ref[...], wqb_ref[...]).astype(BF16)
    kv = kv_ref[...]
    outs = []
    for h in range(XA_HEADS):
        cols = slice(h * XA_HEAD_DIM, (h + 1) * XA_HEAD_DIM)
        s = _dot_nt(q[:, cols], kv[:, cols]) * (XA_HEAD_DIM ** -0.5)
        e = jnp.exp(s - jnp.max(s, axis=-1, keepdims=True))
        p = (e / jnp.sum(e, axis=-1, keepdims=True)).astype(BF16)
        outs.append(_dot(p, kv[:, D_MODEL + h * XA_HEAD_DIM:D_MODEL + (h + 1) * XA_HEAD_DIM]))
    o = jnp.concatenate(outs, axis=1).astype(BF16)
    y = _layer_norm(DN_ALPHA * xf_ref[...] + _dot(o, wob_ref[...]), g_ref[...], b_ref[...])
    of_ref[...] = y
    ob_ref[...] = y.astype(BF16)
    _to_token_tiles(og_ref, y)


def _cross_attn(xb, xf, kv, l, w, p, batch, seq, mem_len):
    t, d = xf.shape
    nblk = seq // ROW_BLOCK
    row = pl.BlockSpec((ROW_BLOCK, d), lambda b, i: (b * nblk + i, 0))
    mat = _layer_weight(l, d, d)
    vec = pl.BlockSpec((1, d), lambda b, i: (0, 0))
    return pl.pallas_call(
        _attn_kernel,
        grid=(batch, nblk),
        in_specs=[row, row, pl.BlockSpec((mem_len, 2 * d), lambda b, i: (b, 0)), mat, mat, vec, vec],
        out_specs=[row, row, pl.BlockSpec((ROW_BLOCK * SUBLANES, LANES), lambda b, i: (b * nblk + i, 0))],
        out_shape=[jax.ShapeDtypeStruct((t, d), F32), jax.ShapeDtypeStruct((t, d), BF16),
                   jax.ShapeDtypeStruct((t * SUBLANES, LANES), F32)],
        scratch_shapes=[pltpu.VMEM((d, d), BF16)] * 2,
        compiler_params=_cparams("arbitrary", "arbitrary"),
        name="cross_attn",
    )(xb, xf, kv, w["wq"], w["wo"], p["ln_g1"], p["ln_b1"])


def _lanes_from_cols(cols, shape):
    lane = lax.broadcasted_iota(jnp.int32, shape, 1)
    out = jnp.zeros(shape, cols[0].dtype)
    for k, c in enumerate(cols):
        out = jnp.where(lane == k, c, out)
    return out


def _router_kernel(xb_ref, wr_ref, br_ref, gate_ref, lpos_ref, bcnt_ref, bbase_ref, bstart_ref, carry_ref):
    @pl.when(pl.program_id(0) == 0)
    def _():
        carry_ref[...] = jnp.zeros_like(carry_ref)

    m = xb_ref.shape[0]
    logits = _dot(xb_ref[...], wr_ref[...]) + br_ref[...]
    lane = lax.broadcasted_iota(jnp.int32, logits.shape, 1)
    tops, hots = [], []
    for _ in range(TOP_K):
        top = jnp.max(logits, axis=-1, keepdims=True)
        idx = jnp.min(jnp.where(logits == top, lane, LANES), axis=-1, keepdims=True)
        hot = lane == idx
        logits = jnp.where(hot, -jnp.inf, logits)
        tops.append(top)
        hots.append(hot)
    es = [jnp.exp(v - tops[0]) for v in tops]
    den = es[0] + es[1] + es[2] + es[3]

    hot_all = jnp.zeros(logits.shape, F32)
    for hot in hots:
        hot_all = hot_all + jnp.where(hot, 1.0, 0.0)
    r = lax.broadcasted_iota(jnp.int32, (m, m), 0)
    c = lax.broadcasted_iota(jnp.int32, (m, m), 1)
    before = jnp.where(c < r, 1.0, 0.0).astype(BF16)
    cnt = jnp.sum(hot_all, axis=0, keepdims=True)
    er = lax.broadcasted_iota(jnp.int32, (LANES, LANES), 0)
    ec = lax.broadcasted_iota(jnp.int32, (LANES, LANES), 1)
    lower_experts = jnp.where(er < ec, 1.0, 0.0).astype(BF16)
    lstart = _dot_exact_lhs(jnp.broadcast_to(cnt, (SUBLANES, LANES)), lower_experts)[0:1]
    local = lstart + _dot(before, hot_all.astype(BF16))
    lpos = [jnp.sum(jnp.where(hot, local, 0.0), axis=-1, keepdims=True) for hot in hots]

    gate_ref[...] = _lanes_from_cols([e / den for e in es], logits.shape)
    lpos_ref[...] = _lanes_from_cols(lpos, logits.shape).astype(jnp.int32)
    bcnt_ref[0] = cnt.astype(jnp.int32)
    bbase_ref[0] = carry_ref[...].astype(jnp.int32)
    bstart_ref[0] = lstart.astype(jnp.int32)
    carry_ref[...] = carry_ref[...] + cnt


def _router(xb, p):
    t, d = xb.shape
    nblk = t // ROUTE_BLOCK
    row = pl.BlockSpec((ROUTE_BLOCK, LANES), lambda i: (i, 0))
    one = pl.BlockSpec((1, LANES), lambda i: (0, 0))
    per_block = pl.BlockSpec((1, 1, LANES), lambda i: (i, 0, 0))
    per_block_shape = jax.ShapeDtypeStruct((nblk, 1, LANES), jnp.int32)
    return pl.pallas_call(
        _router_kernel,
        grid=(nblk,),
        in_specs=[pl.BlockSpec((ROUTE_BLOCK, d), lambda i: (i, 0)),
                  pl.BlockSpec((d, LANES), lambda i: (0, 0)), one],
        out_specs=[row, row, per_block, per_block, per_block],
        out_shape=[jax.ShapeDtypeStruct((t, LANES), F32), jax.ShapeDtypeStruct((t, LANES), jnp.int32),
                   per_block_shape, per_block_shape, per_block_shape],
        scratch_shapes=[pltpu.VMEM((1, LANES), F32)],
        compiler_params=_cparams("arbitrary"),
        name="router",
    )(xb, p["w_router"], p["b_router"])


def _range_copies(n, near_ref, near_row, far_hbm, far_row, sem, to_far):
    p = ROUTE_BLOCK
    while p >= 1:
        done = jnp.bitwise_and(n, -2 * p)

        @pl.when(jnp.bitwise_and(n, p) != 0)
        def _(p=p, done=done):
            near = near_ref.at[pl.ds(near_row + done, p)]
            far = far_hbm.at[pl.ds(far_row + done, p)]
            if to_far:
                pltpu.make_async_copy(near, far, sem).start()
            else:
                pltpu.make_async_copy(far, near, sem).start()

        p //= 2


def _stage_half(stage_ref, s):
    return stage_ref.at[pl.ds(pl.multiple_of(s * PAIRS_PER_BLOCK, PAIRS_PER_BLOCK), PAIRS_PER_BLOCK)]


def _dispatch_kernel(cnt_ref, lstart_ref, gstart_ref, fill_ref, lpos_ref, xg_ref, xs_hbm,
                     stage_ref, zero_ref, sem, zsem):
    b = pl.program_id(0)
    last = pl.num_programs(0) - 1
    slot = jnp.bitwise_and(b, 1)

    def wait_half(s):
        pltpu.make_async_copy(_stage_half(stage_ref, s), xs_hbm.at[pl.ds(0, PAIRS_PER_BLOCK)], sem.at[s]).wait()

    @pl.when(b == 0)
    def _():
        zero_ref[...] = jnp.zeros_like(zero_ref)

        def zfill(e, carry):
            copy = pltpu.make_async_copy(zero_ref, xs_hbm.at[pl.ds(fill_ref[e], EXPERT_TILE)], zsem)
            copy.start()
            copy.wait()
            return carry

        lax.fori_loop(0, N_EXPERTS, zfill, 0)

        def tail_copy(i):
            dst = xs_hbm.at[pl.ds(pl.multiple_of(i * EXPERT_TILE, EXPERT_TILE), EXPERT_TILE)]
            return pltpu.make_async_copy(zero_ref, dst, zsem)

        def tail_start(i, carry):
            tail_copy(i).start()
            return carry

        def tail_wait(i, carry):
            tail_copy(i).wait()
            return carry

        n_all = xs_hbm.shape[0] // EXPERT_TILE
        lax.fori_loop(fill_ref[N_EXPERTS], n_all, tail_start, 0)
        lax.fori_loop(fill_ref[N_EXPERTS], n_all, tail_wait, 0)

    base = slot * PAIRS_PER_BLOCK

    def fill(c, carry):
        for u in range(ROW_UNROLL):
            t = c * ROW_UNROLL + u
            tile = xg_ref[t]
            for k in range(TOP_K):
                stage_ref[base + lpos_ref[t * TOP_K + k]] = tile
        return carry

    lax.fori_loop(0, ROUTE_BLOCK // ROW_UNROLL, fill, 0)

    def ranges(e, carry):
        j = b * N_EXPERTS + e
        _range_copies(cnt_ref[j], stage_ref, base + lstart_ref[j], xs_hbm, gstart_ref[j], sem.at[slot], True)
        return carry

    lax.fori_loop(0, N_EXPERTS, ranges, 0)

    @pl.when(b > 0)
    def _():
        wait_half(1 - slot)

    @pl.when(b == last)
    def _():
        wait_half(slot)


def _dispatch(xg, lpos_flat, cnt, lstart, gstart, fill_from, n_slots):
    nblk = xg.shape[0] // ROUTE_BLOCK
    tile = (SUBLANES, LANES)
    grid_spec = pltpu.PrefetchScalarGridSpec(
        num_scalar_prefetch=4,
        grid=(nblk,),
        in_specs=[pl.BlockSpec((PAIRS_PER_BLOCK,), lambda b, *_: (b,), memory_space=pltpu.SMEM),
                  pl.BlockSpec((ROUTE_BLOCK,) + tile, lambda b, *_: (b, 0, 0))],
        out_specs=pl.BlockSpec(memory_space=pl.ANY),
        scratch_shapes=[pltpu.VMEM((2 * PAIRS_PER_BLOCK,) + tile, F32), pltpu.VMEM((EXPERT_TILE,) + tile, F32),
                        pltpu.SemaphoreType.DMA((2,)), pltpu.SemaphoreType.DMA(())],
    )
    return pl.pallas_call(
        _dispatch_kernel,
        grid_spec=grid_spec,
        out_shape=jax.ShapeDtypeStruct((n_slots,) + tile, F32),
        compiler_params=_cparams("arbitrary"),
        name="moe_dispatch",
    )(cnt, lstart, gstart, fill_from, lpos_flat, xg)


def _expert_kernel(te_ref, nu_ref, x_ref, wgu_ref, bgu_ref, wd_ref, bd_ref, o_ref, wgub_ref, wdb_ref):
    i = pl.program_id(0)

    @pl.when(jnp.logical_or(i == 0, te_ref[i] != te_ref[jnp.maximum(i - 1, 0)]))
    def _():
        _cast_weight(wgub_ref, wgu_ref.at[0, 0])
        _cast_weight(wdb_ref, wd_ref.at[0, 0])

    @pl.when(i < nu_ref[0])
    def _():
        x = _from_token_tiles(x_ref, EXPERT_TILE).astype(BF16)
        hgu = _dot(x, wgub_ref[...]) + bgu_ref[0, 0]
        gate = jnp.minimum(hgu[:, :D_EXPERT], SWIGLU_LIMIT)
        up = jnp.clip(hgu[:, D_EXPERT:], -SWIGLU_LIMIT, SWIGLU_LIMIT)
        glu = gate * _sigmoid(SWIGLU_ALPHA * gate)
        _to_token_tiles(o_ref, _dot(((up + 1.0) * glu).astype(BF16), wdb_ref[...]) + bd_ref[0, 0])

    @pl.when(i >= nu_ref[0])
    def _():
        o_ref[...] = jnp.zeros_like(o_ref)


def _experts(xs, tile_expert, n_used, n_tiles, l, w):
    d = D_MODEL

    def of_expert(shape):
        return pl.BlockSpec((1, 1) + shape, lambda i, te, nu: (l, te[i], 0, 0))

    grid_spec = pltpu.PrefetchScalarGridSpec(
        num_scalar_prefetch=2,
        grid=(n_tiles,),
        in_specs=[
            pl.BlockSpec((TILE_LINES, LANES), lambda i, te, nu: (jnp.minimum(i, nu[0] - 1), 0)),
            of_expert((d, 2 * D_EXPERT)), of_expert((1, 2 * D_EXPERT)),
            of_expert((D_EXPERT, d)), of_expert((1, d)),
        ],
        out_specs=pl.BlockSpec((TILE_LINES, LANES), lambda i, te, nu: (i, 0)),
        scratch_shapes=[pltpu.VMEM((d, 2 * D_EXPERT), BF16), pltpu.VMEM((D_EXPERT, d), BF16)],
    )
    return pl.pallas_call(
        _expert_kernel,
        grid_spec=grid_spec,
        out_shape=jax.ShapeDtypeStruct((n_tiles * TILE_LINES, LANES), F32),
        compiler_params=_cparams("arbitrary"),
        name="moe_experts",
    )(tile_expert, n_used, xs, w["w_gu"], w["b_gu"], w["w_down"], w["b_down"])


def _combine_kernel(cnt_ref, lstart_ref, gstart_ref, lpos_ref, gate_ref, ys_hbm, x_ref, g_ref, b_ref,
                    of_ref, ob_ref, stage_ref, acc_ref, sem):
    b = pl.program_id(0)
    last = pl.num_programs(0) - 1
    slot = jnp.bitwise_and(b, 1)

    def fetch(blk, s):
        def ranges(e, carry):
            j = blk * N_EXPERTS + e
            _range_copies(cnt_ref[j], stage_ref, s * PAIRS_PER_BLOCK + lstart_ref[j], ys_hbm, gstart_ref[j],
                          sem.at[s], False)
            return carry

        lax.fori_loop(0, N_EXPERTS, ranges, 0)

    @pl.when(b == 0)
    def _():
        fetch(b, slot)

    @pl.when(b < last)
    def _():
        fetch(b + 1, 1 - slot)

    pltpu.make_async_copy(ys_hbm.at[pl.ds(0, PAIRS_PER_BLOCK)], _stage_half(stage_ref, slot), sem.at[slot]).wait()

    base = slot * PAIRS_PER_BLOCK

    def gather(c, carry):
        for u in range(ROW_UNROLL):
            t = c * ROW_UNROLL + u
            tile = jnp.zeros((SUBLANES, LANES), F32)
            for k in range(TOP_K):
                tile = tile + gate_ref[t * TOP_K + k] * stage_ref[base + lpos_ref[t * TOP_K + k]]
            acc_ref[pl.ds(pl.multiple_of(t * SUBLANES, SUBLANES), SUBLANES), :] = tile
        return carry

    lax.fori_loop(0, ROUTE_BLOCK // ROW_UNROLL, gather, 0)

    y = _layer_norm(DN_ALPHA * x_ref[...] + _from_token_tiles(acc_ref, ROUTE_BLOCK), g_ref[...], b_ref[...])
    of_ref[...] = y
    ob_ref[...] = y.astype(BF16)


def _combine(ys, lpos_flat, gate_flat, cnt, lstart, gstart, xf, p):
    t, d = xf.shape
    nblk = t // ROUTE_BLOCK
    row = pl.BlockSpec((ROUTE_BLOCK, d), lambda b, *_: (b, 0))
    vec = pl.BlockSpec((1, d), lambda b, *_: (0, 0))
    pairs = pl.BlockSpec((PAIRS_PER_BLOCK,), lambda b, *_: (b,), memory_space=pltpu.SMEM)
    grid_spec = pltpu.PrefetchScalarGridSpec(
        num_scalar_prefetch=3,
        grid=(nblk,),
        in_specs=[pairs, pairs, pl.BlockSpec(memory_space=pl.ANY), row, vec, vec],
        out_specs=[row, row],
        scratch_shapes=[pltpu.VMEM((2 * PAIRS_PER_BLOCK, SUBLANES, LANES), F32),
                        pltpu.VMEM((ROUTE_BLOCK * SUBLANES, LANES), F32), pltpu.SemaphoreType.DMA((2,))],
    )
    return pl.pallas_call(
        _combine_kernel,
        grid_spec=grid_spec,
        out_shape=[jax.ShapeDtypeStruct((t, d), F32), jax.ShapeDtypeStruct((t, d), BF16)],
        compiler_params=_cparams("arbitrary"),
        name="moe_combine",
    )(cnt, lstart, gstart, lpos_flat, gate_flat, ys, xf, p["ln_g2"], p["ln_b2"])


def _moe(xb, xf, xg, l, w, p):
    t, d = xf.shape
    gates, lpos, bcnt, bbase, bstart = _router(xb, p)
    bcnt = bcnt[:, 0, :N_EXPERTS]
    counts = jnp.sum(bcnt, axis=0)
    padded = (counts + EXPERT_TILE - 1) // EXPERT_TILE * EXPERT_TILE
    ends = jnp.cumsum(padded)
    starts = ends - padded
    gstart = (starts[None, :] + bbase[:, 0, :N_EXPERTS]).reshape(-1)
    lstart = bstart[:, 0, :N_EXPERTS].reshape(-1)
    n_tiles = (t * TOP_K) // EXPERT_TILE + N_EXPERTS
    n_used = ends[-1] // EXPERT_TILE
    tile_start = jnp.minimum(jnp.arange(n_tiles, dtype=jnp.int32), n_used - 1) * EXPERT_TILE
    tile_expert = jnp.sum((ends[None, :] <= tile_start[:, None]).astype(jnp.int32), axis=1)
    lpos_flat = lpos[:, :TOP_K].reshape(-1)
    gate_flat = gates[:, :TOP_K].reshape(-1)

    fill_from = jnp.concatenate([starts + counts, n_used.reshape(1)])
    n_slots = (n_tiles + 1) * EXPERT_TILE
    xs = _dispatch(xg.reshape(t, SUBLANES, LANES), lpos_flat, bcnt.reshape(-1), lstart, gstart, fill_from, n_slots)
    ys = _experts(xs.reshape(n_slots * SUBLANES, LANES), tile_expert, n_used.reshape(1), n_tiles, l, w)
    return _combine(ys.reshape(n_tiles * EXPERT_TILE, SUBLANES, LANES), lpos_flat, gate_flat, bcnt.reshape(-1),
                    lstart, gstart, xf, p)


def _layer_params(l, w_in, conv_w, conv_b, dt_bias, a_log, d_skip, ssd_norm_w, gm_ln_g, gm_ln_b,
                  w_sp, b_sp, w_router, b_router, ln_g, ln_b):
    d = D_MODEL
    off_dt = d + conv_w.shape[-1]
    off_u = off_dt + SSD_HEADS
    pad_h = LANES - SSD_HEADS
    pad_e = LANES - N_EXPERTS
    head_of_channel = jnp.arange(d, dtype=jnp.int32) // SSD_HEAD_DIM
    return {
        "w_dt": jnp.pad(w_in[l, :, off_dt:off_u], ((0, 0), (0, pad_h))).astype(BF16),
        "conv_w": conv_w[l], "conv_b": conv_b[l].reshape(1, -1),
        "dt_bias": jnp.pad(dt_bias[l], (0, pad_h)).reshape(1, LANES),
        "a_log": jnp.pad(a_log[l], (0, pad_h)).reshape(1, LANES),
        "d_skip": d_skip[l][head_of_channel].reshape(1, d),
        "ssd_norm_w": ssd_norm_w[l].reshape(1, d),
        "gm_ln_g": gm_ln_g[l].reshape(1, d), "gm_ln_b": gm_ln_b[l].reshape(1, d),
        "w_sp": w_sp[l],
        "b_sp": jnp.repeat(b_sp[l].T, GM_GROUP_DIM, axis=1),
        "head_expand": (jnp.arange(LANES, dtype=jnp.int32)[:, None] == head_of_channel[None, :]).astype(BF16),
        "w_router": jnp.pad(w_router[l], ((0, 0), (0, pad_e))).astype(BF16),
        "b_router": jnp.pad(b_router[l], (0, pad_e), constant_values=NEG_BIG).reshape(1, LANES),
        "ln_g0": ln_g[l, 0].reshape(1, d), "ln_b0": ln_b[l, 0].reshape(1, d),
        "ln_g1": ln_g[l, 1].reshape(1, d), "ln_b1": ln_b[l, 1].reshape(1, d),
        "ln_g2": ln_g[l, 2].reshape(1, d), "ln_b2": ln_b[l, 2].reshape(1, d),
    }


def kernel(x, mem, ln0_g, ln0_b, w_in, conv_w, conv_b, dt_bias, a_log, d_skip, ssd_norm_w, gm_ln_g, gm_ln_b, w_sp, b_sp, p_ssd, p_gm, w_out, wq, wk, wv, wo, w_router, b_router, w_gu, b_gu, w_down, b_down, ln_g, ln_b):
    batch, seq, d = x.shape
    mem_len = mem.shape[1]
    depth = w_in.shape[0]
    assert d == D_MODEL and seq % ROW_BLOCK == 0 and seq % CHUNK == 0
    t = batch * seq
    memb = mem.reshape(batch * mem_len, d).astype(BF16)
    off_u = d + conv_w.shape[-1] + SSD_HEADS
    w = {"w_in_tail": w_in[:, :, off_u:],
         "w_kv": jnp.concatenate([wk, wv], axis=2),
         "p_ssd": p_ssd, "p_gm": p_gm, "w_out": w_out, "wq": wq, "wo": wo,
         "w_gu": w_gu, "b_gu": b_gu.reshape(depth, N_EXPERTS, 1, -1),
         "w_down": w_down, "b_down": b_down.reshape(depth, N_EXPERTS, 1, -1)}
    xf, xb = _entry_ln(x.reshape(t, d), ln0_g, ln0_b)
    for l in range(depth):
        p = _layer_params(l, w_in, conv_w, conv_b, dt_bias, a_log, d_skip, ssd_norm_w, gm_ln_g,
                          gm_ln_b, w_sp, b_sp, w_router, b_router, ln_g, ln_b)
        bm = min(MM_BLOCK_M, t)
        zxa = _matmul(xb, w_in, l, 3 * d, bm)
        zxb = _matmul(xb, w["w_in_tail"], l, 4 * d, bm)
        y_ssd, y_gm = _mixer(zxa, zxb, xb, p, batch, seq)
        xf, xb = _merge(y_ssd, y_gm, zxb, xf, l, w, p)
        kv = _matmul(memb, w["w_kv"], l, 2 * d, min(MM_BLOCK_M, batch * mem_len))
        xf, xb, xg = _cross_attn(xb, xf, kv, l, w, p, batch, seq, mem_len)
        xf, xb = _moe(xb, xf, xg, l, w, p)
    return xf.reshape(batch, seq, d)
```

```python
import math

import jax
import jax.numpy as jnp
from jax import lax
from jax.experimental import pallas as pl
from jax.experimental.pallas import tpu as pltpu

F32 = jnp.float32
BF16 = jnp.bfloat16

D_MODEL = 1024
DEPTH = 2
CHUNK = 128
SSD_HEADS = 16
SSD_HEAD_DIM = 64
SSD_GROUPS = 4
SSD_HPG = SSD_HEADS // SSD_GROUPS
SSD_STATE = 128
SSD_GROUP_W = SSD_HPG * SSD_HEAD_DIM
CONV_K = 4
CONV_TAIL = CHUNK
GM_GROUPS = 8
GM_GROUP_DIM = D_MODEL // GM_GROUPS
XA_HEADS = 4
XA_HEAD_DIM = D_MODEL // XA_HEADS
N_EXPERTS = 32
TOP_K = 4
D_EXPERT = D_MODEL
SWIGLU_LIMIT = 7.0
SWIGLU_ALPHA = 1.702
DN_ALPHA = (2 * DEPTH) ** 0.25
EPS = 1e-5

LANES = 128
SUBLANES = 8
VMEM_LIMIT = 56 * 1024 * 1024

ROW_BLOCK = 512
MM_BLOCK_M = 1024
MM_BLOCK_N = 1024
CAST_ROWS = 128
ROUTE_BLOCK = 512
EXPERT_TILE = 512
NEG_BIG = -1e30
TILE_LINES = EXPERT_TILE * SUBLANES
PAIRS_PER_BLOCK = ROUTE_BLOCK * TOP_K
ROW_UNROLL = 4


def _cparams(*sem):
    return pltpu.CompilerParams(dimension_semantics=sem, vmem_limit_bytes=VMEM_LIMIT)


def _layer_norm(x, g, b):
    mu = jnp.mean(x, axis=-1, keepdims=True)
    xc = x - mu
    var = jnp.mean(xc * xc, axis=-1, keepdims=True)
    return xc * lax.rsqrt(var + EPS) * g + b


def _dot(a, b):
    return jnp.dot(a, b, preferred_element_type=F32)


def _dot_nt(a, b):
    return lax.dot_general(a, b, (((1,), (1,)), ((), ())), preferred_element_type=F32)


def _dot_tn(a, b):
    return lax.dot_general(a, b, (((0,), (0,)), ((), ())), preferred_element_type=F32)


def _split3(v):
    hi = v.astype(BF16)
    r1 = v - hi.astype(F32)
    mid = r1.astype(BF16)
    lo = (r1 - mid.astype(F32)).astype(BF16)
    return hi, mid, lo


def _dot_exact_rhs(sel, v):
    hi, mid, lo = _split3(v)
    return _dot(sel, hi) + _dot(sel, mid) + _dot(sel, lo)


def _dot_exact_lhs(v, sel):
    hi, mid, lo = _split3(v)
    return _dot(hi, sel) + _dot(mid, sel) + _dot(lo, sel)


def _sigmoid(x):
    return 1.0 / (1.0 + jnp.exp(-x))


def _gelu(x):
    return 0.5 * x * (1.0 + lax.erf(x * math.sqrt(0.5)))


def _softplus(x):
    return jnp.maximum(x, 0.0) + jnp.log1p(jnp.exp(-jnp.abs(x)))


def _cast_weight(dst_ref, src_ref):
    def body(c, carry):
        rows = pl.ds(pl.multiple_of(c * CAST_ROWS, CAST_ROWS), CAST_ROWS)
        dst_ref[rows, :] = src_ref[rows, :].astype(BF16)
        return carry

    lax.fori_loop(0, src_ref.shape[0] // CAST_ROWS, body, 0)


def _layer_weight(l, k, n):
    return pl.BlockSpec((1, k, n), lambda *_: (l, 0, 0), pipeline_mode=pl.Buffered(1))


def _ln_kernel(x_ref, g_ref, b_ref, of_ref, ob_ref):
    y = _layer_norm(x_ref[...], g_ref[...], b_ref[...])
    of_ref[...] = y
    ob_ref[...] = y.astype(BF16)


def _entry_ln(x, g, b):
    t, d = x.shape
    row = pl.BlockSpec((ROW_BLOCK, d), lambda i: (i, 0))
    vec = pl.BlockSpec((1, d), lambda i: (0, 0))
    return pl.pallas_call(
        _ln_kernel,
        grid=(t // ROW_BLOCK,),
        in_specs=[row, vec, vec],
        out_specs=[row, row],
        out_shape=[jax.ShapeDtypeStruct((t, d), F32), jax.ShapeDtypeStruct((t, d), BF16)],
        compiler_params=_cparams("arbitrary"),
        name="entry_ln",
    )(x, g.reshape(1, d), b.reshape(1, d))


def _mm_kernel(a_ref, w_ref, o_ref, wb_ref):
    @pl.when(pl.program_id(1) == 0)
    def _():
        _cast_weight(wb_ref, w_ref.at[0])

    o_ref[...] = _dot(a_ref[...], wb_ref[...]).astype(o_ref.dtype)


def _matmul(a, w, l, n, bm):
    m, k = a.shape
    bn = MM_BLOCK_N
    return pl.pallas_call(
        _mm_kernel,
        grid=(n // bn, m // bm),
        in_specs=[pl.BlockSpec((bm, k), lambda j, i: (i, 0)),
                  pl.BlockSpec((1, k, bn), lambda j, i: (l, 0, j))],
        out_specs=pl.BlockSpec((bm, bn), lambda j, i: (i, j)),
        out_shape=jax.ShapeDtypeStruct((m, n), BF16),
        scratch_shapes=[pltpu.VMEM((k, bn), BF16)],
        compiler_params=_cparams("arbitrary", "arbitrary"),
        name="matmul",
    )(a, w)


def _conv_silu(raw, tail_ref, shift, w, b):
    aug = jnp.concatenate([tail_ref[...], raw], axis=0)
    shifted = _dot(shift, aug)
    acc = b
    for j in range(CONV_K):
        acc = acc + shifted[j * CHUNK:(j + 1) * CHUNK] * w[CONV_K - 1 - j:CONV_K - j]
    tail_ref[...] = raw[CHUNK - CONV_TAIL:CHUNK]
    return acc * _sigmoid(acc)


def _mixer_kernel(z_ref, xs_ref, bc_ref, u_ref, v_ref, xb_ref, wdt_ref, cw_ref, cb_ref, dtb_ref,
                  alog_ref, dskip_ref, nw_ref, lng_ref, lnb_ref, wsp_ref, bsp_ref, hexp_ref, shift_ref,
                  yssd_ref, ygm_ref, tailx_ref, tailbc_ref, state_ref):
    @pl.when(pl.program_id(1) == 0)
    def _():
        tailx_ref[...] = jnp.zeros_like(tailx_ref)
        tailbc_ref[...] = jnp.zeros_like(tailbc_ref)
        state_ref[...] = jnp.zeros_like(state_ref)

    row = lax.broadcasted_iota(jnp.int32, (CHUNK, CHUNK), 0)
    col = lax.broadcasted_iota(jnp.int32, (CHUNK, CHUNK), 1)
    causal = col <= row
    tri = jnp.where(causal, 1.0, 0.0).astype(BF16)

    cw = cw_ref[...]
    cb = cb_ref[...]
    shift = shift_ref[...]
    xs = _conv_silu(xs_ref[...], tailx_ref, shift, cw[:, :D_MODEL], cb[:, :D_MODEL])
    bc = _conv_silu(bc_ref[...], tailbc_ref, shift, cw[:, D_MODEL:], cb[:, D_MODEL:])
    gn = SSD_GROUPS * SSD_STATE

    dt = _softplus(_dot(xb_ref[...], wdt_ref[...]) + dtb_ref[...])
    a = -jnp.exp(alog_ref[...])
    cs = _dot_exact_rhs(tri, dt * a)
    cs_t = cs.T
    hexp = hexp_ref[...]
    dt_x = _dot_exact_lhs(dt, hexp)
    cs_x = _dot_exact_lhs(cs, hexp)
    tot_x = cs_x[CHUNK - 1:CHUNK]
    xdt = xs * dt_x
    xdt_b = xdt.astype(BF16)
    xdec_b = (xdt * jnp.exp(tot_x - cs_x)).astype(BF16)
    seg = lax.shift_right_logical(lax.broadcasted_iota(jnp.int32, (CHUNK, SSD_GROUP_W), 1),
                                  int(math.log2(SSD_HEAD_DIM)))

    y_parts = []
    for g in range(SSD_GROUPS):
        b_g = bc[:, g * SSD_STATE:(g + 1) * SSD_STATE].astype(BF16)
        c_g = bc[:, gn + g * SSD_STATE:gn + (g + 1) * SSD_STATE].astype(BF16)
        cols = slice(g * SSD_GROUP_W, (g + 1) * SSD_GROUP_W)
        cb_g = _dot_nt(c_g, b_g)
        st = state_ref[:, cols]
        y_g = _dot(c_g, st.astype(BF16)) * jnp.exp(cs_x[:, cols])
        for hh in range(SSD_HPG):
            h = g * SSD_HPG + hh
            diff = cs[:, h:h + 1] - cs_t[h:h + 1, :]
            m_h = (cb_g * jnp.exp(jnp.where(causal, diff, -jnp.inf))).astype(BF16)
            y_g = y_g + jnp.where(seg == hh, _dot(m_h, xdt_b[:, cols]), 0.0)
        state_ref[:, cols] = st * jnp.exp(tot_x[:, cols]) + _dot_tn(b_g, xdec_b[:, cols])
        y_parts.append(y_g)
    y = jnp.concatenate(y_parts, axis=1) + dskip_ref[...] * xs

    z = z_ref[...].astype(F32)
    y = y * (z * _sigmoid(z))
    n_parts = []
    for g in range(SSD_GROUPS):
        yg = y[:, g * SSD_GROUP_W:(g + 1) * SSD_GROUP_W]
        n_parts.append(yg * lax.rsqrt(jnp.mean(yg * yg, axis=-1, keepdims=True) + EPS))
    yssd_ref[...] = (jnp.concatenate(n_parts, axis=1) * nw_ref[...]).astype(BF16)

    u = _gelu(u_ref[...].astype(F32))
    v = _layer_norm(_gelu(v_ref[...].astype(F32)), lng_ref[...], lnb_ref[...]).astype(BF16)
    sv_parts = []
    for g in range(GM_GROUPS):
        w_g = jnp.where(causal, wsp_ref[g], 0.0).astype(BF16)
        sv_parts.append(_dot(w_g, v[:, g * GM_GROUP_DIM:(g + 1) * GM_GROUP_DIM]))
    ygm_ref[...] = (u * (jnp.concatenate(sv_parts, axis=1) + bsp_ref[...])).astype(BF16)


def _mixer(zxa, zxb, xb, p, batch, seq):
    t, d = xb.shape
    nchunk = seq // CHUNK

    def blk(j):
        return pl.BlockSpec((CHUNK, d), lambda b, c, j=j: (b * nchunk + c, j))

    def const(shape):
        return pl.BlockSpec(shape, lambda b, c: (0,) * len(shape))

    out = pl.BlockSpec((CHUNK, d), lambda b, c: (b * nchunk + c, 0))
    return pl.pallas_call(
        _mixer_kernel,
        grid=(batch, nchunk),
        in_specs=[blk(0), blk(1), blk(2), blk(0), blk(1),
                  pl.BlockSpec((CHUNK, d), lambda b, c: (b * nchunk + c, 0)),
                  const((d, LANES)), const((CONV_K, 2 * d)), const((1, 2 * d)), const((1, LANES)),
                  const((1, LANES)), const((1, d)), const((1, d)), const((1, d)), const((1, d)),
                  const((GM_GROUPS, CHUNK, CHUNK)), const((CHUNK, d)), const((LANES, d)),
                  const((CONV_K * CHUNK, CONV_TAIL + CHUNK))],
        out_specs=[out, out],
        out_shape=[jax.ShapeDtypeStruct((t, d), BF16), jax.ShapeDtypeStruct((t, d), BF16)],
        scratch_shapes=[pltpu.VMEM((CONV_TAIL, d), BF16), pltpu.VMEM((CONV_TAIL, d), BF16),
                        pltpu.VMEM((SSD_STATE, d), F32)],
        compiler_params=_cparams("arbitrary", "arbitrary"),
        name="mixer",
    )(zxa, zxa, zxa, zxb, zxb, xb, p["w_dt"], p["conv_w"], p["conv_b"], p["dt_bias"], p["a_log"],
      p["d_skip"], p["ssd_norm_w"], p["gm_ln_g"], p["gm_ln_b"], p["w_sp"], p["b_sp"], p["head_expand"],
      p["conv_shift"])


def _merge_kernel(ys_ref, yg_ref, gs_ref, gg_ref, x_ref, ps_ref, pg_ref, wo_ref, g_ref, b_ref,
                  of_ref, ob_ref, psb_ref, pgb_ref, wob_ref):
    @pl.when(pl.program_id(0) == 0)
    def _():
        _cast_weight(psb_ref, ps_ref.at[0])
        _cast_weight(pgb_ref, pg_ref.at[0])
        _cast_weight(wob_ref, wo_ref.at[0])

    h = (_sigmoid(gs_ref[...].astype(F32)) * _dot(ys_ref[...], psb_ref[...])
         + _sigmoid(gg_ref[...].astype(F32)) * _dot(yg_ref[...], pgb_ref[...]))
    mix = _dot(h.astype(BF16), wob_ref[...])
    y = _layer_norm(DN_ALPHA * x_ref[...] + mix, g_ref[...], b_ref[...])
    of_ref[...] = y
    ob_ref[...] = y.astype(BF16)


def _merge(y_ssd, y_gm, zxb, xf, l, w, p):
    t, d = xf.shape
    row = pl.BlockSpec((ROW_BLOCK, d), lambda i: (i, 0))
    mat = _layer_weight(l, d, d)
    vec = pl.BlockSpec((1, d), lambda i: (0, 0))
    return pl.pallas_call(
        _merge_kernel,
        grid=(t // ROW_BLOCK,),
        in_specs=[row, row, pl.BlockSpec((ROW_BLOCK, d), lambda i: (i, 2)),
                  pl.BlockSpec((ROW_BLOCK, d), lambda i: (i, 3)), row, mat, mat, mat, vec, vec],
        out_specs=[row, row],
        out_shape=[jax.ShapeDtypeStruct((t, d), F32), jax.ShapeDtypeStruct((t, d), BF16)],
        scratch_shapes=[pltpu.VMEM((d, d), BF16)] * 3,
        compiler_params=_cparams("arbitrary"),
        name="merge",
    )(y_ssd, y_gm, zxb, zxb, xf, w["p_ssd"], w["p_gm"], w["w_out"], p["ln_g0"], p["ln_b0"])


def _to_token_tiles(ref, y):
    m = y.shape[0]
    for j in range(SUBLANES):
        ref[pl.ds(j, m, stride=SUBLANES), :] = y[:, j * LANES:(j + 1) * LANES]


def _from_token_tiles(ref, m):
    return jnp.concatenate([ref[pl.ds(j, m, stride=SUBLANES), :] for j in range(SUBLANES)], axis=1)


def _attn_kernel(xb_ref, xf_ref, kv_ref, wq_ref, wo_ref, g_ref, b_ref, of_ref, ob_ref, og_ref,
                 wqb_ref, wob_ref):
    @pl.when(jnp.logical_and(pl.program_id(0) == 0, pl.program_id(1) == 0))
    def _():
        _cast_weight(wqb_ref, wq_ref.at[0])
        _cast_weight(wob_ref, wo_ref.at[0])

    q = _dot(xb_ref[...], wqb_ref[...]).astype(BF16)
    kv = kv_ref[...]
    outs = []
    for h in range(XA_HEADS):
        cols = slice(h * XA_HEAD_DIM, (h + 1) * XA_HEAD_DIM)
        s = _dot_nt(q[:, cols], kv[:, cols]) * (XA_HEAD_DIM ** -0.5)
        e = jnp.exp(s - jnp.max(s, axis=-1, keepdims=True))
        p = (e / jnp.sum(e, axis=-1, keepdims=True)).astype(BF16)
        outs.append(_dot(p, kv[:, D_MODEL + h * XA_HEAD_DIM:D_MODEL + (h + 1) * XA_HEAD_DIM]))
    o = jnp.concatenate(outs, axis=1).astype(BF16)
    y = _layer_norm(DN_ALPHA * xf_ref[...] + _dot(o, wob_ref[...]), g_ref[...], b_ref[...])
    of_ref[...] = y
    ob_ref[...] = y.astype(BF16)
    _to_token_tiles(og_ref, y)


def _cross_attn(xb, xf, kv, l, w, p, batch, seq, mem_len):
    t, d = xf.shape
    nblk = seq // ROW_BLOCK
    row = pl.BlockSpec((ROW_BLOCK, d), lambda b, i: (b * nblk + i, 0))
    mat = _layer_weight(l, d, d)
    vec = pl.BlockSpec((1, d), lambda b, i: (0, 0))
    return pl.pallas_call(
        _attn_kernel,
        grid=(batch, nblk),
        in_specs=[row, row, pl.BlockSpec((mem_len, 2 * d), lambda b, i: (b, 0)), mat, mat, vec, vec],
        out_specs=[row, row, pl.BlockSpec((ROW_BLOCK * SUBLANES, LANES), lambda b, i: (b * nblk + i, 0))],
        out_shape=[jax.ShapeDtypeStruct((t, d), F32), jax.ShapeDtypeStruct((t, d), BF16),
                   jax.ShapeDtypeStruct((t * SUBLANES, LANES), F32)],
        scratch_shapes=[pltpu.VMEM((d, d), BF16)] * 2,
        compiler_params=_cparams("arbitrary", "arbitrary"),
        name="cross_attn",
    )(xb, xf, kv, w["wq"], w["wo"], p["ln_g1"], p["ln_b1"])


def _lanes_from_cols(cols, shape):
    lane = lax.broadcasted_iota(jnp.int32, shape, 1)
    out = jnp.zeros(shape, cols[0].dtype)
    for k, c in enumerate(cols):
        out = jnp.where(lane == k, c, out)
    return out


def _router_kernel(xb_ref, wr_ref, br_ref, gate_ref, lpos_ref, bcnt_ref, bbase_ref, bstart_ref, carry_ref):
    @pl.when(pl.program_id(0) == 0)
    def _():
        carry_ref[...] = jnp.zeros_like(carry_ref)

    m = xb_ref.shape[0]
    logits = _dot(xb_ref[...], wr_ref[...]) + br_ref[...]
    lane = lax.broadcasted_iota(jnp.int32, logits.shape, 1)
    tops, hots = [], []
    for _ in range(TOP_K):
        top = jnp.max(logits, axis=-1, keepdims=True)
        idx = jnp.min(jnp.where(logits == top, lane, LANES), axis=-1, keepdims=True)
        hot = lane == idx
        logits = jnp.where(hot, -jnp.inf, logits)
        tops.append(top)
        hots.append(hot)
    es = [jnp.exp(v - tops[0]) for v in tops]
    den = es[0] + es[1] + es[2] + es[3]

    hot_all = jnp.zeros(logits.shape, F32)
    for hot in hots:
        hot_all = hot_all + jnp.where(hot, 1.0, 0.0)
    r = lax.broadcasted_iota(jnp.int32, (m, m), 0)
    c = lax.broadcasted_iota(jnp.int32, (m, m), 1)
    before = jnp.where(c < r, 1.0, 0.0).astype(BF16)
    cnt = jnp.sum(hot_all, axis=0, keepdims=True)
    er = lax.broadcasted_iota(jnp.int32, (LANES, LANES), 0)
    ec = lax.broadcasted_iota(jnp.int32, (LANES, LANES), 1)
    lower_experts = jnp.where(er < ec, 1.0, 0.0).astype(BF16)
    lstart = _dot_exact_lhs(jnp.broadcast_to(cnt, (SUBLANES, LANES)), lower_experts)[0:1]
    local = lstart + _dot(before, hot_all.astype(BF16))
    lpos = [jnp.sum(jnp.where(hot, local, 0.0), axis=-1, keepdims=True) for hot in hots]

    half = jnp.bitwise_and(pl.program_id(0), 1).astype(F32) * PAIRS_PER_BLOCK
    lines = [(v + half) * SUBLANES for v in lpos]
    gate_ref[...] = _lanes_from_cols([e / den for e in es], logits.shape)
    lpos_ref[...] = _lanes_from_cols(lines, logits.shape).astype(jnp.int32)
    bcnt_ref[0] = cnt.astype(jnp.int32)
    bbase_ref[0] = carry_ref[...].astype(jnp.int32)
    bstart_ref[0] = lstart.astype(jnp.int32)
    carry_ref[...] = carry_ref[...] + cnt


def _router(xb, p):
    t, d = xb.shape
    nblk = t // ROUTE_BLOCK
    row = pl.BlockSpec((ROUTE_BLOCK, LANES), lambda i: (i, 0))
    one = pl.BlockSpec((1, LANES), lambda i: (0, 0))
    per_block = pl.BlockSpec((1, 1, LANES), lambda i: (i, 0, 0))
    per_block_shape = jax.ShapeDtypeStruct((nblk, 1, LANES), jnp.int32)
    return pl.pallas_call(
        _router_kernel,
        grid=(nblk,),
        in_specs=[pl.BlockSpec((ROUTE_BLOCK, d), lambda i: (i, 0)),
                  pl.BlockSpec((d, LANES), lambda i: (0, 0)), one],
        out_specs=[row, row, per_block, per_block, per_block],
        out_shape=[jax.ShapeDtypeStruct((t, LANES), F32), jax.ShapeDtypeStruct((t, LANES), jnp.int32),
                   per_block_shape, per_block_shape, per_block_shape],
        scratch_shapes=[pltpu.VMEM((1, LANES), F32)],
        compiler_params=_cparams("arbitrary"),
        name="router",
    )(xb, p["w_router"], p["b_router"])


def _rows(ref, row, n):
    return ref.at[pl.ds(pl.multiple_of(row * SUBLANES, SUBLANES), n * SUBLANES)]


def _range_copies(n, near_ref, near_row, far_hbm, far_row, sem, to_far):
    p = ROUTE_BLOCK
    while p >= 1:
        done = jnp.bitwise_and(n, -2 * p)

        @pl.when(jnp.bitwise_and(n, p) != 0)
        def _(p=p, done=done):
            near = _rows(near_ref, near_row + done, p)
            far = _rows(far_hbm, far_row + done, p)
            if to_far:
                pltpu.make_async_copy(near, far, sem).start()
            else:
                pltpu.make_async_copy(far, near, sem).start()

        p //= 2


def _stage_half(stage_ref, s):
    return _rows(stage_ref, s * PAIRS_PER_BLOCK, PAIRS_PER_BLOCK)


def _tile_at(ref, line):
    return ref.at[pl.ds(pl.multiple_of(line, SUBLANES), SUBLANES)]


def _dispatch_kernel(cnt_ref, lstart_ref, gstart_ref, fill_ref, lpos_ref, xg_ref, xs_hbm,
                     stage_ref, zero_ref, sem, zsem):
    b = pl.program_id(0)
    last = pl.num_programs(0) - 1
    slot = jnp.bitwise_and(b, 1)

    def wait_half(s):
        pltpu.make_async_copy(_stage_half(stage_ref, s), _rows(xs_hbm, 0, PAIRS_PER_BLOCK), sem.at[s]).wait()

    @pl.when(b == 0)
    def _():
        zero_ref[...] = jnp.zeros_like(zero_ref)

        def zfill(e, carry):
            copy = pltpu.make_async_copy(zero_ref, _rows(xs_hbm, fill_ref[e], EXPERT_TILE), zsem)
            copy.start()
            copy.wait()
            return carry

        lax.fori_loop(0, N_EXPERTS, zfill, 0)

        def tail_copy(i):
            return pltpu.make_async_copy(zero_ref, _rows(xs_hbm, i * EXPERT_TILE, EXPERT_TILE), zsem)

        def tail_start(i, carry):
            tail_copy(i).start()
            return carry

        def tail_wait(i, carry):
            tail_copy(i).wait()
            return carry

        n_all = xs_hbm.shape[0] // TILE_LINES
        lax.fori_loop(fill_ref[N_EXPERTS], n_all, tail_start, 0)
        lax.fori_loop(fill_ref[N_EXPERTS], n_all, tail_wait, 0)

    lines_of = [lpos_ref.at[pl.ds(k * ROUTE_BLOCK, ROUTE_BLOCK)] for k in range(TOP_K)]

    def fill(c, carry):
        for u in range(ROW_UNROLL):
            t = c * ROW_UNROLL + u
            tile = _tile_at(xg_ref, t * SUBLANES)[...]
            for k in range(TOP_K):
                _tile_at(stage_ref, lines_of[k][t])[...] = tile
        return carry

    lax.fori_loop(0, ROUTE_BLOCK // ROW_UNROLL, fill, 0)

    def ranges(e, carry):
        j = b * N_EXPERTS + e
        _range_copies(cnt_ref[j], stage_ref, slot * PAIRS_PER_BLOCK + lstart_ref[j], xs_hbm, gstart_ref[j],
                      sem.at[slot], True)
        return carry

    lax.fori_loop(0, N_EXPERTS, ranges, 0)

    @pl.when(b > 0)
    def _():
        wait_half(1 - slot)

    @pl.when(b == last)
    def _():
        wait_half(slot)


def _dispatch(xg, lpos_flat, cnt, lstart, gstart, fill_from, n_slots):
    block_lines = ROUTE_BLOCK * SUBLANES
    nblk = xg.shape[0] // block_lines
    grid_spec = pltpu.PrefetchScalarGridSpec(
        num_scalar_prefetch=4,
        grid=(nblk,),
        in_specs=[pl.BlockSpec((PAIRS_PER_BLOCK,), lambda b, *_: (b,), memory_space=pltpu.SMEM),
                  pl.BlockSpec((block_lines, LANES), lambda b, *_: (b, 0))],
        out_specs=pl.BlockSpec(memory_space=pl.ANY),
        scratch_shapes=[pltpu.VMEM((2 * PAIRS_PER_BLOCK * SUBLANES, LANES), F32),
                        pltpu.VMEM((TILE_LINES, LANES), F32),
                        pltpu.SemaphoreType.DMA((2,)), pltpu.SemaphoreType.DMA(())],
    )
    return pl.pallas_call(
        _dispatch_kernel,
        grid_spec=grid_spec,
        out_shape=jax.ShapeDtypeStruct((n_slots * SUBLANES, LANES), F32),
        compiler_params=_cparams("arbitrary"),
        name="moe_dispatch",
    )(cnt, lstart, gstart, fill_from, lpos_flat, xg)


def _expert_kernel(te_ref, nu_ref, x_ref, wgu_ref, bgu_ref, wd_ref, bd_ref, o_ref, wgub_ref, wdb_ref):
    i = pl.program_id(0)

    @pl.when(jnp.logical_or(i == 0, te_ref[i] != te_ref[jnp.maximum(i - 1, 0)]))
    def _():
        _cast_weight(wgub_ref, wgu_ref.at[0, 0])
        _cast_weight(wdb_ref, wd_ref.at[0, 0])

    @pl.when(i < nu_ref[0])
    def _():
        x = _from_token_tiles(x_ref, EXPERT_TILE).astype(BF16)
        hgu = _dot(x, wgub_ref[...]) + bgu_ref[0, 0]
        gate = jnp.minimum(hgu[:, :D_EXPERT], SWIGLU_LIMIT)
        up = jnp.clip(hgu[:, D_EXPERT:], -SWIGLU_LIMIT, SWIGLU_LIMIT)
        glu = gate * _sigmoid(SWIGLU_ALPHA * gate)
        _to_token_tiles(o_ref, _dot(((up + 1.0) * glu).astype(BF16), wdb_ref[...]) + bd_ref[0, 0])

    @pl.when(i >= nu_ref[0])
    def _():
        o_ref[...] = jnp.zeros_like(o_ref)


def _experts(xs, tile_expert, n_used, n_tiles, l, w):
    d = D_MODEL

    def of_expert(shape):
        return pl.BlockSpec((1, 1) + shape, lambda i, te, nu: (l, te[i], 0, 0))

    grid_spec = pltpu.PrefetchScalarGridSpec(
        num_scalar_prefetch=2,
        grid=(n_tiles,),
        in_specs=[
            pl.BlockSpec((TILE_LINES, LANES), lambda i, te, nu: (jnp.minimum(i, nu[0] - 1), 0)),
            of_expert((d, 2 * D_EXPERT)), of_expert((1, 2 * D_EXPERT)),
            of_expert((D_EXPERT, d)), of_expert((1, d)),
        ],
        out_specs=pl.BlockSpec((TILE_LINES, LANES), lambda i, te, nu: (i, 0)),
        scratch_shapes=[pltpu.VMEM((d, 2 * D_EXPERT), BF16), pltpu.VMEM((D_EXPERT, d), BF16)],
    )
    return pl.pallas_call(
        _expert_kernel,
        grid_spec=grid_spec,
        out_shape=jax.ShapeDtypeStruct((n_tiles * TILE_LINES, LANES), F32),
        compiler_params=_cparams("arbitrary"),
        name="moe_experts",
    )(tile_expert, n_used, xs, w["w_gu"], w["b_gu"], w["w_down"], w["b_down"])


def _combine_kernel(cnt_ref, lstart_ref, gstart_ref, lpos_ref, gate_ref, ys_hbm, x_ref, g_ref, b_ref,
                    of_ref, ob_ref, stage_ref, acc_ref, sem):
    b = pl.program_id(0)
    last = pl.num_programs(0) - 1
    slot = jnp.bitwise_and(b, 1)

    def fetch(blk, s):
        def ranges(e, carry):
            j = blk * N_EXPERTS + e
            _range_copies(cnt_ref[j], stage_ref, s * PAIRS_PER_BLOCK + lstart_ref[j], ys_hbm, gstart_ref[j],
                          sem.at[s], False)
            return carry

        lax.fori_loop(0, N_EXPERTS, ranges, 0)

    @pl.when(b == 0)
    def _():
        fetch(b, slot)

    @pl.when(b < last)
    def _():
        fetch(b + 1, 1 - slot)

    pltpu.make_async_copy(_rows(ys_hbm, 0, PAIRS_PER_BLOCK), _stage_half(stage_ref, slot), sem.at[slot]).wait()

    lines_of = [lpos_ref.at[pl.ds(k * ROUTE_BLOCK, ROUTE_BLOCK)] for k in range(TOP_K)]
    gates_of = [gate_ref.at[pl.ds(k * ROUTE_BLOCK, ROUTE_BLOCK)] for k in range(TOP_K)]

    def gather(c, carry):
        for u in range(ROW_UNROLL):
            t = c * ROW_UNROLL + u
            tile = jnp.zeros((SUBLANES, LANES), F32)
            for k in range(TOP_K):
                tile = tile + gates_of[k][t] * _tile_at(stage_ref, lines_of[k][t])[...]
            _tile_at(acc_ref, t * SUBLANES)[...] = tile
        return carry

    lax.fori_loop(0, ROUTE_BLOCK // ROW_UNROLL, gather, 0)

    y = _layer_norm(DN_ALPHA * x_ref[...] + _from_token_tiles(acc_ref, ROUTE_BLOCK), g_ref[...], b_ref[...])
    of_ref[...] = y
    ob_ref[...] = y.astype(BF16)


def _combine(ys, lpos_flat, gate_flat, cnt, lstart, gstart, xf, p):
    t, d = xf.shape
    nblk = t // ROUTE_BLOCK
    row = pl.BlockSpec((ROUTE_BLOCK, d), lambda b, *_: (b, 0))
    vec = pl.BlockSpec((1, d), lambda b, *_: (0, 0))
    pairs = pl.BlockSpec((PAIRS_PER_BLOCK,), lambda b, *_: (b,), memory_space=pltpu.SMEM)
    grid_spec = pltpu.PrefetchScalarGridSpec(
        num_scalar_prefetch=3,
        grid=(nblk,),
        in_specs=[pairs, pairs, pl.BlockSpec(memory_space=pl.ANY), row, vec, vec],
        out_specs=[row, row],
        scratch_shapes=[pltpu.VMEM((2 * PAIRS_PER_BLOCK * SUBLANES, LANES), F32),
                        pltpu.VMEM((ROUTE_BLOCK * SUBLANES, LANES), F32), pltpu.SemaphoreType.DMA((2,))],
    )
    return pl.pallas_call(
        _combine_kernel,
        grid_spec=grid_spec,
        out_shape=[jax.ShapeDtypeStruct((t, d), F32), jax.ShapeDtypeStruct((t, d), BF16)],
        compiler_params=_cparams("arbitrary"),
        name="moe_combine",
    )(cnt, lstart, gstart, lpos_flat, gate_flat, ys, xf, p["ln_g2"], p["ln_b2"])


def _moe(xb, xf, xg, l, w, p):
    t, d = xf.shape
    gates, lpos, bcnt, bbase, bstart = _router(xb, p)
    bcnt = bcnt[:, 0, :N_EXPERTS]
    counts = jnp.sum(bcnt, axis=0)
    padded = (counts + EXPERT_TILE - 1) // EXPERT_TILE * EXPERT_TILE
    ends = jnp.cumsum(padded)
    starts = ends - padded
    gstart = (starts[None, :] + bbase[:, 0, :N_EXPERTS]).reshape(-1)
    lstart = bstart[:, 0, :N_EXPERTS].reshape(-1)
    n_tiles = (t * TOP_K) // EXPERT_TILE + N_EXPERTS
    n_used = ends[-1] // EXPERT_TILE
    tile_start = jnp.minimum(jnp.arange(n_tiles, dtype=jnp.int32), n_used - 1) * EXPERT_TILE
    tile_expert = jnp.sum((ends[None, :] <= tile_start[:, None]).astype(jnp.int32), axis=1)
    def k_major(v):
        return v[:, :TOP_K].reshape(-1, ROUTE_BLOCK, TOP_K).transpose(0, 2, 1).reshape(-1)

    lpos_flat = k_major(lpos)
    gate_flat = k_major(gates)

    fill_from = jnp.concatenate([starts + counts, n_used.reshape(1)])
    n_slots = (n_tiles + 1) * EXPERT_TILE
    xs = _dispatch(xg, lpos_flat, bcnt.reshape(-1), lstart, gstart, fill_from, n_slots)
    ys = _experts(xs, tile_expert, n_used.reshape(1), n_tiles, l, w)
    return _combine(ys, lpos_flat, gate_flat, bcnt.reshape(-1), lstart, gstart, xf, p)


def _layer_params(l, w_in, conv_w, conv_b, dt_bias, a_log, d_skip, ssd_norm_w, gm_ln_g, gm_ln_b,
                  w_sp, b_sp, w_router, b_router, ln_g, ln_b):
    d = D_MODEL
    off_dt = d + conv_w.shape[-1]
    off_u = off_dt + SSD_HEADS
    pad_h = LANES - SSD_HEADS
    pad_e = LANES - N_EXPERTS
    head_of_channel = jnp.arange(d, dtype=jnp.int32) // SSD_HEAD_DIM
    shift_row = jnp.arange(CONV_K * CHUNK, dtype=jnp.int32)
    shift_col = CONV_TAIL + shift_row % CHUNK - shift_row // CHUNK
    return {
        "conv_shift": (jnp.arange(CONV_TAIL + CHUNK, dtype=jnp.int32)[None, :] == shift_col[:, None]).astype(BF16),
        "w_dt": jnp.pad(w_in[l, :, off_dt:off_u], ((0, 0), (0, pad_h))).astype(BF16),
        "conv_w": conv_w[l], "conv_b": conv_b[l].reshape(1, -1),
        "dt_bias": jnp.pad(dt_bias[l], (0, pad_h)).reshape(1, LANES),
        "a_log": jnp.pad(a_log[l], (0, pad_h)).reshape(1, LANES),
        "d_skip": d_skip[l][head_of_channel].reshape(1, d),
        "ssd_norm_w": ssd_norm_w[l].reshape(1, d),
        "gm_ln_g": gm_ln_g[l].reshape(1, d), "gm_ln_b": gm_ln_b[l].reshape(1, d),
        "w_sp": w_sp[l],
        "b_sp": jnp.repeat(b_sp[l].T, GM_GROUP_DIM, axis=1),
        "head_expand": (jnp.arange(LANES, dtype=jnp.int32)[:, None] == head_of_channel[None, :]).astype(BF16),
        "w_router": jnp.pad(w_router[l], ((0, 0), (0, pad_e))).astype(BF16),
        "b_router": jnp.pad(b_router[l], (0, pad_e), constant_values=NEG_BIG).reshape(1, LANES),
        "ln_g0": ln_g[l, 0].reshape(1, d), "ln_b0": ln_b[l, 0].reshape(1, d),
        "ln_g1": ln_g[l, 1].reshape(1, d), "ln_b1": ln_b[l, 1].reshape(1, d),
        "ln_g2": ln_g[l, 2].reshape(1, d), "ln_b2": ln_b[l, 2].reshape(1, d),
    }


def kernel(x, mem, ln0_g, ln0_b, w_in, conv_w, conv_b, dt_bias, a_log, d_skip, ssd_norm_w, gm_ln_g, gm_ln_b, w_sp, b_sp, p_ssd, p_gm, w_out, wq, wk, wv, wo, w_router, b_router, w_gu, b_gu, w_down, b_down, ln_g, ln_b):
    batch, seq, d = x.shape
    mem_len = mem.shape[1]
    depth = w_in.shape[0]
    assert d == D_MODEL and seq % ROW_BLOCK == 0 and seq % CHUNK == 0
    t = batch * seq
    memb = mem.reshape(batch * mem_len, d).astype(BF16)
    off_u = d + conv_w.shape[-1] + SSD_HEADS
    w = {"w_in_tail": w_in[:, :, off_u:],
         "w_kv": jnp.concatenate([wk, wv], axis=2),
         "p_ssd": p_ssd, "p_gm": p_gm, "w_out": w_out, "wq": wq, "wo": wo,
         "w_gu": w_gu, "b_gu": b_gu.reshape(depth, N_EXPERTS, 1, -1),
         "w_down": w_down, "b_down": b_down.reshape(depth, N_EXPERTS, 1, -1)}
    xf, xb = _entry_ln(x.reshape(t, d), ln0_g, ln0_b)
    for l in range(depth):
        p = _layer_params(l, w_in, conv_w, conv_b, dt_bias, a_log, d_skip, ssd_norm_w, gm_ln_g,
                          gm_ln_b, w_sp, b_sp, w_router, b_router, ln_g, ln_b)
        bm = min(MM_BLOCK_M, t)
        zxa = _matmul(xb, w_in, l, 3 * d, bm)
        zxb = _matmul(xb, w["w_in_tail"], l, 4 * d, bm)
        y_ssd, y_gm = _mixer(zxa, zxb, xb, p, batch, seq)
        xf, xb = _merge(y_ssd, y_gm, zxb, xf, l, w, p)
        kv = _matmul(memb, w["w_kv"], l, 2 * d, min(MM_BLOCK_M, batch * mem_len))
        xf, xb, xg = _cross_attn(xb, xf, kv, l, w, p, batch, seq, mem_len)
        xf, xb = _moe(xb, xf, xg, l, w, p)
    return xf.reshape(batch, seq, d)
```

```python
import math

import jax
import jax.numpy as jnp
from jax import lax
from jax.experimental import pallas as pl
from jax.experimental.pallas import tpu as pltpu

F32 = jnp.float32
BF16 = jnp.bfloat16

D_MODEL = 1024
DEPTH = 2
CHUNK = 128
SSD_HEADS = 16
SSD_HEAD_DIM = 64
SSD_GROUPS = 4
SSD_HPG = SSD_HEADS // SSD_GROUPS
SSD_STATE = 128
SSD_GROUP_W = SSD_HPG * SSD_HEAD_DIM
CONV_K = 4
CONV_TAIL = CHUNK
GM_GROUPS = 8
GM_GROUP_DIM = D_MODEL // GM_GROUPS
XA_HEADS = 4
XA_HEAD_DIM = D_MODEL // XA_HEADS
N_EXPERTS = 32
TOP_K = 4
D_EXPERT = D_MODEL
SWIGLU_LIMIT = 7.0
SWIGLU_ALPHA = 1.702
DN_ALPHA = (2 * DEPTH) ** 0.25
EPS = 1e-5

LANES = 128
SUBLANES = 8
VMEM_LIMIT = 56 * 1024 * 1024

ROW_BLOCK = 512
MM_BLOCK_M = 1024
MM_BLOCK_N = 1024
CAST_ROWS = 128
ROUTE_BLOCK = 512
EXPERT_TILE = 512
NEG_BIG = -1e30
TILE_LINES = EXPERT_TILE * SUBLANES
PAIRS_PER_BLOCK = ROUTE_BLOCK * TOP_K
ROW_UNROLL = 8


def _cparams(*sem):
    return pltpu.CompilerParams(dimension_semantics=sem, vmem_limit_bytes=VMEM_LIMIT)


def _layer_norm(x, g, b):
    mu = jnp.mean(x, axis=-1, keepdims=True)
    xc = x - mu
    var = jnp.mean(xc * xc, axis=-1, keepdims=True)
    return xc * lax.rsqrt(var + EPS) * g + b


def _dot(a, b):
    return jnp.dot(a, b, preferred_element_type=F32)


def _dot_nt(a, b):
    return lax.dot_general(a, b, (((1,), (1,)), ((), ())), preferred_element_type=F32)


def _dot_tn(a, b):
    return lax.dot_general(a, b, (((0,), (0,)), ((), ())), preferred_element_type=F32)


def _split3(v):
    hi = v.astype(BF16)
    r1 = v - hi.astype(F32)
    mid = r1.astype(BF16)
    lo = (r1 - mid.astype(F32)).astype(BF16)
    return hi, mid, lo


def _dot_exact_rhs(sel, v):
    hi, mid, lo = _split3(v)
    return _dot(sel, hi) + _dot(sel, mid) + _dot(sel, lo)


def _dot_exact_lhs(v, sel):
    hi, mid, lo = _split3(v)
    return _dot(hi, sel) + _dot(mid, sel) + _dot(lo, sel)


def _sigmoid(x):
    return 1.0 / (1.0 + jnp.exp(-x))


def _gelu(x):
    return 0.5 * x * (1.0 + lax.erf(x * math.sqrt(0.5)))


def _softplus(x):
    return jnp.maximum(x, 0.0) + jnp.log1p(jnp.exp(-jnp.abs(x)))


def _cast_weight(dst_ref, src_ref):
    def body(c, carry):
        rows = pl.ds(pl.multiple_of(c * CAST_ROWS, CAST_ROWS), CAST_ROWS)
        dst_ref[rows, :] = src_ref[rows, :].astype(BF16)
        return carry

    lax.fori_loop(0, src_ref.shape[0] // CAST_ROWS, body, 0)


def _layer_weight(l, k, n):
    return pl.BlockSpec((1, k, n), lambda *_: (l, 0, 0), pipeline_mode=pl.Buffered(1))


def _ln_kernel(x_ref, g_ref, b_ref, of_ref, ob_ref):
    y = _layer_norm(x_ref[...], g_ref[...], b_ref[...])
    of_ref[...] = y
    ob_ref[...] = y.astype(BF16)


def _entry_ln(x, g, b):
    t, d = x.shape
    row = pl.BlockSpec((ROW_BLOCK, d), lambda i: (i, 0))
    vec = pl.BlockSpec((1, d), lambda i: (0, 0))
    return pl.pallas_call(
        _ln_kernel,
        grid=(t // ROW_BLOCK,),
        in_specs=[row, vec, vec],
        out_specs=[row, row],
        out_shape=[jax.ShapeDtypeStruct((t, d), F32), jax.ShapeDtypeStruct((t, d), BF16)],
        compiler_params=_cparams("arbitrary"),
        name="entry_ln",
    )(x, g.reshape(1, d), b.reshape(1, d))


def _mm_kernel(a_ref, w_ref, o_ref, wb_ref):
    @pl.when(pl.program_id(1) == 0)
    def _():
        _cast_weight(wb_ref, w_ref.at[0])

    o_ref[...] = _dot(a_ref[...], wb_ref[...]).astype(o_ref.dtype)


def _matmul(a, w, l, n, bm):
    m, k = a.shape
    bn = MM_BLOCK_N
    return pl.pallas_call(
        _mm_kernel,
        grid=(n // bn, m // bm),
        in_specs=[pl.BlockSpec((bm, k), lambda j, i: (i, 0)),
                  pl.BlockSpec((1, k, bn), lambda j, i: (l, 0, j))],
        out_specs=pl.BlockSpec((bm, bn), lambda j, i: (i, j)),
        out_shape=jax.ShapeDtypeStruct((m, n), BF16),
        scratch_shapes=[pltpu.VMEM((k, bn), BF16)],
        compiler_params=_cparams("arbitrary", "arbitrary"),
        name="matmul",
    )(a, w)


def _conv_silu(raw, tail_ref, shift, w, b):
    aug = jnp.concatenate([tail_ref[...], raw], axis=0)
    shifted = _dot(shift, aug)
    acc = b
    for j in range(CONV_K):
        acc = acc + shifted[j * CHUNK:(j + 1) * CHUNK] * w[CONV_K - 1 - j:CONV_K - j]
    tail_ref[...] = raw[CHUNK - CONV_TAIL:CHUNK]
    return acc * _sigmoid(acc)


def _mixer_kernel(z_ref, xs_ref, bc_ref, u_ref, v_ref, xb_ref, wdt_ref, cw_ref, cb_ref, dtb_ref,
                  alog_ref, dskip_ref, nw_ref, lng_ref, lnb_ref, wsp_ref, bsp_ref, hexp_ref, shift_ref,
                  yssd_ref, ygm_ref, tailx_ref, tailbc_ref, state_ref):
    @pl.when(pl.program_id(1) == 0)
    def _():
        tailx_ref[...] = jnp.zeros_like(tailx_ref)
        tailbc_ref[...] = jnp.zeros_like(tailbc_ref)
        state_ref[...] = jnp.zeros_like(state_ref)

    row = lax.broadcasted_iota(jnp.int32, (CHUNK, CHUNK), 0)
    col = lax.broadcasted_iota(jnp.int32, (CHUNK, CHUNK), 1)
    causal = col <= row
    tri = jnp.where(causal, 1.0, 0.0).astype(BF16)

    cw = cw_ref[...]
    cb = cb_ref[...]
    shift = shift_ref[...]
    xs = _conv_silu(xs_ref[...], tailx_ref, shift, cw[:, :D_MODEL], cb[:, :D_MODEL])
    bc = _conv_silu(bc_ref[...], tailbc_ref, shift, cw[:, D_MODEL:], cb[:, D_MODEL:])
    gn = SSD_GROUPS * SSD_STATE

    dt = _softplus(_dot(xb_ref[...], wdt_ref[...]) + dtb_ref[...])
    a = -jnp.exp(alog_ref[...])
    cs = _dot_exact_rhs(tri, dt * a)
    cs_t = cs.T
    hexp = hexp_ref[...]
    dt_x = _dot_exact_lhs(dt, hexp)
    cs_x = _dot_exact_lhs(cs, hexp)
    tot_x = cs_x[CHUNK - 1:CHUNK]
    xdt = xs * dt_x
    xdt_b = xdt.astype(BF16)
    xdec_b = (xdt * jnp.exp(tot_x - cs_x)).astype(BF16)
    seg = lax.shift_right_logical(lax.broadcasted_iota(jnp.int32, (CHUNK, SSD_GROUP_W), 1),
                                  int(math.log2(SSD_HEAD_DIM)))

    y_parts = []
    for g in range(SSD_GROUPS):
        b_g = bc[:, g * SSD_STATE:(g + 1) * SSD_STATE].astype(BF16)
        c_g = bc[:, gn + g * SSD_STATE:gn + (g + 1) * SSD_STATE].astype(BF16)
        cols = slice(g * SSD_GROUP_W, (g + 1) * SSD_GROUP_W)
        cb_g = _dot_nt(c_g, b_g)
        st = state_ref[:, cols]
        y_g = _dot(c_g, st.astype(BF16)) * jnp.exp(cs_x[:, cols])
        for hh in range(SSD_HPG):
            h = g * SSD_HPG + hh
            diff = cs[:, h:h + 1] - cs_t[h:h + 1, :]
            m_h = (cb_g * jnp.exp(jnp.where(causal, diff, -jnp.inf))).astype(BF16)
            y_g = y_g + jnp.where(seg == hh, _dot(m_h, xdt_b[:, cols]), 0.0)
        state_ref[:, cols] = st * jnp.exp(tot_x[:, cols]) + _dot_tn(b_g, xdec_b[:, cols])
        y_parts.append(y_g)
    y = jnp.concatenate(y_parts, axis=1) + dskip_ref[...] * xs

    z = z_ref[...].astype(F32)
    y = y * (z * _sigmoid(z))
    n_parts = []
    for g in range(SSD_GROUPS):
        yg = y[:, g * SSD_GROUP_W:(g + 1) * SSD_GROUP_W]
        n_parts.append(yg * lax.rsqrt(jnp.mean(yg * yg, axis=-1, keepdims=True) + EPS))
    yssd_ref[...] = (jnp.concatenate(n_parts, axis=1) * nw_ref[...]).astype(BF16)

    u = _gelu(u_ref[...].astype(F32))
    v = _layer_norm(_gelu(v_ref[...].astype(F32)), lng_ref[...], lnb_ref[...]).astype(BF16)
    sv_parts = []
    for g in range(GM_GROUPS):
        w_g = jnp.where(causal, wsp_ref[g], 0.0).astype(BF16)
        sv_parts.append(_dot(w_g, v[:, g * GM_GROUP_DIM:(g + 1) * GM_GROUP_DIM]))
    ygm_ref[...] = (u * (jnp.concatenate(sv_parts, axis=1) + bsp_ref[...])).astype(BF16)


def _mixer(zxa, zxb, xb, p, batch, seq):
    t, d = xb.shape
    nchunk = seq // CHUNK

    def blk(j):
        return pl.BlockSpec((CHUNK, d), lambda b, c, j=j: (b * nchunk + c, j))

    def const(shape):
        return pl.BlockSpec(shape, lambda b, c: (0,) * len(shape))

    out = pl.BlockSpec((CHUNK, d), lambda b, c: (b * nchunk + c, 0))
    return pl.pallas_call(
        _mixer_kernel,
        grid=(batch, nchunk),
        in_specs=[blk(0), blk(1), blk(2), blk(0), blk(1),
                  pl.BlockSpec((CHUNK, d), lambda b, c: (b * nchunk + c, 0)),
                  const((d, LANES)), const((CONV_K, 2 * d)), const((1, 2 * d)), const((1, LANES)),
                  const((1, LANES)), const((1, d)), const((1, d)), const((1, d)), const((1, d)),
                  const((GM_GROUPS, CHUNK, CHUNK)), const((CHUNK, d)), const((LANES, d)),
                  const((CONV_K * CHUNK, CONV_TAIL + CHUNK))],
        out_specs=[out, out],
        out_shape=[jax.ShapeDtypeStruct((t, d), BF16), jax.ShapeDtypeStruct((t, d), BF16)],
        scratch_shapes=[pltpu.VMEM((CONV_TAIL, d), BF16), pltpu.VMEM((CONV_TAIL, d), BF16),
                        pltpu.VMEM((SSD_STATE, d), F32)],
        compiler_params=_cparams("arbitrary", "arbitrary"),
        name="mixer",
    )(zxa, zxa, zxa, zxb, zxb, xb, p["w_dt"], p["conv_w"], p["conv_b"], p["dt_bias"], p["a_log"],
      p["d_skip"], p["ssd_norm_w"], p["gm_ln_g"], p["gm_ln_b"], p["w_sp"], p["b_sp"], p["head_expand"],
      p["conv_shift"])


def _merge_kernel(ys_ref, yg_ref, gs_ref, gg_ref, x_ref, ps_ref, pg_ref, wo_ref, g_ref, b_ref,
                  of_ref, ob_ref, psb_ref, pgb_ref, wob_ref):
    @pl.when(pl.program_id(0) == 0)
    def _():
        _cast_weight(psb_ref, ps_ref.at[0])
        _cast_weight(pgb_ref, pg_ref.at[0])
        _cast_weight(wob_ref, wo_ref.at[0])

    h = (_sigmoid(gs_ref[...].astype(F32)) * _dot(ys_ref[...], psb_ref[...])
         + _sigmoid(gg_ref[...].astype(F32)) * _dot(yg_ref[...], pgb_ref[...]))
    mix = _dot(h.astype(BF16), wob_ref[...])
    y = _layer_norm(DN_ALPHA * x_ref[...] + mix, g_ref[...], b_ref[...])
    of_ref[...] = y
    ob_ref[...] = y.astype(BF16)


def _merge(y_ssd, y_gm, zxb, xf, l, w, p):
    t, d = xf.shape
    row = pl.BlockSpec((ROW_BLOCK, d), lambda i: (i, 0))
    mat = _layer_weight(l, d, d)
    vec = pl.BlockSpec((1, d), lambda i: (0, 0))
    return pl.pallas_call(
        _merge_kernel,
        grid=(t // ROW_BLOCK,),
        in_specs=[row, row, pl.BlockSpec((ROW_BLOCK, d), lambda i: (i, 2)),
                  pl.BlockSpec((ROW_BLOCK, d), lambda i: (i, 3)), row, mat, mat, mat, vec, vec],
        out_specs=[row, row],
        out_shape=[jax.ShapeDtypeStruct((t, d), F32), jax.ShapeDtypeStruct((t, d), BF16)],
        scratch_shapes=[pltpu.VMEM((d, d), BF16)] * 3,
        compiler_params=_cparams("arbitrary"),
        name="merge",
    )(y_ssd, y_gm, zxb, zxb, xf, w["p_ssd"], w["p_gm"], w["w_out"], p["ln_g0"], p["ln_b0"])


def _to_token_tiles(ref, y):
    m = y.shape[0]
    for j in range(SUBLANES):
        ref[pl.ds(j, m, stride=SUBLANES), :] = y[:, j * LANES:(j + 1) * LANES]


def _from_token_tiles(ref, m):
    return jnp.concatenate([ref[pl.ds(j, m, stride=SUBLANES), :] for j in range(SUBLANES)], axis=1)


def _attn_kernel(xb_ref, xf_ref, kv_ref, wq_ref, wo_ref, g_ref, b_ref, of_ref, ob_ref, og_ref,
                 wqb_ref, wob_ref):
    @pl.when(jnp.logical_and(pl.program_id(0) == 0, pl.program_id(1) == 0))
    def _():
        _cast_weight(wqb_ref, wq_ref.at[0])
        _cast_weight(wob_ref, wo_ref.at[0])

    q = _dot(xb_ref[...], wqb_ref[...]).astype(BF16)
    kv = kv_ref[...]
    outs = []
    for h in range(XA_HEADS):
        cols = slice(h * XA_HEAD_DIM, (h + 1) * XA_HEAD_DIM)
        s = _dot_nt(q[:, cols], kv[:, cols]) * (XA_HEAD_DIM ** -0.5)
        e = jnp.exp(s - jnp.max(s, axis=-1, keepdims=True))
        p = (e / jnp.sum(e, axis=-1, keepdims=True)).astype(BF16)
        outs.append(_dot(p, kv[:, D_MODEL + h * XA_HEAD_DIM:D_MODEL + (h + 1) * XA_HEAD_DIM]))
    o = jnp.concatenate(outs, axis=1).astype(BF16)
    y = _layer_norm(DN_ALPHA * xf_ref[...] + _dot(o, wob_ref[...]), g_ref[...], b_ref[...])
    of_ref[...] = y
    ob_ref[...] = y.astype(BF16)
    _to_token_tiles(og_ref, y)


def _cross_attn(xb, xf, kv, l, w, p, batch, seq, mem_len):
    t, d = xf.shape
    nblk = seq // ROW_BLOCK
    row = pl.BlockSpec((ROW_BLOCK, d), lambda b, i: (b * nblk + i, 0))
    mat = _layer_weight(l, d, d)
    vec = pl.BlockSpec((1, d), lambda b, i: (0, 0))
    return pl.pallas_call(
        _attn_kernel,
        grid=(batch, nblk),
        in_specs=[row, row, pl.BlockSpec((mem_len, 2 * d), lambda b, i: (b, 0)), mat, mat, vec, vec],
        out_specs=[row, row, pl.BlockSpec((ROW_BLOCK * SUBLANES, LANES), lambda b, i: (b * nblk + i, 0))],
        out_shape=[jax.ShapeDtypeStruct((t, d), F32), jax.ShapeDtypeStruct((t, d), BF16),
                   jax.ShapeDtypeStruct((t * SUBLANES, LANES), F32)],
        scratch_shapes=[pltpu.VMEM((d, d), BF16)] * 2,
        compiler_params=_cparams("arbitrary", "arbitrary"),
        name="cross_attn",
    )(xb, xf, kv, w["wq"], w["wo"], p["ln_g1"], p["ln_b1"])


def _rows_from(rows, shape):
    sub = lax.broadcasted_iota(jnp.int32, shape, 0)
    out = jnp.zeros(shape, rows[0].dtype)
    for k, r in enumerate(rows):
        out = jnp.where(sub == k, r, out)
    return out


def _col_to_row(col):
    return jnp.broadcast_to(col, (LANES, LANES)).T[0:1]


def _row_to_col(row):
    return jnp.broadcast_to(row, (LANES, LANES)).T[:, 0:1]


def _router_kernel(xb_ref, wr_ref, br_ref, upper_ref, gate_ref, lpos_ref, bcnt_ref, bbase_ref, bstart_ref,
                   carry_ref):
    @pl.when(pl.program_id(0) == 0)
    def _():
        carry_ref[...] = jnp.zeros_like(carry_ref)

    m = xb_ref.shape[0]
    logits = _dot(xb_ref[...], wr_ref[...]) + br_ref[...]
    lt = jnp.concatenate([logits[r:r + LANES].T for r in range(0, m, LANES)], axis=1)[:N_EXPERTS]
    sub = lax.broadcasted_iota(jnp.int32, lt.shape, 0)
    tops, hots = [], []
    for _ in range(TOP_K):
        top = jnp.max(lt, axis=0, keepdims=True)
        idx = jnp.min(jnp.where(lt == top, sub, N_EXPERTS), axis=0, keepdims=True)
        hot = sub == idx
        lt = jnp.where(hot, -jnp.inf, lt)
        tops.append(top)
        hots.append(hot)
    es = [jnp.exp(v - tops[0]) for v in tops]
    den = es[0] + es[1] + es[2] + es[3]

    hot_all = jnp.zeros(lt.shape, F32)
    for hot in hots:
        hot_all = hot_all + jnp.where(hot, 1.0, 0.0)
    cnt_col = jnp.sum(hot_all, axis=1, keepdims=True)
    cnt = _col_to_row(jnp.concatenate([cnt_col, jnp.zeros((LANES - N_EXPERTS, 1), F32)], axis=0))
    er = lax.broadcasted_iota(jnp.int32, (LANES, LANES), 0)
    ec = lax.broadcasted_iota(jnp.int32, (LANES, LANES), 1)
    lower_experts = jnp.where(er < ec, 1.0, 0.0).astype(BF16)
    lstart = _dot_exact_lhs(jnp.broadcast_to(cnt, (SUBLANES, LANES)), lower_experts)[0:1]
    local = _row_to_col(lstart)[:N_EXPERTS] + _dot(hot_all.astype(BF16), upper_ref[...])
    lpos = [jnp.sum(jnp.where(hot, local, 0.0), axis=0, keepdims=True) for hot in hots]

    half = jnp.bitwise_and(pl.program_id(0), 1).astype(F32) * PAIRS_PER_BLOCK
    lines = [(v + half) * SUBLANES for v in lpos]
    gate_ref[...] = _rows_from([e / den for e in es], gate_ref.shape)
    lpos_ref[...] = _rows_from(lines, lpos_ref.shape).astype(jnp.int32)
    bcnt_ref[0] = cnt.astype(jnp.int32)
    bbase_ref[0] = carry_ref[...].astype(jnp.int32)
    bstart_ref[0] = lstart.astype(jnp.int32)
    carry_ref[...] = carry_ref[...] + cnt


def _router(xb, p):
    t, d = xb.shape
    nblk = t // ROUTE_BLOCK
    per_token = pl.BlockSpec((SUBLANES, ROUTE_BLOCK), lambda i: (i, 0))
    one = pl.BlockSpec((1, LANES), lambda i: (0, 0))
    per_block = pl.BlockSpec((1, 1, LANES), lambda i: (i, 0, 0))
    per_block_shape = jax.ShapeDtypeStruct((nblk, 1, LANES), jnp.int32)
    tok = jnp.arange(ROUTE_BLOCK, dtype=jnp.int32)
    earlier = (tok[:, None] < tok[None, :]).astype(BF16)
    return pl.pallas_call(
        _router_kernel,
        grid=(nblk,),
        in_specs=[pl.BlockSpec((ROUTE_BLOCK, d), lambda i: (i, 0)),
                  pl.BlockSpec((d, LANES), lambda i: (0, 0)), one,
                  pl.BlockSpec((ROUTE_BLOCK, ROUTE_BLOCK), lambda i: (0, 0))],
        out_specs=[per_token, per_token, per_block, per_block, per_block],
        out_shape=[jax.ShapeDtypeStruct((nblk * SUBLANES, ROUTE_BLOCK), F32),
                   jax.ShapeDtypeStruct((nblk * SUBLANES, ROUTE_BLOCK), jnp.int32),
                   per_block_shape, per_block_shape, per_block_shape],
        scratch_shapes=[pltpu.VMEM((1, LANES), F32)],
        compiler_params=_cparams("arbitrary"),
        name="router",
    )(xb, p["w_router"], p["b_router"], earlier)


def _rows(ref, row, n):
    return ref.at[pl.ds(pl.multiple_of(row * SUBLANES, SUBLANES), n * SUBLANES)]


def _range_copies(n, near_ref, near_row, far_hbm, far_row, sem, to_far, wait=False, same_near=False):
    p = ROUTE_BLOCK
    while p >= 1:
        done = jnp.bitwise_and(n, -2 * p)

        @pl.when(jnp.bitwise_and(n, p) != 0)
        def _(p=p, done=done):
            near = _rows(near_ref, near_row if same_near else near_row + done, p)
            far = _rows(far_hbm, far_row + done, p)
            copy = pltpu.make_async_copy(near, far, sem) if to_far else pltpu.make_async_copy(far, near, sem)
            copy.wait() if wait else copy.start()

        p //= 2


def _stage_half(stage_ref, s):
    return _rows(stage_ref, s * PAIRS_PER_BLOCK, PAIRS_PER_BLOCK)


def _tile_at(ref, line):
    return ref.at[pl.ds(pl.multiple_of(line, SUBLANES), SUBLANES)]


def _dispatch_kernel(cnt_ref, lstart_ref, gstart_ref, fill_ref, lpos_ref, xg_ref, xs_hbm,
                     stage_ref, zero_ref, sem, zsem):
    b = pl.program_id(0)
    last = pl.num_programs(0) - 1
    slot = jnp.bitwise_and(b, 1)

    def wait_half(s):
        pltpu.make_async_copy(_stage_half(stage_ref, s), _rows(xs_hbm, 0, PAIRS_PER_BLOCK), sem.at[s]).wait()

    @pl.when(b == 0)
    def _():
        zero_ref[...] = jnp.zeros_like(zero_ref)

        def zfill(e, wait):
            _range_copies(fill_ref[N_EXPERTS + 1 + e], zero_ref, 0, xs_hbm, fill_ref[e], zsem, True,
                          wait=wait, same_near=True)

        def zfill_start(e, carry):
            zfill(e, False)
            return carry

        def zfill_wait(e, carry):
            zfill(e, True)
            return carry

        lax.fori_loop(0, N_EXPERTS, zfill_start, 0)
        lax.fori_loop(0, N_EXPERTS, zfill_wait, 0)

        def tail_copy(i):
            return pltpu.make_async_copy(zero_ref, _rows(xs_hbm, i * EXPERT_TILE, EXPERT_TILE), zsem)

        def tail_start(i, carry):
            tail_copy(i).start()
            return carry

        def tail_wait(i, carry):
            tail_copy(i).wait()
            return carry

        n_all = xs_hbm.shape[0] // TILE_LINES
        lax.fori_loop(fill_ref[N_EXPERTS], n_all, tail_start, 0)
        lax.fori_loop(fill_ref[N_EXPERTS], n_all, tail_wait, 0)

    lines_of = [lpos_ref.at[pl.ds(k * ROUTE_BLOCK, ROUTE_BLOCK)] for k in range(TOP_K)]

    def fill(c, carry):
        for u in range(ROW_UNROLL):
            t = c * ROW_UNROLL + u
            tile = _tile_at(xg_ref, t * SUBLANES)[...]
            for k in range(TOP_K):
                _tile_at(stage_ref, lines_of[k][t])[...] = tile
        return carry

    lax.fori_loop(0, ROUTE_BLOCK // ROW_UNROLL, fill, 0)

    def ranges(e, carry):
        j = b * N_EXPERTS + e
        _range_copies(cnt_ref[j], stage_ref, slot * PAIRS_PER_BLOCK + lstart_ref[j], xs_hbm, gstart_ref[j],
                      sem.at[slot], True)
        return carry

    lax.fori_loop(0, N_EXPERTS, ranges, 0)

    @pl.when(b > 0)
    def _():
        wait_half(1 - slot)

    @pl.when(b == last)
    def _():
        wait_half(slot)


def _dispatch(xg, lpos_flat, cnt, lstart, gstart, fill_from, n_slots):
    block_lines = ROUTE_BLOCK * SUBLANES
    nblk = xg.shape[0] // block_lines
    grid_spec = pltpu.PrefetchScalarGridSpec(
        num_scalar_prefetch=4,
        grid=(nblk,),
        in_specs=[pl.BlockSpec((PAIRS_PER_BLOCK,), lambda b, *_: (b,), memory_space=pltpu.SMEM),
                  pl.BlockSpec((block_lines, LANES), lambda b, *_: (b, 0))],
        out_specs=pl.BlockSpec(memory_space=pl.ANY),
        scratch_shapes=[pltpu.VMEM((2 * PAIRS_PER_BLOCK * SUBLANES, LANES), F32),
                        pltpu.VMEM((TILE_LINES, LANES), F32),
                        pltpu.SemaphoreType.DMA((2,)), pltpu.SemaphoreType.DMA(())],
    )
    return pl.pallas_call(
        _dispatch_kernel,
        grid_spec=grid_spec,
        out_shape=jax.ShapeDtypeStruct((n_slots * SUBLANES, LANES), F32),
        compiler_params=_cparams("arbitrary"),
        name="moe_dispatch",
    )(cnt, lstart, gstart, fill_from, lpos_flat, xg)


def _expert_kernel(te_ref, nu_ref, x_ref, wgu_ref, bgu_ref, wd_ref, bd_ref, o_ref, wgub_ref, wdb_ref):
    i = pl.program_id(0)

    @pl.when(jnp.logical_or(i == 0, te_ref[i] != te_ref[jnp.maximum(i - 1, 0)]))
    def _():
        _cast_weight(wgub_ref, wgu_ref.at[0, 0])
        _cast_weight(wdb_ref, wd_ref.at[0, 0])

    @pl.when(i < nu_ref[0])
    def _():
        x = _from_token_tiles(x_ref, EXPERT_TILE).astype(BF16)
        hgu = _dot(x, wgub_ref[...]) + bgu_ref[0, 0]
        gate = jnp.minimum(hgu[:, :D_EXPERT], SWIGLU_LIMIT)
        up = jnp.clip(hgu[:, D_EXPERT:], -SWIGLU_LIMIT, SWIGLU_LIMIT)
        glu = gate * _sigmoid(SWIGLU_ALPHA * gate)
        _to_token_tiles(o_ref, _dot(((up + 1.0) * glu).astype(BF16), wdb_ref[...]) + bd_ref[0, 0])

    @pl.when(i >= nu_ref[0])
    def _():
        o_ref[...] = jnp.zeros_like(o_ref)


def _experts(xs, tile_expert, n_used, n_tiles, l, w):
    d = D_MODEL

    def of_expert(shape):
        return pl.BlockSpec((1, 1) + shape, lambda i, te, nu: (l, te[i], 0, 0))

    grid_spec = pltpu.PrefetchScalarGridSpec(
        num_scalar_prefetch=2,
        grid=(n_tiles,),
        in_specs=[
            pl.BlockSpec((TILE_LINES, LANES), lambda i, te, nu: (jnp.minimum(i, nu[0] - 1), 0)),
            of_expert((d, 2 * D_EXPERT)), of_expert((1, 2 * D_EXPERT)),
            of_expert((D_EXPERT, d)), of_expert((1, d)),
        ],
        out_specs=pl.BlockSpec((TILE_LINES, LANES), lambda i, te, nu: (i, 0)),
        scratch_shapes=[pltpu.VMEM((d, 2 * D_EXPERT), BF16), pltpu.VMEM((D_EXPERT, d), BF16)],
    )
    return pl.pallas_call(
        _expert_kernel,
        grid_spec=grid_spec,
        out_shape=jax.ShapeDtypeStruct((n_tiles * TILE_LINES, LANES), F32),
        compiler_params=_cparams("arbitrary"),
        name="moe_experts",
    )(tile_expert, n_used, xs, w["w_gu"], w["b_gu"], w["w_down"], w["b_down"])


def _combine_kernel(cnt_ref, lstart_ref, gstart_ref, lpos_ref, gate_ref, ys_hbm, x_ref, g_ref, b_ref,
                    of_ref, ob_ref, stage_ref, acc_ref, sem):
    b = pl.program_id(0)
    last = pl.num_programs(0) - 1
    slot = jnp.bitwise_and(b, 1)

    def fetch(blk, s):
        def ranges(e, carry):
            j = blk * N_EXPERTS + e
            _range_copies(cnt_ref[j], stage_ref, s * PAIRS_PER_BLOCK + lstart_ref[j], ys_hbm, gstart_ref[j],
                          sem.at[s], False)
            return carry

        lax.fori_loop(0, N_EXPERTS, ranges, 0)

    @pl.when(b == 0)
    def _():
        fetch(b, slot)

    @pl.when(b < last)
    def _():
        fetch(b + 1, 1 - slot)

    pltpu.make_async_copy(_rows(ys_hbm, 0, PAIRS_PER_BLOCK), _stage_half(stage_ref, slot), sem.at[slot]).wait()

    lines_of = [lpos_ref.at[pl.ds(k * ROUTE_BLOCK, ROUTE_BLOCK)] for k in range(TOP_K)]
    gates_of = [gate_ref.at[pl.ds(k * ROUTE_BLOCK, ROUTE_BLOCK)] for k in range(TOP_K)]

    def gather(c, carry):
        for u in range(ROW_UNROLL):
            t = c * ROW_UNROLL + u
            tile = jnp.zeros((SUBLANES, LANES), F32)
            for k in range(TOP_K):
                tile = tile + gates_of[k][t] * _tile_at(stage_ref, lines_of[k][t])[...]
            _tile_at(acc_ref, t * SUBLANES)[...] = tile
        return carry

    lax.fori_loop(0, ROUTE_BLOCK // ROW_UNROLL, gather, 0)

    y = _layer_norm(DN_ALPHA * x_ref[...] + _from_token_tiles(acc_ref, ROUTE_BLOCK), g_ref[...], b_ref[...])
    of_ref[...] = y
    ob_ref[...] = y.astype(BF16)


def _combine(ys, lpos_flat, gate_flat, cnt, lstart, gstart, xf, p):
    t, d = xf.shape
    nblk = t // ROUTE_BLOCK
    row = pl.BlockSpec((ROUTE_BLOCK, d), lambda b, *_: (b, 0))
    vec = pl.BlockSpec((1, d), lambda b, *_: (0, 0))
    pairs = pl.BlockSpec((PAIRS_PER_BLOCK,), lambda b, *_: (b,), memory_space=pltpu.SMEM)
    grid_spec = pltpu.PrefetchScalarGridSpec(
        num_scalar_prefetch=3,
        grid=(nblk,),
        in_specs=[pairs, pairs, pl.BlockSpec(memory_space=pl.ANY), row, vec, vec],
        out_specs=[row, row],
        scratch_shapes=[pltpu.VMEM((2 * PAIRS_PER_BLOCK * SUBLANES, LANES), F32),
                        pltpu.VMEM((ROUTE_BLOCK * SUBLANES, LANES), F32), pltpu.SemaphoreType.DMA((2,))],
    )
    return pl.pallas_call(
        _combine_kernel,
        grid_spec=grid_spec,
        out_shape=[jax.ShapeDtypeStruct((t, d), F32), jax.ShapeDtypeStruct((t, d), BF16)],
        compiler_params=_cparams("arbitrary"),
        name="moe_combine",
    )(cnt, lstart, gstart, lpos_flat, gate_flat, ys, xf, p["ln_g2"], p["ln_b2"])


def _moe(xb, xf, xg, l, w, p):
    t, d = xf.shape
    gates, lpos, bcnt, bbase, bstart = _router(xb, p)
    bcnt = bcnt[:, 0, :N_EXPERTS]
    counts = jnp.sum(bcnt, axis=0)
    padded = (counts + EXPERT_TILE - 1) // EXPERT_TILE * EXPERT_TILE
    ends = jnp.cumsum(padded)
    starts = ends - padded
    gstart = (starts[None, :] + bbase[:, 0, :N_EXPERTS]).reshape(-1)
    lstart = bstart[:, 0, :N_EXPERTS].reshape(-1)
    n_tiles = (t * TOP_K) // EXPERT_TILE + N_EXPERTS
    n_used = ends[-1] // EXPERT_TILE
    tile_start = jnp.minimum(jnp.arange(n_tiles, dtype=jnp.int32), n_used - 1) * EXPERT_TILE
    tile_expert = jnp.sum((ends[None, :] <= tile_start[:, None]).astype(jnp.int32), axis=1)
    def k_major(v):
        return v.reshape(-1, SUBLANES, ROUTE_BLOCK)[:, :TOP_K].reshape(-1)

    lpos_flat = k_major(lpos)
    gate_flat = k_major(gates)

    fill_from = jnp.concatenate([starts + counts, n_used.reshape(1), padded - counts])
    n_slots = n_tiles * EXPERT_TILE
    xs = _dispatch(xg, lpos_flat, bcnt.reshape(-1), lstart, gstart, fill_from, n_slots)
    ys = _experts(xs, tile_expert, n_used.reshape(1), n_tiles, l, w)
    return _combine(ys, lpos_flat, gate_flat, bcnt.reshape(-1), lstart, gstart, xf, p)


def _layer_params(l, w_in, conv_w, conv_b, dt_bias, a_log, d_skip, ssd_norm_w, gm_ln_g, gm_ln_b,
                  w_sp, b_sp, w_router, b_router, ln_g, ln_b):
    d = D_MODEL
    off_dt = d + conv_w.shape[-1]
    off_u = off_dt + SSD_HEADS
    pad_h = LANES - SSD_HEADS
    pad_e = LANES - N_EXPERTS
    head_of_channel = jnp.arange(d, dtype=jnp.int32) // SSD_HEAD_DIM
    shift_row = jnp.arange(CONV_K * CHUNK, dtype=jnp.int32)
    shift_col = CONV_TAIL + shift_row % CHUNK - shift_row // CHUNK
    return {
        "conv_shift": (jnp.arange(CONV_TAIL + CHUNK, dtype=jnp.int32)[None, :] == shift_col[:, None]).astype(BF16),
        "w_dt": jnp.pad(w_in[l, :, off_dt:off_u], ((0, 0), (0, pad_h))).astype(BF16),
        "conv_w": conv_w[l], "conv_b": conv_b[l].reshape(1, -1),
        "dt_bias": jnp.pad(dt_bias[l], (0, pad_h)).reshape(1, LANES),
        "a_log": jnp.pad(a_log[l], (0, pad_h)).reshape(1, LANES),
        "d_skip": d_skip[l][head_of_channel].reshape(1, d),
        "ssd_norm_w": ssd_norm_w[l].reshape(1, d),
        "gm_ln_g": gm_ln_g[l].reshape(1, d), "gm_ln_b": gm_ln_b[l].reshape(1, d),
        "w_sp": w_sp[l],
        "b_sp": jnp.repeat(b_sp[l].T, GM_GROUP_DIM, axis=1),
        "head_expand": (jnp.arange(LANES, dtype=jnp.int32)[:, None] == head_of_channel[None, :]).astype(BF16),
        "w_router": jnp.pad(w_router[l], ((0, 0), (0, pad_e))).astype(BF16),
        "b_router": jnp.pad(b_router[l], (0, pad_e), constant_values=NEG_BIG).reshape(1, LANES),
        "ln_g0": ln_g[l, 0].reshape(1, d), "ln_b0": ln_b[l, 0].reshape(1, d),
        "ln_g1": ln_g[l, 1].reshape(1, d), "ln_b1": ln_b[l, 1].reshape(1, d),
        "ln_g2": ln_g[l, 2].reshape(1, d), "ln_b2": ln_b[l, 2].reshape(1, d),
    }


def kernel(x, mem, ln0_g, ln0_b, w_in, conv_w, conv_b, dt_bias, a_log, d_skip, ssd_norm_w, gm_ln_g, gm_ln_b, w_sp, b_sp, p_ssd, p_gm, w_out, wq, wk, wv, wo, w_router, b_router, w_gu, b_gu, w_down, b_down, ln_g, ln_b):
    batch, seq, d = x.shape
    mem_len = mem.shape[1]
    depth = w_in.shape[0]
    assert d == D_MODEL and seq % ROW_BLOCK == 0 and seq % CHUNK == 0
    t = batch * seq
    memb = mem.reshape(batch * mem_len, d).astype(BF16)
    off_u = d + conv_w.shape[-1] + SSD_HEADS
    w = {"w_in_tail": w_in[:, :, off_u:],
         "w_kv": jnp.concatenate([wk, wv], axis=2),
         "p_ssd": p_ssd, "p_gm": p_gm, "w_out": w_out, "wq": wq, "wo": wo,
         "w_gu": w_gu, "b_gu": b_gu.reshape(depth, N_EXPERTS, 1, -1),
         "w_down": w_down, "b_down": b_down.reshape(depth, N_EXPERTS, 1, -1)}
    xf, xb = _entry_ln(x.reshape(t, d), ln0_g, ln0_b)
    for l in range(depth):
        p = _layer_params(l, w_in, conv_w, conv_b, dt_bias, a_log, d_skip, ssd_norm_w, gm_ln_g,
                          gm_ln_b, w_sp, b_sp, w_router, b_router, ln_g, ln_b)
        bm = min(MM_BLOCK_M, t)
        zxa = _matmul(xb, w_in, l, 3 * d, bm)
        zxb = _matmul(xb, w["w_in_tail"], l, 4 * d, bm)
        y_ssd, y_gm = _mixer(zxa, zxb, xb, p, batch, seq)
        xf, xb = _merge(y_ssd, y_gm, zxb, xf, l, w, p)
        kv = _matmul(memb, w["w_kv"], l, 2 * d, min(MM_BLOCK_M, batch * mem_len))
        xf, xb, xg = _cross_attn(xb, xf, kv, l, w, p, batch, seq, mem_len)
        xf, xb = _moe(xb, xf, xg, l, w, p)
    return xf.reshape(batch, seq, d)
```

```python
import math

import jax
import jax.numpy as jnp
from jax import lax
from jax.experimental import pallas as pl
from jax.experimental.pallas import tpu as pltpu

F32 = jnp.float32
BF16 = jnp.bfloat16

D_MODEL = 1024
DEPTH = 2
CHUNK = 128
SSD_HEADS = 16
SSD_HEAD_DIM = 64
SSD_GROUPS = 4
SSD_HPG = SSD_HEADS // SSD_GROUPS
SSD_STATE = 128
SSD_GROUP_W = SSD_HPG * SSD_HEAD_DIM
CONV_K = 4
CONV_TAIL = CHUNK
GM_GROUPS = 8
GM_GROUP_DIM = D_MODEL // GM_GROUPS
XA_HEADS = 4
XA_HEAD_DIM = D_MODEL // XA_HEADS
N_EXPERTS = 32
TOP_K = 4
D_EXPERT = D_MODEL
SWIGLU_LIMIT = 7.0
SWIGLU_ALPHA = 1.702
DN_ALPHA = (2 * DEPTH) ** 0.25
EPS = 1e-5

LANES = 128
SUBLANES = 8
VMEM_LIMIT = 56 * 1024 * 1024

ROW_BLOCK = 512
MM_BLOCK_M = 1024
MM_BLOCK_N = 1024
CAST_ROWS = 128
ROUTE_BLOCK = 512
EXPERT_TILE = 512
NEG_BIG = -1e30
TILE_LINES = EXPERT_TILE * SUBLANES
PAIRS_PER_BLOCK = ROUTE_BLOCK * TOP_K
ROW_UNROLL = 8


def _cparams(*sem):
    return pltpu.CompilerParams(dimension_semantics=sem, vmem_limit_bytes=VMEM_LIMIT)


def _layer_norm(x, g, b):
    mu = jnp.mean(x, axis=-1, keepdims=True)
    xc = x - mu
    var = jnp.mean(xc * xc, axis=-1, keepdims=True)
    return xc * lax.rsqrt(var + EPS) * g + b


def _dot(a, b):
    return jnp.dot(a, b, preferred_element_type=F32)


def _dot_nt(a, b):
    return lax.dot_general(a, b, (((1,), (1,)), ((), ())), preferred_element_type=F32)


def _dot_tn(a, b):
    return lax.dot_general(a, b, (((0,), (0,)), ((), ())), preferred_element_type=F32)


def _split3(v):
    hi = v.astype(BF16)
    r1 = v - hi.astype(F32)
    mid = r1.astype(BF16)
    lo = (r1 - mid.astype(F32)).astype(BF16)
    return hi, mid, lo


def _dot_exact_rhs(sel, v):
    hi, mid, lo = _split3(v)
    return _dot(sel, hi) + _dot(sel, mid) + _dot(sel, lo)


def _dot_exact_lhs(v, sel):
    hi, mid, lo = _split3(v)
    return _dot(hi, sel) + _dot(mid, sel) + _dot(lo, sel)


def _dot_select_lhs(v, sel):
    hi = v.astype(BF16)
    lo = (v - hi.astype(F32)).astype(BF16)
    return _dot(hi, sel) + _dot(lo, sel)


def _sigmoid(x):
    return 1.0 / (1.0 + jnp.exp(-x))


def _gelu(x):
    return 0.5 * x * (1.0 + lax.erf(x * math.sqrt(0.5)))


def _softplus(x):
    return jnp.maximum(x, 0.0) + jnp.log1p(jnp.exp(-jnp.abs(x)))


def _cast_weight(dst_ref, src_ref):
    def body(c, carry):
        rows = pl.ds(pl.multiple_of(c * CAST_ROWS, CAST_ROWS), CAST_ROWS)
        dst_ref[rows, :] = src_ref[rows, :].astype(BF16)
        return carry

    lax.fori_loop(0, src_ref.shape[0] // CAST_ROWS, body, 0)


def _layer_weight(l, k, n):
    return pl.BlockSpec((1, k, n), lambda *_: (l, 0, 0), pipeline_mode=pl.Buffered(1))


def _ln_kernel(x_ref, g_ref, b_ref, of_ref, ob_ref):
    y = _layer_norm(x_ref[...], g_ref[...], b_ref[...])
    of_ref[...] = y
    ob_ref[...] = y.astype(BF16)


def _entry_ln(x, g, b):
    t, d = x.shape
    row = pl.BlockSpec((ROW_BLOCK, d), lambda i: (i, 0))
    vec = pl.BlockSpec((1, d), lambda i: (0, 0))
    return pl.pallas_call(
        _ln_kernel,
        grid=(t // ROW_BLOCK,),
        in_specs=[row, vec, vec],
        out_specs=[row, row],
        out_shape=[jax.ShapeDtypeStruct((t, d), F32), jax.ShapeDtypeStruct((t, d), BF16)],
        compiler_params=_cparams("arbitrary"),
        name="entry_ln",
    )(x, g.reshape(1, d), b.reshape(1, d))


def _mm_kernel(a_ref, w_ref, o_ref, wb_ref):
    @pl.when(pl.program_id(1) == 0)
    def _():
        _cast_weight(wb_ref, w_ref.at[0])

    o_ref[...] = _dot(a_ref[...], wb_ref[...]).astype(o_ref.dtype)


def _matmul(a, w, l, n, bm):
    m, k = a.shape
    bn = MM_BLOCK_N
    return pl.pallas_call(
        _mm_kernel,
        grid=(n // bn, m // bm),
        in_specs=[pl.BlockSpec((bm, k), lambda j, i: (i, 0)),
                  pl.BlockSpec((1, k, bn), lambda j, i: (l, 0, j))],
        out_specs=pl.BlockSpec((bm, bn), lambda j, i: (i, j)),
        out_shape=jax.ShapeDtypeStruct((m, n), BF16),
        scratch_shapes=[pltpu.VMEM((k, bn), BF16)],
        compiler_params=_cparams("arbitrary", "arbitrary"),
        name="matmul",
    )(a, w)


def _conv_silu(raw_ref, tail_ref, cols, shift, w, b):
    raw = raw_ref[:, cols]
    aug = jnp.concatenate([tail_ref[:, cols], raw], axis=0)
    shifted = _dot(shift, aug)
    acc = raw.astype(F32) * w[CONV_K - 1:CONV_K] + b
    for j in range(1, CONV_K):
        acc = acc + shifted[(j - 1) * CHUNK:j * CHUNK] * w[CONV_K - 1 - j:CONV_K - j]
    tail_ref[:, cols] = raw[CHUNK - CONV_TAIL:CHUNK]
    return acc * _sigmoid(acc)


def _mixer_kernel(z_ref, xs_ref, bc_ref, u_ref, v_ref, xb_ref, wdt_ref, cw_ref, cb_ref, dtb_ref,
                  alog_ref, dskip_ref, nw_ref, lng_ref, lnb_ref, wsp_ref, bsp_ref, hexp_ref, shift_ref,
                  yssd_ref, ygm_ref, tailx_ref, tailbc_ref, state_ref):
    @pl.when(pl.program_id(1) == 0)
    def _():
        tailx_ref[...] = jnp.zeros_like(tailx_ref)
        tailbc_ref[...] = jnp.zeros_like(tailbc_ref)
        state_ref[...] = jnp.zeros_like(state_ref)

    row = lax.broadcasted_iota(jnp.int32, (CHUNK, CHUNK), 0)
    col = lax.broadcasted_iota(jnp.int32, (CHUNK, CHUNK), 1)
    causal = col <= row
    tri = jnp.where(causal, 1.0, 0.0).astype(BF16)

    cw = cw_ref[...]
    cb = cb_ref[...]
    shift = shift_ref[...]
    bc = _conv_silu(bc_ref, tailbc_ref, slice(0, D_MODEL), shift, cw[:, D_MODEL:], cb[:, D_MODEL:])
    gn = SSD_GROUPS * SSD_STATE

    dt = _softplus(_dot(xb_ref[...], wdt_ref[...]) + dtb_ref[...])
    a = -jnp.exp(alog_ref[...])
    cs = _dot_exact_rhs(tri, dt * a)
    cs_t = cs.T
    seg = lax.shift_right_logical(lax.broadcasted_iota(jnp.int32, (CHUNK, SSD_GROUP_W), 1),
                                  int(math.log2(SSD_HEAD_DIM)))

    for g in range(SSD_GROUPS):
        cols = slice(g * SSD_GROUP_W, (g + 1) * SSD_GROUP_W)
        xs = _conv_silu(xs_ref, tailx_ref, cols, shift, cw[:, cols], cb[:, cols])
        hexp = hexp_ref[:, cols]
        dt_x = _dot_select_lhs(dt, hexp)
        cs_x = _dot_select_lhs(cs, hexp)
        tot_x = cs_x[CHUNK - 1:CHUNK]
        xdt = xs * dt_x
        xdt_b = xdt.astype(BF16)
        xdec_b = (xdt * jnp.exp(tot_x - cs_x)).astype(BF16)

        b_g = bc[:, g * SSD_STATE:(g + 1) * SSD_STATE].astype(BF16)
        c_g = bc[:, gn + g * SSD_STATE:gn + (g + 1) * SSD_STATE].astype(BF16)
        cb_g = _dot_nt(c_g, b_g)
        st = state_ref[:, cols]
        y_g = _dot(c_g, st.astype(BF16)) * jnp.exp(cs_x)
        for hh in range(SSD_HPG):
            h = g * SSD_HPG + hh
            diff = cs[:, h:h + 1] - cs_t[h:h + 1, :]
            m_h = (cb_g * jnp.exp(jnp.where(causal, diff, -jnp.inf))).astype(BF16)
            y_g = y_g + jnp.where(seg == hh, _dot(m_h, xdt_b), 0.0)
        state_ref[:, cols] = st * jnp.exp(tot_x) + _dot_tn(b_g, xdec_b)

        z = z_ref[:, cols].astype(F32)
        y_g = (y_g + dskip_ref[:, cols] * xs) * (z * _sigmoid(z))
        y_g = y_g * lax.rsqrt(jnp.mean(y_g * y_g, axis=-1, keepdims=True) + EPS)
        yssd_ref[:, cols] = (y_g * nw_ref[:, cols]).astype(BF16)

    u = _gelu(u_ref[...].astype(F32))
    v = _layer_norm(_gelu(v_ref[...].astype(F32)), lng_ref[...], lnb_ref[...]).astype(BF16)
    sv_parts = []
    for g in range(GM_GROUPS):
        w_g = jnp.where(causal, wsp_ref[g], 0.0).astype(BF16)
        sv_parts.append(_dot(w_g, v[:, g * GM_GROUP_DIM:(g + 1) * GM_GROUP_DIM]))
    ygm_ref[...] = (u * (jnp.concatenate(sv_parts, axis=1) + bsp_ref[...])).astype(BF16)


def _mixer(zxa, zxb, xb, p, batch, seq):
    t, d = xb.shape
    nchunk = seq // CHUNK

    def blk(j):
        return pl.BlockSpec((CHUNK, d), lambda b, c, j=j: (b * nchunk + c, j))

    def const(shape):
        return pl.BlockSpec(shape, lambda b, c: (0,) * len(shape))

    out = pl.BlockSpec((CHUNK, d), lambda b, c: (b * nchunk + c, 0))
    return pl.pallas_call(
        _mixer_kernel,
        grid=(batch, nchunk),
        in_specs=[blk(0), blk(1), blk(2), blk(0), blk(1),
                  pl.BlockSpec((CHUNK, d), lambda b, c: (b * nchunk + c, 0)),
                  const((d, LANES)), const((CONV_K, 2 * d)), const((1, 2 * d)), const((1, LANES)),
                  const((1, LANES)), const((1, d)), const((1, d)), const((1, d)), const((1, d)),
                  const((GM_GROUPS, CHUNK, CHUNK)), const((CHUNK, d)), const((LANES, d)),
                  const(((CONV_K - 1) * CHUNK, CONV_TAIL + CHUNK))],
        out_specs=[out, out],
        out_shape=[jax.ShapeDtypeStruct((t, d), BF16), jax.ShapeDtypeStruct((t, d), BF16)],
        scratch_shapes=[pltpu.VMEM((CONV_TAIL, d), BF16), pltpu.VMEM((CONV_TAIL, d), BF16),
                        pltpu.VMEM((SSD_STATE, d), F32)],
        compiler_params=_cparams("arbitrary", "arbitrary"),
        name="mixer",
    )(zxa, zxa, zxa, zxb, zxb, xb, p["w_dt"], p["conv_w"], p["conv_b"], p["dt_bias"], p["a_log"],
      p["d_skip"], p["ssd_norm_w"], p["gm_ln_g"], p["gm_ln_b"], p["w_sp"], p["b_sp"], p["head_expand"],
      p["conv_shift"])


def _merge_kernel(ys_ref, yg_ref, gs_ref, gg_ref, x_ref, ps_ref, pg_ref, wo_ref, g_ref, b_ref,
                  of_ref, ob_ref, psb_ref, pgb_ref, wob_ref):
    @pl.when(pl.program_id(0) == 0)
    def _():
        _cast_weight(psb_ref, ps_ref.at[0])
        _cast_weight(pgb_ref, pg_ref.at[0])
        _cast_weight(wob_ref, wo_ref.at[0])

    h = (_sigmoid(gs_ref[...].astype(F32)) * _dot(ys_ref[...], psb_ref[...])
         + _sigmoid(gg_ref[...].astype(F32)) * _dot(yg_ref[...], pgb_ref[...]))
    mix = _dot(h.astype(BF16), wob_ref[...])
    y = _layer_norm(DN_ALPHA * x_ref[...] + mix, g_ref[...], b_ref[...])
    of_ref[...] = y
    ob_ref[...] = y.astype(BF16)


def _merge(y_ssd, y_gm, zxb, xf, l, w, p):
    t, d = xf.shape
    row = pl.BlockSpec((ROW_BLOCK, d), lambda i: (i, 0))
    mat = _layer_weight(l, d, d)
    vec = pl.BlockSpec((1, d), lambda i: (0, 0))
    return pl.pallas_call(
        _merge_kernel,
        grid=(t // ROW_BLOCK,),
        in_specs=[row, row, pl.BlockSpec((ROW_BLOCK, d), lambda i: (i, 2)),
                  pl.BlockSpec((ROW_BLOCK, d), lambda i: (i, 3)), row, mat, mat, mat, vec, vec],
        out_specs=[row, row],
        out_shape=[jax.ShapeDtypeStruct((t, d), F32), jax.ShapeDtypeStruct((t, d), BF16)],
        scratch_shapes=[pltpu.VMEM((d, d), BF16)] * 3,
        compiler_params=_cparams("arbitrary"),
        name="merge",
    )(y_ssd, y_gm, zxb, zxb, xf, w["p_ssd"], w["p_gm"], w["w_out"], p["ln_g0"], p["ln_b0"])


def _to_token_tiles(ref, y):
    m = y.shape[0]
    for j in range(SUBLANES):
        ref[pl.ds(j, m, stride=SUBLANES), :] = y[:, j * LANES:(j + 1) * LANES]


def _from_token_tiles(ref, m):
    return jnp.concatenate([ref[pl.ds(j, m, stride=SUBLANES), :] for j in range(SUBLANES)], axis=1)


def _attn_kernel(xb_ref, xf_ref, kv_ref, wq_ref, wo_ref, g_ref, b_ref, of_ref, ob_ref, og_ref,
                 wqb_ref, wob_ref):
    @pl.when(jnp.logical_and(pl.program_id(0) == 0, pl.program_id(1) == 0))
    def _():
        _cast_weight(wqb_ref, wq_ref.at[0])
        _cast_weight(wob_ref, wo_ref.at[0])

    q = _dot(xb_ref[...], wqb_ref[...]).astype(BF16)
    kv = kv_ref[...]
    outs = []
    for h in range(XA_HEADS):
        cols = slice(h * XA_HEAD_DIM, (h + 1) * XA_HEAD_DIM)
        s = _dot_nt(q[:, cols], kv[:, cols]) * (XA_HEAD_DIM ** -0.5)
        e = jnp.exp(s - jnp.max(s, axis=-1, keepdims=True))
        p = (e / jnp.sum(e, axis=-1, keepdims=True)).astype(BF16)
        outs.append(_dot(p, kv[:, D_MODEL + h * XA_HEAD_DIM:D_MODEL + (h + 1) * XA_HEAD_DIM]))
    o = jnp.concatenate(outs, axis=1).astype(BF16)
    y = _layer_norm(DN_ALPHA * xf_ref[...] + _dot(o, wob_ref[...]), g_ref[...], b_ref[...])
    of_ref[...] = y
    ob_ref[...] = y.astype(BF16)
    _to_token_tiles(og_ref, y)


def _cross_attn(xb, xf, kv, l, w, p, batch, seq, mem_len):
    t, d = xf.shape
    nblk = seq // ROW_BLOCK
    row = pl.BlockSpec((ROW_BLOCK, d), lambda b, i: (b * nblk + i, 0))
    mat = _layer_weight(l, d, d)
    vec = pl.BlockSpec((1, d), lambda b, i: (0, 0))
    return pl.pallas_call(
        _attn_kernel,
        grid=(batch, nblk),
        in_specs=[row, row, pl.BlockSpec((mem_len, 2 * d), lambda b, i: (b, 0)), mat, mat, vec, vec],
        out_specs=[row, row, pl.BlockSpec((ROW_BLOCK * SUBLANES, LANES), lambda b, i: (b * nblk + i, 0))],
        out_shape=[jax.ShapeDtypeStruct((t, d), F32), jax.ShapeDtypeStruct((t, d), BF16),
                   jax.ShapeDtypeStruct((t * SUBLANES, LANES), F32)],
        scratch_shapes=[pltpu.VMEM((d, d), BF16)] * 2,
        compiler_params=_cparams("arbitrary", "arbitrary"),
        name="cross_attn",
    )(xb, xf, kv, w["wq"], w["wo"], p["ln_g1"], p["ln_b1"])


def _rows_from(rows, shape):
    sub = lax.broadcasted_iota(jnp.int32, shape, 0)
    out = jnp.zeros(shape, rows[0].dtype)
    for k, r in enumerate(rows):
        out = jnp.where(sub == k, r, out)
    return out


def _col_to_row(col):
    return jnp.broadcast_to(col, (LANES, LANES)).T[0:1]


def _row_to_col(row):
    return jnp.broadcast_to(row, (LANES, LANES)).T[:, 0:1]


def _router_kernel(xb_ref, wr_ref, br_ref, upper_ref, gate_ref, lpos_ref, bcnt_ref, bbase_ref, bstart_ref,
                   carry_ref):
    @pl.when(pl.program_id(0) == 0)
    def _():
        carry_ref[...] = jnp.zeros_like(carry_ref)

    m = xb_ref.shape[0]
    logits = _dot(xb_ref[...], wr_ref[...]) + br_ref[...]
    lt = jnp.concatenate([logits[r:r + LANES].T for r in range(0, m, LANES)], axis=1)[:N_EXPERTS]
    sub = lax.broadcasted_iota(jnp.int32, lt.shape, 0)
    tops, hots = [], []
    for _ in range(TOP_K):
        top = jnp.max(lt, axis=0, keepdims=True)
        idx = jnp.min(jnp.where(lt == top, sub, N_EXPERTS), axis=0, keepdims=True)
        hot = sub == idx
        lt = jnp.where(hot, -jnp.inf, lt)
        tops.append(top)
        hots.append(hot)
    es = [jnp.exp(v - tops[0]) for v in tops]
    den = es[0] + es[1] + es[2] + es[3]

    hot_all = jnp.zeros(lt.shape, F32)
    for hot in hots:
        hot_all = hot_all + jnp.where(hot, 1.0, 0.0)
    cnt_col = jnp.sum(hot_all, axis=1, keepdims=True)
    cnt = _col_to_row(jnp.concatenate([cnt_col, jnp.zeros((LANES - N_EXPERTS, 1), F32)], axis=0))
    er = lax.broadcasted_iota(jnp.int32, (LANES, LANES), 0)
    ec = lax.broadcasted_iota(jnp.int32, (LANES, LANES), 1)
    lower_experts = jnp.where(er < ec, 1.0, 0.0).astype(BF16)
    lstart = _dot_exact_lhs(jnp.broadcast_to(cnt, (SUBLANES, LANES)), lower_experts)[0:1]
    local = _row_to_col(lstart)[:N_EXPERTS] + _dot(hot_all.astype(BF16), upper_ref[...])
    lpos = [jnp.sum(jnp.where(hot, local, 0.0), axis=0, keepdims=True) for hot in hots]

    half = jnp.bitwise_and(pl.program_id(0), 1).astype(F32) * PAIRS_PER_BLOCK
    lines = [(v + half) * SUBLANES for v in lpos]
    gate_ref[...] = _rows_from([e / den for e in es], gate_ref.shape)
    lpos_ref[...] = _rows_from(lines, lpos_ref.shape).astype(jnp.int32)
    bcnt_ref[0] = cnt.astype(jnp.int32)
    bbase_ref[0] = carry_ref[...].astype(jnp.int32)
    bstart_ref[0] = lstart.astype(jnp.int32)
    carry_ref[...] = carry_ref[...] + cnt


def _router(xb, p):
    t, d = xb.shape
    nblk = t // ROUTE_BLOCK
    per_token = pl.BlockSpec((SUBLANES, ROUTE_BLOCK), lambda i: (i, 0))
    one = pl.BlockSpec((1, LANES), lambda i: (0, 0))
    per_block = pl.BlockSpec((1, 1, LANES), lambda i: (i, 0, 0))
    per_block_shape = jax.ShapeDtypeStruct((nblk, 1, LANES), jnp.int32)
    tok = jnp.arange(ROUTE_BLOCK, dtype=jnp.int32)
    earlier = (tok[:, None] < tok[None, :]).astype(BF16)
    return pl.pallas_call(
        _router_kernel,
        grid=(nblk,),
        in_specs=[pl.BlockSpec((ROUTE_BLOCK, d), lambda i: (i, 0)),
                  pl.BlockSpec((d, LANES), lambda i: (0, 0)), one,
                  pl.BlockSpec((ROUTE_BLOCK, ROUTE_BLOCK), lambda i: (0, 0))],
        out_specs=[per_token, per_token, per_block, per_block, per_block],
        out_shape=[jax.ShapeDtypeStruct((nblk * SUBLANES, ROUTE_BLOCK), F32),
                   jax.ShapeDtypeStruct((nblk * SUBLANES, ROUTE_BLOCK), jnp.int32),
                   per_block_shape, per_block_shape, per_block_shape],
        scratch_shapes=[pltpu.VMEM((1, LANES), F32)],
        compiler_params=_cparams("arbitrary"),
        name="router",
    )(xb, p["w_router"], p["b_router"], earlier)


def _rows(ref, row, n):
    return ref.at[pl.ds(pl.multiple_of(row * SUBLANES, SUBLANES), n * SUBLANES)]


def _range_copies(n, near_ref, near_row, far_hbm, far_row, sem, to_far, wait=False, same_near=False):
    p = ROUTE_BLOCK
    while p >= 1:
        done = jnp.bitwise_and(n, -2 * p)

        @pl.when(jnp.bitwise_and(n, p) != 0)
        def _(p=p, done=done):
            near = _rows(near_ref, near_row if same_near else near_row + done, p)
            far = _rows(far_hbm, far_row + done, p)
            copy = pltpu.make_async_copy(near, far, sem) if to_far else pltpu.make_async_copy(far, near, sem)
            copy.wait() if wait else copy.start()

        p //= 2


def _stage_half(stage_ref, s):
    return _rows(stage_ref, s * PAIRS_PER_BLOCK, PAIRS_PER_BLOCK)


def _tile_at(ref, line):
    return ref.at[pl.ds(pl.multiple_of(line, SUBLANES), SUBLANES)]


def _dispatch_kernel(cnt_ref, lstart_ref, gstart_ref, fill_ref, lpos_ref, xg_ref, xs_hbm,
                     stage_ref, zero_ref, sem, zsem):
    b = pl.program_id(0)
    last = pl.num_programs(0) - 1
    slot = jnp.bitwise_and(b, 1)

    def wait_half(s):
        pltpu.make_async_copy(_stage_half(stage_ref, s), _rows(xs_hbm, 0, PAIRS_PER_BLOCK), sem.at[s]).wait()

    @pl.when(b == 0)
    def _():
        zero_ref[...] = jnp.zeros_like(zero_ref)

        def zfill(e, wait):
            _range_copies(fill_ref[N_EXPERTS + 1 + e], zero_ref, 0, xs_hbm, fill_ref[e], zsem, True,
                          wait=wait, same_near=True)

        def zfill_start(e, carry):
            zfill(e, False)
            return carry

        def zfill_wait(e, carry):
            zfill(e, True)
            return carry

        lax.fori_loop(0, N_EXPERTS, zfill_start, 0)
        lax.fori_loop(0, N_EXPERTS, zfill_wait, 0)

        def tail_copy(i):
            return pltpu.make_async_copy(zero_ref, _rows(xs_hbm, i * EXPERT_TILE, EXPERT_TILE), zsem)

        def tail_start(i, carry):
            tail_copy(i).start()
            return carry

        def tail_wait(i, carry):
            tail_copy(i).wait()
            return carry

        n_all = xs_hbm.shape[0] // TILE_LINES
        lax.fori_loop(fill_ref[N_EXPERTS], n_all, tail_start, 0)
        lax.fori_loop(fill_ref[N_EXPERTS], n_all, tail_wait, 0)

    lines_of = [lpos_ref.at[pl.ds(k * ROUTE_BLOCK, ROUTE_BLOCK)] for k in range(TOP_K)]

    def fill(c, carry):
        for u in range(ROW_UNROLL):
            t = c * ROW_UNROLL + u
            tile = _tile_at(xg_ref, t * SUBLANES)[...]
            for k in range(TOP_K):
                _tile_at(stage_ref, lines_of[k][t])[...] = tile
        return carry

    lax.fori_loop(0, ROUTE_BLOCK // ROW_UNROLL, fill, 0)

    def ranges(e, carry):
        j = b * N_EXPERTS + e
        _range_copies(cnt_ref[j], stage_ref, slot * PAIRS_PER_BLOCK + lstart_ref[j], xs_hbm, gstart_ref[j],
                      sem.at[slot], True)
        return carry

    lax.fori_loop(0, N_EXPERTS, ranges, 0)

    @pl.when(b > 0)
    def _():
        wait_half(1 - slot)

    @pl.when(b == last)
    def _():
        wait_half(slot)


def _dispatch(xg, lpos_flat, cnt, lstart, gstart, fill_from, n_slots):
    block_lines = ROUTE_BLOCK * SUBLANES
    nblk = xg.shape[0] // block_lines
    grid_spec = pltpu.PrefetchScalarGridSpec(
        num_scalar_prefetch=4,
        grid=(nblk,),
        in_specs=[pl.BlockSpec((PAIRS_PER_BLOCK,), lambda b, *_: (b,), memory_space=pltpu.SMEM),
                  pl.BlockSpec((block_lines, LANES), lambda b, *_: (b, 0))],
        out_specs=pl.BlockSpec(memory_space=pl.ANY),
        scratch_shapes=[pltpu.VMEM((2 * PAIRS_PER_BLOCK * SUBLANES, LANES), F32),
                        pltpu.VMEM((TILE_LINES, LANES), F32),
                        pltpu.SemaphoreType.DMA((2,)), pltpu.SemaphoreType.DMA(())],
    )
    return pl.pallas_call(
        _dispatch_kernel,
        grid_spec=grid_spec,
        out_shape=jax.ShapeDtypeStruct((n_slots * SUBLANES, LANES), F32),
        compiler_params=_cparams("arbitrary"),
        name="moe_dispatch",
    )(cnt, lstart, gstart, fill_from, lpos_flat, xg)


def _expert_kernel(te_ref, nu_ref, x_ref, wgu_ref, bgu_ref, wd_ref, bd_ref, o_ref, wgub_ref, wdb_ref):
    i = pl.program_id(0)

    @pl.when(jnp.logical_or(i == 0, te_ref[i] != te_ref[jnp.maximum(i - 1, 0)]))
    def _():
        _cast_weight(wgub_ref, wgu_ref.at[0, 0])
        _cast_weight(wdb_ref, wd_ref.at[0, 0])

    @pl.when(i < nu_ref[0])
    def _():
        x = _from_token_tiles(x_ref, EXPERT_TILE).astype(BF16)
        hgu = _dot(x, wgub_ref[...]) + bgu_ref[0, 0]
        gate = jnp.minimum(hgu[:, :D_EXPERT], SWIGLU_LIMIT)
        up = jnp.clip(hgu[:, D_EXPERT:], -SWIGLU_LIMIT, SWIGLU_LIMIT)
        glu = gate * _sigmoid(SWIGLU_ALPHA * gate)
        _to_token_tiles(o_ref, _dot(((up + 1.0) * glu).astype(BF16), wdb_ref[...]) + bd_ref[0, 0])

    @pl.when(i >= nu_ref[0])
    def _():
        o_ref[...] = jnp.zeros_like(o_ref)


def _experts(xs, tile_expert, n_used, n_tiles, l, w):
    d = D_MODEL

    def of_expert(shape):
        return pl.BlockSpec((1, 1) + shape, lambda i, te, nu: (l, te[i], 0, 0))

    grid_spec = pltpu.PrefetchScalarGridSpec(
        num_scalar_prefetch=2,
        grid=(n_tiles,),
        in_specs=[
            pl.BlockSpec((TILE_LINES, LANES), lambda i, te, nu: (jnp.minimum(i, nu[0] - 1), 0)),
            of_expert((d, 2 * D_EXPERT)), of_expert((1, 2 * D_EXPERT)),
            of_expert((D_EXPERT, d)), of_expert((1, d)),
        ],
        out_specs=pl.BlockSpec((TILE_LINES, LANES), lambda i, te, nu: (i, 0)),
        scratch_shapes=[pltpu.VMEM((d, 2 * D_EXPERT), BF16), pltpu.VMEM((D_EXPERT, d), BF16)],
    )
    return pl.pallas_call(
        _expert_kernel,
        grid_spec=grid_spec,
        out_shape=jax.ShapeDtypeStruct((n_tiles * TILE_LINES, LANES), F32),
        compiler_params=_cparams("arbitrary"),
        name="moe_experts",
    )(tile_expert, n_used, xs, w["w_gu"], w["b_gu"], w["w_down"], w["b_down"])


def _combine_kernel(cnt_ref, lstart_ref, gstart_ref, lpos_ref, gate_ref, ys_hbm, x_ref, g_ref, b_ref,
                    of_ref, ob_ref, stage_ref, acc_ref, sem):
    b = pl.program_id(0)
    last = pl.num_programs(0) - 1
    slot = jnp.bitwise_and(b, 1)

    def fetch(blk, s):
        def ranges(e, carry):
            j = blk * N_EXPERTS + e
            _range_copies(cnt_ref[j], stage_ref, s * PAIRS_PER_BLOCK + lstart_ref[j], ys_hbm, gstart_ref[j],
                          sem.at[s], False)
            return carry

        lax.fori_loop(0, N_EXPERTS, ranges, 0)

    @pl.when(b == 0)
    def _():
        fetch(b, slot)

    @pl.when(b < last)
    def _():
        fetch(b + 1, 1 - slot)

    pltpu.make_async_copy(_rows(ys_hbm, 0, PAIRS_PER_BLOCK), _stage_half(stage_ref, slot), sem.at[slot]).wait()

    lines_of = [lpos_ref.at[pl.ds(k * ROUTE_BLOCK, ROUTE_BLOCK)] for k in range(TOP_K)]
    gates_of = [gate_ref.at[pl.ds(k * ROUTE_BLOCK, ROUTE_BLOCK)] for k in range(TOP_K)]

    def gather(c, carry):
        for u in range(ROW_UNROLL):
            t = c * ROW_UNROLL + u
            tile = jnp.zeros((SUBLANES, LANES), F32)
            for k in range(TOP_K):
                tile = tile + gates_of[k][t] * _tile_at(stage_ref, lines_of[k][t])[...]
            _tile_at(acc_ref, t * SUBLANES)[...] = tile
        return carry

    lax.fori_loop(0, ROUTE_BLOCK // ROW_UNROLL, gather, 0)

    y = _layer_norm(DN_ALPHA * x_ref[...] + _from_token_tiles(acc_ref, ROUTE_BLOCK), g_ref[...], b_ref[...])
    of_ref[...] = y
    ob_ref[...] = y.astype(BF16)


def _combine(ys, lpos_flat, gate_flat, cnt, lstart, gstart, xf, p):
    t, d = xf.shape
    nblk = t // ROUTE_BLOCK
    row = pl.BlockSpec((ROUTE_BLOCK, d), lambda b, *_: (b, 0))
    vec = pl.BlockSpec((1, d), lambda b, *_: (0, 0))
    pairs = pl.BlockSpec((PAIRS_PER_BLOCK,), lambda b, *_: (b,), memory_space=pltpu.SMEM)
    grid_spec = pltpu.PrefetchScalarGridSpec(
        num_scalar_prefetch=3,
        grid=(nblk,),
        in_specs=[pairs, pairs, pl.BlockSpec(memory_space=pl.ANY), row, vec, vec],
        out_specs=[row, row],
        scratch_shapes=[pltpu.VMEM((2 * PAIRS_PER_BLOCK * SUBLANES, LANES), F32),
                        pltpu.VMEM((ROUTE_BLOCK * SUBLANES, LANES), F32), pltpu.SemaphoreType.DMA((2,))],
    )
    return pl.pallas_call(
        _combine_kernel,
        grid_spec=grid_spec,
        out_shape=[jax.ShapeDtypeStruct((t, d), F32), jax.ShapeDtypeStruct((t, d), BF16)],
        compiler_params=_cparams("arbitrary"),
        name="moe_combine",
    )(cnt, lstart, gstart, lpos_flat, gate_flat, ys, xf, p["ln_g2"], p["ln_b2"])


def _moe(xb, xf, xg, l, w, p):
    t, d = xf.shape
    gates, lpos, bcnt, bbase, bstart = _router(xb, p)
    bcnt = bcnt[:, 0, :N_EXPERTS]
    counts = jnp.sum(bcnt, axis=0)
    padded = (counts + EXPERT_TILE - 1) // EXPERT_TILE * EXPERT_TILE
    ends = jnp.cumsum(padded)
    starts = ends - padded
    gstart = (starts[None, :] + bbase[:, 0, :N_EXPERTS]).reshape(-1)
    lstart = bstart[:, 0, :N_EXPERTS].reshape(-1)
    n_tiles = (t * TOP_K) // EXPERT_TILE + N_EXPERTS
    n_used = ends[-1] // EXPERT_TILE
    tile_start = jnp.minimum(jnp.arange(n_tiles, dtype=jnp.int32), n_used - 1) * EXPERT_TILE
    tile_expert = jnp.sum((ends[None, :] <= tile_start[:, None]).astype(jnp.int32), axis=1)
    def k_major(v):
        return v.reshape(-1, SUBLANES, ROUTE_BLOCK)[:, :TOP_K].reshape(-1)

    lpos_flat = k_major(lpos)
    gate_flat = k_major(gates)

    fill_from = jnp.concatenate([starts + counts, n_used.reshape(1), padded - counts])
    n_slots = n_tiles * EXPERT_TILE
    xs = _dispatch(xg, lpos_flat, bcnt.reshape(-1), lstart, gstart, fill_from, n_slots)
    ys = _experts(xs, tile_expert, n_used.reshape(1), n_tiles, l, w)
    return _combine(ys, lpos_flat, gate_flat, bcnt.reshape(-1), lstart, gstart, xf, p)


def _layer_params(l, w_in, conv_w, conv_b, dt_bias, a_log, d_skip, ssd_norm_w, gm_ln_g, gm_ln_b,
                  w_sp, b_sp, w_router, b_router, ln_g, ln_b):
    d = D_MODEL
    off_dt = d + conv_w.shape[-1]
    off_u = off_dt + SSD_HEADS
    pad_h = LANES - SSD_HEADS
    pad_e = LANES - N_EXPERTS
    head_of_channel = jnp.arange(d, dtype=jnp.int32) // SSD_HEAD_DIM
    shift_row = jnp.arange((CONV_K - 1) * CHUNK, dtype=jnp.int32)
    shift_col = CONV_TAIL + shift_row % CHUNK - (1 + shift_row // CHUNK)
    return {
        "conv_shift": (jnp.arange(CONV_TAIL + CHUNK, dtype=jnp.int32)[None, :] == shift_col[:, None]).astype(BF16),
        "w_dt": jnp.pad(w_in[l, :, off_dt:off_u], ((0, 0), (0, pad_h))).astype(BF16),
        "conv_w": conv_w[l], "conv_b": conv_b[l].reshape(1, -1),
        "dt_bias": jnp.pad(dt_bias[l], (0, pad_h)).reshape(1, LANES),
        "a_log": jnp.pad(a_log[l], (0, pad_h)).reshape(1, LANES),
        "d_skip": d_skip[l][head_of_channel].reshape(1, d),
        "ssd_norm_w": ssd_norm_w[l].reshape(1, d),
        "gm_ln_g": gm_ln_g[l].reshape(1, d), "gm_ln_b": gm_ln_b[l].reshape(1, d),
        "w_sp": w_sp[l],
        "b_sp": jnp.repeat(b_sp[l].T, GM_GROUP_DIM, axis=1),
        "head_expand": (jnp.arange(LANES, dtype=jnp.int32)[:, None] == head_of_channel[None, :]).astype(BF16),
        "w_router": jnp.pad(w_router[l], ((0, 0), (0, pad_e))).astype(BF16),
        "b_router": jnp.pad(b_router[l], (0, pad_e), constant_values=NEG_BIG).reshape(1, LANES),
        "ln_g0": ln_g[l, 0].reshape(1, d), "ln_b0": ln_b[l, 0].reshape(1, d),
        "ln_g1": ln_g[l, 1].reshape(1, d), "ln_b1": ln_b[l, 1].reshape(1, d),
        "ln_g2": ln_g[l, 2].reshape(1, d), "ln_b2": ln_b[l, 2].reshape(1, d),
    }


def kernel(x, mem, ln0_g, ln0_b, w_in, conv_w, conv_b, dt_bias, a_log, d_skip, ssd_norm_w, gm_ln_g, gm_ln_b, w_sp, b_sp, p_ssd, p_gm, w_out, wq, wk, wv, wo, w_router, b_router, w_gu, b_gu, w_down, b_down, ln_g, ln_b):
    batch, seq, d = x.shape
    mem_len = mem.shape[1]
    depth = w_in.shape[0]
    assert d == D_MODEL and seq % ROW_BLOCK == 0 and seq % CHUNK == 0
    t = batch * seq
    memb = mem.reshape(batch * mem_len, d).astype(BF16)
    off_u = d + conv_w.shape[-1] + SSD_HEADS
    w = {"w_in_tail": w_in[:, :, off_u:],
         "w_kv": jnp.concatenate([wk, wv], axis=2),
         "p_ssd": p_ssd, "p_gm": p_gm, "w_out": w_out, "wq": wq, "wo": wo,
         "w_gu": w_gu, "b_gu": b_gu.reshape(depth, N_EXPERTS, 1, -1),
         "w_down": w_down, "b_down": b_down.reshape(depth, N_EXPERTS, 1, -1)}
    xf, xb = _entry_ln(x.reshape(t, d), ln0_g, ln0_b)
    for l in range(depth):
        p = _layer_params(l, w_in, conv_w, conv_b, dt_bias, a_log, d_skip, ssd_norm_w, gm_ln_g,
                          gm_ln_b, w_sp, b_sp, w_router, b_router, ln_g, ln_b)
        bm = min(MM_BLOCK_M, t)
        zxa = _matmul(xb, w_in, l, 3 * d, bm)
        zxb = _matmul(xb, w["w_in_tail"], l, 4 * d, bm)
        y_ssd, y_gm = _mixer(zxa, zxb, xb, p, batch, seq)
        xf, xb = _merge(y_ssd, y_gm, zxb, xf, l, w, p)
        kv = _matmul(memb, w["w_kv"], l, 2 * d, min(MM_BLOCK_M, batch * mem_len))
        xf, xb, xg = _cross_attn(xb, xf, kv, l, w, p, batch, seq, mem_len)
        xf, xb = _moe(xb, xf, xg, l, w, p)
    return xf.reshape(batch, seq, d)
```

```python
import math

import jax
import jax.numpy as jnp
from jax import lax
from jax.experimental import pallas as pl
from jax.experimental.pallas import tpu as pltpu

F32 = jnp.float32
BF16 = jnp.bfloat16

D_MODEL = 1024
DEPTH = 2
CHUNK = 128
SSD_HEADS = 16
SSD_HEAD_DIM = 64
SSD_GROUPS = 4
SSD_HPG = SSD_HEADS // SSD_GROUPS
SSD_STATE = 128
SSD_GROUP_W = SSD_HPG * SSD_HEAD_DIM
CONV_K = 4
CONV_TAIL = CHUNK
GM_GROUPS = 8
GM_GROUP_DIM = D_MODEL // GM_GROUPS
XA_HEADS = 4
XA_HEAD_DIM = D_MODEL // XA_HEADS
N_EXPERTS = 32
TOP_K = 4
D_EXPERT = D_MODEL
SWIGLU_LIMIT = 7.0
SWIGLU_ALPHA = 1.702
DN_ALPHA = (2 * DEPTH) ** 0.25
EPS = 1e-5

LANES = 128
SUBLANES = 8
VMEM_LIMIT = 56 * 1024 * 1024

ROW_BLOCK = 512
MM_BLOCK_M = 2048
MM_BLOCK_N = 1024
CAST_ROWS = 128
ROUTE_BLOCK = 512
EXPERT_TILE = 512
NEG_BIG = -1e30
TILE_LINES = EXPERT_TILE * SUBLANES
PAIRS_PER_BLOCK = ROUTE_BLOCK * TOP_K
ROW_UNROLL = 8


def _cparams(*sem):
    return pltpu.CompilerParams(dimension_semantics=sem, vmem_limit_bytes=VMEM_LIMIT)


def _layer_norm(x, g, b):
    mu = jnp.mean(x, axis=-1, keepdims=True)
    xc = x - mu
    var = jnp.mean(xc * xc, axis=-1, keepdims=True)
    return xc * lax.rsqrt(var + EPS) * g + b


def _dot(a, b):
    return jnp.dot(a, b, preferred_element_type=F32)


def _dot_nt(a, b):
    return lax.dot_general(a, b, (((1,), (1,)), ((), ())), preferred_element_type=F32)


def _dot_tn(a, b):
    return lax.dot_general(a, b, (((0,), (0,)), ((), ())), preferred_element_type=F32)


def _split3(v):
    hi = v.astype(BF16)
    r1 = v - hi.astype(F32)
    mid = r1.astype(BF16)
    lo = (r1 - mid.astype(F32)).astype(BF16)
    return hi, mid, lo


def _dot_exact_rhs(sel, v):
    hi, mid, lo = _split3(v)
    return _dot(sel, hi) + _dot(sel, mid) + _dot(sel, lo)


def _dot_exact_lhs(v, sel):
    hi, mid, lo = _split3(v)
    return _dot(hi, sel) + _dot(mid, sel) + _dot(lo, sel)


def _dot_select_lhs(v, sel):
    hi = v.astype(BF16)
    lo = (v - hi.astype(F32)).astype(BF16)
    return _dot(hi, sel) + _dot(lo, sel)


def _sigmoid(x):
    return 1.0 / (1.0 + jnp.exp(-x))


def _gelu(x):
    return 0.5 * x * (1.0 + lax.erf(x * math.sqrt(0.5)))


def _softplus(x):
    return jnp.maximum(x, 0.0) + jnp.log1p(jnp.exp(-jnp.abs(x)))


def _cast_weight(dst_ref, src_ref):
    def body(c, carry):
        rows = pl.ds(pl.multiple_of(c * CAST_ROWS, CAST_ROWS), CAST_ROWS)
        dst_ref[rows, :] = src_ref[rows, :].astype(BF16)
        return carry

    lax.fori_loop(0, src_ref.shape[0] // CAST_ROWS, body, 0)


def _layer_weight(l, k, n):
    return pl.BlockSpec((1, k, n), lambda *_: (l, 0, 0), pipeline_mode=pl.Buffered(1))


def _ln_kernel(x_ref, g_ref, b_ref, of_ref, ob_ref):
    y = _layer_norm(x_ref[...], g_ref[...], b_ref[...])
    of_ref[...] = y
    ob_ref[...] = y.astype(BF16)


def _entry_ln(x, g, b):
    t, d = x.shape
    row = pl.BlockSpec((ROW_BLOCK, d), lambda i: (i, 0))
    vec = pl.BlockSpec((1, d), lambda i: (0, 0))
    return pl.pallas_call(
        _ln_kernel,
        grid=(t // ROW_BLOCK,),
        in_specs=[row, vec, vec],
        out_specs=[row, row],
        out_shape=[jax.ShapeDtypeStruct((t, d), F32), jax.ShapeDtypeStruct((t, d), BF16)],
        compiler_params=_cparams("arbitrary"),
        name="entry_ln",
    )(x, g.reshape(1, d), b.reshape(1, d))


def _mm_kernel(a_ref, w_ref, o_ref, wb_ref):
    @pl.when(pl.program_id(1) == 0)
    def _():
        _cast_weight(wb_ref, w_ref.at[0])

    o_ref[...] = _dot(a_ref[...].astype(BF16), wb_ref[...]).astype(o_ref.dtype)


def _matmul(a, w, l, n, bm):
    m, k = a.shape
    bn = MM_BLOCK_N
    return pl.pallas_call(
        _mm_kernel,
        grid=(n // bn, m // bm),
        in_specs=[pl.BlockSpec((bm, k), lambda j, i: (i, 0)),
                  pl.BlockSpec((1, k, bn), lambda j, i: (l, 0, j))],
        out_specs=pl.BlockSpec((bm, bn), lambda j, i: (i, j)),
        out_shape=jax.ShapeDtypeStruct((m, n), BF16),
        scratch_shapes=[pltpu.VMEM((k, bn), BF16)],
        compiler_params=_cparams("arbitrary", "arbitrary"),
        name="matmul",
    )(a, w)


def _conv_silu(raw_ref, tail_ref, cols, shift, w, b):
    raw = raw_ref[:, cols]
    aug = jnp.concatenate([tail_ref[:, cols], raw], axis=0)
    shifted = _dot(shift, aug)
    acc = raw.astype(F32) * w[CONV_K - 1:CONV_K] + b
    for j in range(1, CONV_K):
        acc = acc + shifted[(j - 1) * CHUNK:j * CHUNK] * w[CONV_K - 1 - j:CONV_K - j]
    tail_ref[:, cols] = raw[CHUNK - CONV_TAIL:CHUNK]
    return acc * _sigmoid(acc)


def _mixer_kernel(z_ref, xs_ref, bc_ref, u_ref, v_ref, xb_ref, wdt_ref, cw_ref, cb_ref, dtb_ref,
                  alog_ref, dskip_ref, nw_ref, lng_ref, lnb_ref, wsp_ref, bsp_ref, hexp_ref, shift_ref,
                  yssd_ref, ygm_ref, tailx_ref, tailbc_ref, state_ref):
    @pl.when(pl.program_id(1) == 0)
    def _():
        tailx_ref[...] = jnp.zeros_like(tailx_ref)
        tailbc_ref[...] = jnp.zeros_like(tailbc_ref)
        state_ref[...] = jnp.zeros_like(state_ref)

    row = lax.broadcasted_iota(jnp.int32, (CHUNK, CHUNK), 0)
    col = lax.broadcasted_iota(jnp.int32, (CHUNK, CHUNK), 1)
    causal = col <= row
    tri = jnp.where(causal, 1.0, 0.0).astype(BF16)

    cw = cw_ref[...]
    cb = cb_ref[...]
    shift = shift_ref[...]
    bc = _conv_silu(bc_ref, tailbc_ref, slice(0, D_MODEL), shift, cw[:, D_MODEL:], cb[:, D_MODEL:])
    gn = SSD_GROUPS * SSD_STATE

    dt = _softplus(_dot(xb_ref[...], wdt_ref[...]) + dtb_ref[...])
    a = -jnp.exp(alog_ref[...])
    cs = _dot_exact_rhs(tri, dt * a)
    cs_t = cs.T
    seg = lax.shift_right_logical(lax.broadcasted_iota(jnp.int32, (CHUNK, SSD_GROUP_W), 1),
                                  int(math.log2(SSD_HEAD_DIM)))

    for g in range(SSD_GROUPS):
        cols = slice(g * SSD_GROUP_W, (g + 1) * SSD_GROUP_W)
        xs = _conv_silu(xs_ref, tailx_ref, cols, shift, cw[:, cols], cb[:, cols])
        hexp = hexp_ref[:, cols]
        dt_x = _dot_select_lhs(dt, hexp)
        cs_x = _dot_select_lhs(cs, hexp)
        tot_x = cs_x[CHUNK - 1:CHUNK]
        xdt = xs * dt_x
        xdt_b = xdt.astype(BF16)
        xdec_b = (xdt * jnp.exp(tot_x - cs_x)).astype(BF16)

        b_g = bc[:, g * SSD_STATE:(g + 1) * SSD_STATE].astype(BF16)
        c_g = bc[:, gn + g * SSD_STATE:gn + (g + 1) * SSD_STATE].astype(BF16)
        cb_g = _dot_nt(c_g, b_g)
        st = state_ref[:, cols]
        y_g = _dot(c_g, st.astype(BF16)) * jnp.exp(cs_x)
        for hh in range(SSD_HPG):
            h = g * SSD_HPG + hh
            diff = cs[:, h:h + 1] - cs_t[h:h + 1, :]
            m_h = (cb_g * jnp.exp(jnp.where(causal, diff, -jnp.inf))).astype(BF16)
            y_g = y_g + jnp.where(seg == hh, _dot(m_h, xdt_b), 0.0)
        state_ref[:, cols] = st * jnp.exp(tot_x) + _dot_tn(b_g, xdec_b)

        z = z_ref[:, cols].astype(F32)
        y_g = (y_g + dskip_ref[:, cols] * xs) * (z * _sigmoid(z))
        y_g = y_g * lax.rsqrt(jnp.mean(y_g * y_g, axis=-1, keepdims=True) + EPS)
        yssd_ref[:, cols] = (y_g * nw_ref[:, cols]).astype(BF16)

    u = _gelu(u_ref[...].astype(F32))
    v = _layer_norm(_gelu(v_ref[...].astype(F32)), lng_ref[...], lnb_ref[...]).astype(BF16)
    sv_parts = []
    for g in range(GM_GROUPS):
        w_g = jnp.where(causal, wsp_ref[g], 0.0).astype(BF16)
        sv_parts.append(_dot(w_g, v[:, g * GM_GROUP_DIM:(g + 1) * GM_GROUP_DIM]))
    ygm_ref[...] = (u * (jnp.concatenate(sv_parts, axis=1) + bsp_ref[...])).astype(BF16)


def _mixer(zxa, zxb, xb, p, batch, seq):
    t, d = xb.shape
    nchunk = seq // CHUNK

    def blk(j):
        return pl.BlockSpec((CHUNK, d), lambda b, c, j=j: (b * nchunk + c, j))

    def const(shape):
        return pl.BlockSpec(shape, lambda b, c: (0,) * len(shape))

    out = pl.BlockSpec((CHUNK, d), lambda b, c: (b * nchunk + c, 0))
    return pl.pallas_call(
        _mixer_kernel,
        grid=(batch, nchunk),
        in_specs=[blk(0), blk(1), blk(2), blk(0), blk(1),
                  pl.BlockSpec((CHUNK, d), lambda b, c: (b * nchunk + c, 0)),
                  const((d, LANES)), const((CONV_K, 2 * d)), const((1, 2 * d)), const((1, LANES)),
                  const((1, LANES)), const((1, d)), const((1, d)), const((1, d)), const((1, d)),
                  const((GM_GROUPS, CHUNK, CHUNK)), const((CHUNK, d)), const((LANES, d)),
                  const(((CONV_K - 1) * CHUNK, CONV_TAIL + CHUNK))],
        out_specs=[out, out],
        out_shape=[jax.ShapeDtypeStruct((t, d), BF16), jax.ShapeDtypeStruct((t, d), BF16)],
        scratch_shapes=[pltpu.VMEM((CONV_TAIL, d), BF16), pltpu.VMEM((CONV_TAIL, d), BF16),
                        pltpu.VMEM((SSD_STATE, d), F32)],
        compiler_params=_cparams("arbitrary", "arbitrary"),
        name="mixer",
    )(zxa, zxa, zxa, zxb, zxb, xb, p["w_dt"], p["conv_w"], p["conv_b"], p["dt_bias"], p["a_log"],
      p["d_skip"], p["ssd_norm_w"], p["gm_ln_g"], p["gm_ln_b"], p["w_sp"], p["b_sp"], p["head_expand"],
      p["conv_shift"])


def _merge_kernel(ys_ref, yg_ref, gs_ref, gg_ref, x_ref, ps_ref, pg_ref, wo_ref, g_ref, b_ref,
                  of_ref, ob_ref, psb_ref, pgb_ref, wob_ref):
    @pl.when(pl.program_id(0) == 0)
    def _():
        _cast_weight(psb_ref, ps_ref.at[0])
        _cast_weight(pgb_ref, pg_ref.at[0])
        _cast_weight(wob_ref, wo_ref.at[0])

    h = (_sigmoid(gs_ref[...].astype(F32)) * _dot(ys_ref[...], psb_ref[...])
         + _sigmoid(gg_ref[...].astype(F32)) * _dot(yg_ref[...], pgb_ref[...]))
    mix = _dot(h.astype(BF16), wob_ref[...])
    y = _layer_norm(DN_ALPHA * x_ref[...] + mix, g_ref[...], b_ref[...])
    of_ref[...] = y
    ob_ref[...] = y.astype(BF16)


def _merge(y_ssd, y_gm, zxb, xf, l, w, p):
    t, d = xf.shape
    row = pl.BlockSpec((ROW_BLOCK, d), lambda i: (i, 0))
    mat = _layer_weight(l, d, d)
    vec = pl.BlockSpec((1, d), lambda i: (0, 0))
    return pl.pallas_call(
        _merge_kernel,
        grid=(t // ROW_BLOCK,),
        in_specs=[row, row, pl.BlockSpec((ROW_BLOCK, d), lambda i: (i, 2)),
                  pl.BlockSpec((ROW_BLOCK, d), lambda i: (i, 3)), row, mat, mat, mat, vec, vec],
        out_specs=[row, row],
        out_shape=[jax.ShapeDtypeStruct((t, d), F32), jax.ShapeDtypeStruct((t, d), BF16)],
        scratch_shapes=[pltpu.VMEM((d, d), BF16)] * 3,
        compiler_params=_cparams("arbitrary"),
        name="merge",
    )(y_ssd, y_gm, zxb, zxb, xf, w["p_ssd"], w["p_gm"], w["w_out"], p["ln_g0"], p["ln_b0"])


def _to_token_tiles(ref, y):
    m = y.shape[0]
    for j in range(SUBLANES):
        ref[pl.ds(j, m, stride=SUBLANES), :] = y[:, j * LANES:(j + 1) * LANES]


def _from_token_tiles(ref, m):
    return jnp.concatenate([ref[pl.ds(j, m, stride=SUBLANES), :] for j in range(SUBLANES)], axis=1)


def _attn_kernel(xb_ref, xf_ref, kv_ref, wq_ref, wo_ref, g_ref, b_ref, of_ref, ob_ref, og_ref,
                 wqb_ref, wob_ref):
    @pl.when(jnp.logical_and(pl.program_id(0) == 0, pl.program_id(1) == 0))
    def _():
        _cast_weight(wqb_ref, wq_ref.at[0])
        _cast_weight(wob_ref, wo_ref.at[0])

    q = _dot(xb_ref[...], wqb_ref[...]).astype(BF16)
    kv = kv_ref[...]
    outs = []
    for h in range(XA_HEADS):
        cols = slice(h * XA_HEAD_DIM, (h + 1) * XA_HEAD_DIM)
        s = _dot_nt(q[:, cols], kv[:, cols]) * (XA_HEAD_DIM ** -0.5)
        e = jnp.exp(s - jnp.max(s, axis=-1, keepdims=True))
        p = (e / jnp.sum(e, axis=-1, keepdims=True)).astype(BF16)
        outs.append(_dot(p, kv[:, D_MODEL + h * XA_HEAD_DIM:D_MODEL + (h + 1) * XA_HEAD_DIM]))
    o = jnp.concatenate(outs, axis=1).astype(BF16)
    y = _layer_norm(DN_ALPHA * xf_ref[...] + _dot(o, wob_ref[...]), g_ref[...], b_ref[...])
    of_ref[...] = y
    ob_ref[...] = y.astype(BF16)
    _to_token_tiles(og_ref, y)


def _cross_attn(xb, xf, kv, l, w, p, batch, seq, mem_len):
    t, d = xf.shape
    nblk = seq // ROW_BLOCK
    row = pl.BlockSpec((ROW_BLOCK, d), lambda b, i: (b * nblk + i, 0))
    mat = _layer_weight(l, d, d)
    vec = pl.BlockSpec((1, d), lambda b, i: (0, 0))
    return pl.pallas_call(
        _attn_kernel,
        grid=(batch, nblk),
        in_specs=[row, row, pl.BlockSpec((mem_len, 2 * d), lambda b, i: (b, 0)), mat, mat, vec, vec],
        out_specs=[row, row, pl.BlockSpec((ROW_BLOCK * SUBLANES, LANES), lambda b, i: (b * nblk + i, 0))],
        out_shape=[jax.ShapeDtypeStruct((t, d), F32), jax.ShapeDtypeStruct((t, d), BF16),
                   jax.ShapeDtypeStruct((t * SUBLANES, LANES), F32)],
        scratch_shapes=[pltpu.VMEM((d, d), BF16)] * 2,
        compiler_params=_cparams("arbitrary", "arbitrary"),
        name="cross_attn",
    )(xb, xf, kv, w["wq"], w["wo"], p["ln_g1"], p["ln_b1"])


def _rows_from(rows, shape):
    sub = lax.broadcasted_iota(jnp.int32, shape, 0)
    out = jnp.zeros(shape, rows[0].dtype)
    for k, r in enumerate(rows):
        out = jnp.where(sub == k, r, out)
    return out


def _col_to_row(col):
    return jnp.broadcast_to(col, (LANES, LANES)).T[0:1]


def _row_to_col(row):
    return jnp.broadcast_to(row, (LANES, LANES)).T[:, 0:1]


def _router_kernel(xb_ref, wr_ref, br_ref, upper_ref, gate_ref, lpos_ref, bcnt_ref, bbase_ref, bstart_ref,
                   carry_ref):
    @pl.when(pl.program_id(0) == 0)
    def _():
        carry_ref[...] = jnp.zeros_like(carry_ref)

    m = xb_ref.shape[0]
    logits = _dot(xb_ref[...], wr_ref[...]) + br_ref[...]
    lt = jnp.concatenate([logits[r:r + LANES].T for r in range(0, m, LANES)], axis=1)[:N_EXPERTS]
    sub = lax.broadcasted_iota(jnp.int32, lt.shape, 0)
    tops, hots = [], []
    for _ in range(TOP_K):
        top = jnp.max(lt, axis=0, keepdims=True)
        idx = jnp.min(jnp.where(lt == top, sub, N_EXPERTS), axis=0, keepdims=True)
        hot = sub == idx
        lt = jnp.where(hot, -jnp.inf, lt)
        tops.append(top)
        hots.append(hot)
    es = [jnp.exp(v - tops[0]) for v in tops]
    den = es[0] + es[1] + es[2] + es[3]

    hot_all = jnp.zeros(lt.shape, F32)
    for hot in hots:
        hot_all = hot_all + jnp.where(hot, 1.0, 0.0)
    cnt_col = jnp.sum(hot_all, axis=1, keepdims=True)
    cnt = _col_to_row(jnp.concatenate([cnt_col, jnp.zeros((LANES - N_EXPERTS, 1), F32)], axis=0))
    er = lax.broadcasted_iota(jnp.int32, (LANES, LANES), 0)
    ec = lax.broadcasted_iota(jnp.int32, (LANES, LANES), 1)
    lower_experts = jnp.where(er < ec, 1.0, 0.0).astype(BF16)
    lstart = _dot_exact_lhs(jnp.broadcast_to(cnt, (SUBLANES, LANES)), lower_experts)[0:1]
    local = _row_to_col(lstart)[:N_EXPERTS] + _dot(hot_all.astype(BF16), upper_ref[...])
    lpos = [jnp.sum(jnp.where(hot, local, 0.0), axis=0, keepdims=True) for hot in hots]

    half = jnp.bitwise_and(pl.program_id(0), 1).astype(F32) * PAIRS_PER_BLOCK
    lines = [(v + half) * SUBLANES for v in lpos]
    gate_ref[...] = _rows_from([e / den for e in es], gate_ref.shape)
    lpos_ref[...] = _rows_from(lines, lpos_ref.shape).astype(jnp.int32)
    bcnt_ref[0] = cnt.astype(jnp.int32)
    bbase_ref[0] = carry_ref[...].astype(jnp.int32)
    bstart_ref[0] = lstart.astype(jnp.int32)
    carry_ref[...] = carry_ref[...] + cnt


def _router(xb, p):
    t, d = xb.shape
    nblk = t // ROUTE_BLOCK
    per_token = pl.BlockSpec((SUBLANES, ROUTE_BLOCK), lambda i: (i, 0))
    one = pl.BlockSpec((1, LANES), lambda i: (0, 0))
    per_block = pl.BlockSpec((1, 1, LANES), lambda i: (i, 0, 0))
    per_block_shape = jax.ShapeDtypeStruct((nblk, 1, LANES), jnp.int32)
    tok = jnp.arange(ROUTE_BLOCK, dtype=jnp.int32)
    earlier = (tok[:, None] < tok[None, :]).astype(BF16)
    return pl.pallas_call(
        _router_kernel,
        grid=(nblk,),
        in_specs=[pl.BlockSpec((ROUTE_BLOCK, d), lambda i: (i, 0)),
                  pl.BlockSpec((d, LANES), lambda i: (0, 0)), one,
                  pl.BlockSpec((ROUTE_BLOCK, ROUTE_BLOCK), lambda i: (0, 0))],
        out_specs=[per_token, per_token, per_block, per_block, per_block],
        out_shape=[jax.ShapeDtypeStruct((nblk * SUBLANES, ROUTE_BLOCK), F32),
                   jax.ShapeDtypeStruct((nblk * SUBLANES, ROUTE_BLOCK), jnp.int32),
                   per_block_shape, per_block_shape, per_block_shape],
        scratch_shapes=[pltpu.VMEM((1, LANES), F32)],
        compiler_params=_cparams("arbitrary"),
        name="router",
    )(xb, p["w_router"], p["b_router"], earlier)


def _rows(ref, row, n):
    return ref.at[pl.ds(pl.multiple_of(row * SUBLANES, SUBLANES), n * SUBLANES)]


def _range_copies(n, near_ref, near_row, far_hbm, far_row, sem, to_far, wait=False, same_near=False):
    p = ROUTE_BLOCK
    while p >= 1:
        done = jnp.bitwise_and(n, -2 * p)

        @pl.when(jnp.bitwise_and(n, p) != 0)
        def _(p=p, done=done):
            near = _rows(near_ref, near_row if same_near else near_row + done, p)
            far = _rows(far_hbm, far_row + done, p)
            copy = pltpu.make_async_copy(near, far, sem) if to_far else pltpu.make_async_copy(far, near, sem)
            copy.wait() if wait else copy.start()

        p //= 2


def _stage_half(stage_ref, s):
    return _rows(stage_ref, s * PAIRS_PER_BLOCK, PAIRS_PER_BLOCK)


def _tile_at(ref, line):
    return ref.at[pl.ds(pl.multiple_of(line, SUBLANES), SUBLANES)]


def _dispatch_kernel(cnt_ref, lstart_ref, gstart_ref, fill_ref, lpos_ref, xg_ref, xs_hbm,
                     stage_ref, zero_ref, sem, zsem):
    b = pl.program_id(0)
    last = pl.num_programs(0) - 1
    slot = jnp.bitwise_and(b, 1)

    def wait_half(s):
        pltpu.make_async_copy(_stage_half(stage_ref, s), _rows(xs_hbm, 0, PAIRS_PER_BLOCK), sem.at[s]).wait()

    @pl.when(b == 0)
    def _():
        zero_ref[...] = jnp.zeros_like(zero_ref)

        def zfill(e, wait):
            _range_copies(fill_ref[N_EXPERTS + 1 + e], zero_ref, 0, xs_hbm, fill_ref[e], zsem, True,
                          wait=wait, same_near=True)

        def zfill_start(e, carry):
            zfill(e, False)
            return carry

        def zfill_wait(e, carry):
            zfill(e, True)
            return carry

        lax.fori_loop(0, N_EXPERTS, zfill_start, 0)
        lax.fori_loop(0, N_EXPERTS, zfill_wait, 0)

        def tail_copy(i):
            return pltpu.make_async_copy(zero_ref, _rows(xs_hbm, i * EXPERT_TILE, EXPERT_TILE), zsem)

        def tail_start(i, carry):
            tail_copy(i).start()
            return carry

        def tail_wait(i, carry):
            tail_copy(i).wait()
            return carry

        n_all = xs_hbm.shape[0] // TILE_LINES
        lax.fori_loop(fill_ref[N_EXPERTS], n_all, tail_start, 0)
        lax.fori_loop(fill_ref[N_EXPERTS], n_all, tail_wait, 0)

    lines_of = [lpos_ref.at[pl.ds(k * ROUTE_BLOCK, ROUTE_BLOCK)] for k in range(TOP_K)]

    def fill(c, carry):
        for u in range(ROW_UNROLL):
            t = c * ROW_UNROLL + u
            tile = _tile_at(xg_ref, t * SUBLANES)[...]
            for k in range(TOP_K):
                _tile_at(stage_ref, lines_of[k][t])[...] = tile
        return carry

    lax.fori_loop(0, ROUTE_BLOCK // ROW_UNROLL, fill, 0)

    def ranges(e, carry):
        j = b * N_EXPERTS + e
        _range_copies(cnt_ref[j], stage_ref, slot * PAIRS_PER_BLOCK + lstart_ref[j], xs_hbm, gstart_ref[j],
                      sem.at[slot], True)
        return carry

    lax.fori_loop(0, N_EXPERTS, ranges, 0)

    @pl.when(b > 0)
    def _():
        wait_half(1 - slot)

    @pl.when(b == last)
    def _():
        wait_half(slot)


def _dispatch(xg, lpos, cnt, lstart, gstart, fill_from, n_slots):
    block_lines = ROUTE_BLOCK * SUBLANES
    nblk = xg.shape[0] // block_lines
    grid_spec = pltpu.PrefetchScalarGridSpec(
        num_scalar_prefetch=4,
        grid=(nblk,),
        in_specs=[pl.BlockSpec((PAIRS_PER_BLOCK,), lambda b, *_: (b,), memory_space=pltpu.SMEM),
                  pl.BlockSpec((block_lines, LANES), lambda b, *_: (b, 0))],
        out_specs=pl.BlockSpec(memory_space=pl.ANY),
        scratch_shapes=[pltpu.VMEM((2 * PAIRS_PER_BLOCK * SUBLANES, LANES), F32),
                        pltpu.VMEM((TILE_LINES, LANES), F32),
                        pltpu.SemaphoreType.DMA((2,)), pltpu.SemaphoreType.DMA(())],
    )
    return pl.pallas_call(
        _dispatch_kernel,
        grid_spec=grid_spec,
        out_shape=jax.ShapeDtypeStruct((n_slots * SUBLANES, LANES), F32),
        compiler_params=_cparams("arbitrary"),
        name="moe_dispatch",
    )(cnt, lstart, gstart, fill_from, lpos, xg)


def _expert_kernel(te_ref, nu_ref, x_ref, wgu_ref, bgu_ref, wd_ref, bd_ref, o_ref, wgub_ref, wdb_ref):
    i = pl.program_id(0)

    @pl.when(jnp.logical_or(i == 0, te_ref[i] != te_ref[jnp.maximum(i - 1, 0)]))
    def _():
        _cast_weight(wgub_ref, wgu_ref.at[0, 0])
        _cast_weight(wdb_ref, wd_ref.at[0, 0])

    @pl.when(i < nu_ref[0])
    def _():
        x = _from_token_tiles(x_ref, EXPERT_TILE).astype(BF16)
        hgu = _dot(x, wgub_ref[...]) + bgu_ref[0, 0]
        gate = jnp.minimum(hgu[:, :D_EXPERT], SWIGLU_LIMIT)
        up = jnp.clip(hgu[:, D_EXPERT:], -SWIGLU_LIMIT, SWIGLU_LIMIT)
        glu = gate * _sigmoid(SWIGLU_ALPHA * gate)
        _to_token_tiles(o_ref, _dot(((up + 1.0) * glu).astype(BF16), wdb_ref[...]) + bd_ref[0, 0])

    @pl.when(i >= nu_ref[0])
    def _():
        o_ref[...] = jnp.zeros_like(o_ref)


def _experts(xs, tile_expert, n_used, n_tiles, l, w):
    d = D_MODEL

    def of_expert(shape):
        return pl.BlockSpec((1, 1) + shape, lambda i, te, nu: (l, te[i], 0, 0))

    grid_spec = pltpu.PrefetchScalarGridSpec(
        num_scalar_prefetch=2,
        grid=(n_tiles,),
        in_specs=[
            pl.BlockSpec((TILE_LINES, LANES), lambda i, te, nu: (jnp.minimum(i, nu[0] - 1), 0)),
            of_expert((d, 2 * D_EXPERT)), of_expert((1, 2 * D_EXPERT)),
            of_expert((D_EXPERT, d)), of_expert((1, d)),
        ],
        out_specs=pl.BlockSpec((TILE_LINES, LANES), lambda i, te, nu: (i, 0)),
        scratch_shapes=[pltpu.VMEM((d, 2 * D_EXPERT), BF16), pltpu.VMEM((D_EXPERT, d), BF16)],
    )
    return pl.pallas_call(
        _expert_kernel,
        grid_spec=grid_spec,
        out_shape=jax.ShapeDtypeStruct((n_tiles * TILE_LINES, LANES), F32),
        compiler_params=_cparams("arbitrary"),
        name="moe_experts",
    )(tile_expert, n_used, xs, w["w_gu"], w["b_gu"], w["w_down"], w["b_down"])


def _combine_kernel(cnt_ref, lstart_ref, gstart_ref, lpos_ref, gate_ref, ys_hbm, x_ref, g_ref, b_ref,
                    of_ref, ob_ref, stage_ref, acc_ref, sem):
    b = pl.program_id(0)
    last = pl.num_programs(0) - 1
    slot = jnp.bitwise_and(b, 1)

    def fetch(blk, s):
        def ranges(e, carry):
            j = blk * N_EXPERTS + e
            _range_copies(cnt_ref[j], stage_ref, s * PAIRS_PER_BLOCK + lstart_ref[j], ys_hbm, gstart_ref[j],
                          sem.at[s], False)
            return carry

        lax.fori_loop(0, N_EXPERTS, ranges, 0)

    @pl.when(b == 0)
    def _():
        fetch(b, slot)

    @pl.when(b < last)
    def _():
        fetch(b + 1, 1 - slot)

    pltpu.make_async_copy(_rows(ys_hbm, 0, PAIRS_PER_BLOCK), _stage_half(stage_ref, slot), sem.at[slot]).wait()

    lines_of = [lpos_ref.at[pl.ds(k * ROUTE_BLOCK, ROUTE_BLOCK)] for k in range(TOP_K)]
    gates_of = [gate_ref.at[pl.ds(k * ROUTE_BLOCK, ROUTE_BLOCK)] for k in range(TOP_K)]

    def gather(c, carry):
        for u in range(ROW_UNROLL):
            t = c * ROW_UNROLL + u
            tile = jnp.zeros((SUBLANES, LANES), F32)
            for k in range(TOP_K):
                tile = tile + gates_of[k][t] * _tile_at(stage_ref, lines_of[k][t])[...]
            _tile_at(acc_ref, t * SUBLANES)[...] = tile
        return carry

    lax.fori_loop(0, ROUTE_BLOCK // ROW_UNROLL, gather, 0)

    y = _layer_norm(DN_ALPHA * x_ref[...] + _from_token_tiles(acc_ref, ROUTE_BLOCK), g_ref[...], b_ref[...])
    of_ref[...] = y
    ob_ref[...] = y.astype(BF16)


def _combine(ys, lpos, gates, cnt, lstart, gstart, xf, p):
    t, d = xf.shape
    nblk = t // ROUTE_BLOCK
    row = pl.BlockSpec((ROUTE_BLOCK, d), lambda b, *_: (b, 0))
    vec = pl.BlockSpec((1, d), lambda b, *_: (0, 0))
    pairs = pl.BlockSpec((PAIRS_PER_BLOCK,), lambda b, *_: (b,), memory_space=pltpu.SMEM)
    grid_spec = pltpu.PrefetchScalarGridSpec(
        num_scalar_prefetch=3,
        grid=(nblk,),
        in_specs=[pairs, pairs, pl.BlockSpec(memory_space=pl.ANY), row, vec, vec],
        out_specs=[row, row],
        scratch_shapes=[pltpu.VMEM((2 * PAIRS_PER_BLOCK * SUBLANES, LANES), F32),
                        pltpu.VMEM((ROUTE_BLOCK * SUBLANES, LANES), F32), pltpu.SemaphoreType.DMA((2,))],
    )
    return pl.pallas_call(
        _combine_kernel,
        grid_spec=grid_spec,
        out_shape=[jax.ShapeDtypeStruct((t, d), F32), jax.ShapeDtypeStruct((t, d), BF16)],
        compiler_params=_cparams("arbitrary"),
        name="moe_combine",
    )(cnt, lstart, gstart, lpos, gates, ys, xf, p["ln_g2"], p["ln_b2"])


def _moe(xb, xf, xg, l, w, p):
    t, d = xf.shape
    gates, lpos, bcnt, bbase, bstart = _router(xb, p)
    bcnt = bcnt[:, 0, :N_EXPERTS]
    counts = jnp.sum(bcnt, axis=0)
    padded = (counts + EXPERT_TILE - 1) // EXPERT_TILE * EXPERT_TILE
    ends = jnp.cumsum(padded)
    starts = ends - padded
    gstart = (starts[None, :] + bbase[:, 0, :N_EXPERTS]).reshape(-1)
    lstart = bstart[:, 0, :N_EXPERTS].reshape(-1)
    n_tiles = (t * TOP_K) // EXPERT_TILE + N_EXPERTS
    n_used = ends[-1] // EXPERT_TILE
    tile_start = jnp.minimum(jnp.arange(n_tiles, dtype=jnp.int32), n_used - 1) * EXPERT_TILE
    tile_expert = jnp.sum((ends[None, :] <= tile_start[:, None]).astype(jnp.int32), axis=1)

    fill_from = jnp.concatenate([starts + counts, n_used.reshape(1), padded - counts])
    n_slots = n_tiles * EXPERT_TILE
    def k_major(v):
        return v.reshape(-1, SUBLANES, ROUTE_BLOCK)[:, :TOP_K].reshape(-1)

    lpos, gates = k_major(lpos), k_major(gates)
    xs = _dispatch(xg, lpos, bcnt.reshape(-1), lstart, gstart, fill_from, n_slots)
    ys = _experts(xs, tile_expert, n_used.reshape(1), n_tiles, l, w)
    return _combine(ys, lpos, gates, bcnt.reshape(-1), lstart, gstart, xf, p)


def _layer_params(l, w_in, conv_w, conv_b, dt_bias, a_log, d_skip, ssd_norm_w, gm_ln_g, gm_ln_b,
                  w_sp, b_sp, w_router, b_router, ln_g, ln_b):
    d = D_MODEL
    off_dt = d + conv_w.shape[-1]
    off_u = off_dt + SSD_HEADS
    pad_h = LANES - SSD_HEADS
    pad_e = LANES - N_EXPERTS
    head_of_channel = jnp.arange(d, dtype=jnp.int32) // SSD_HEAD_DIM
    shift_row = jnp.arange((CONV_K - 1) * CHUNK, dtype=jnp.int32)
    shift_col = CONV_TAIL + shift_row % CHUNK - (1 + shift_row // CHUNK)
    return {
        "conv_shift": (jnp.arange(CONV_TAIL + CHUNK, dtype=jnp.int32)[None, :] == shift_col[:, None]).astype(BF16),
        "w_dt": jnp.pad(w_in[l, :, off_dt:off_u], ((0, 0), (0, pad_h))).astype(BF16),
        "conv_w": conv_w[l], "conv_b": conv_b[l].reshape(1, -1),
        "dt_bias": jnp.pad(dt_bias[l], (0, pad_h)).reshape(1, LANES),
        "a_log": jnp.pad(a_log[l], (0, pad_h)).reshape(1, LANES),
        "d_skip": d_skip[l][head_of_channel].reshape(1, d),
        "ssd_norm_w": ssd_norm_w[l].reshape(1, d),
        "gm_ln_g": gm_ln_g[l].reshape(1, d), "gm_ln_b": gm_ln_b[l].reshape(1, d),
        "w_sp": w_sp[l],
        "b_sp": jnp.repeat(b_sp[l].T, GM_GROUP_DIM, axis=1),
        "head_expand": (jnp.arange(LANES, dtype=jnp.int32)[:, None] == head_of_channel[None, :]).astype(BF16),
        "w_router": jnp.pad(w_router[l], ((0, 0), (0, pad_e))).astype(BF16),
        "b_router": jnp.pad(b_router[l], (0, pad_e), constant_values=NEG_BIG).reshape(1, LANES),
        "ln_g0": ln_g[l, 0].reshape(1, d), "ln_b0": ln_b[l, 0].reshape(1, d),
        "ln_g1": ln_g[l, 1].reshape(1, d), "ln_b1": ln_b[l, 1].reshape(1, d),
        "ln_g2": ln_g[l, 2].reshape(1, d), "ln_b2": ln_b[l, 2].reshape(1, d),
    }


def kernel(x, mem, ln0_g, ln0_b, w_in, conv_w, conv_b, dt_bias, a_log, d_skip, ssd_norm_w, gm_ln_g, gm_ln_b, w_sp, b_sp, p_ssd, p_gm, w_out, wq, wk, wv, wo, w_router, b_router, w_gu, b_gu, w_down, b_down, ln_g, ln_b):
    batch, seq, d = x.shape
    mem_len = mem.shape[1]
    depth = w_in.shape[0]
    assert d == D_MODEL and seq % ROW_BLOCK == 0 and seq % CHUNK == 0
    t = batch * seq
    memb = mem.reshape(batch * mem_len, d)
    off_u = d + conv_w.shape[-1] + SSD_HEADS
    w = {"w_in_tail": w_in[:, :, off_u:],
         "w_kv": jnp.concatenate([wk, wv], axis=2),
         "p_ssd": p_ssd, "p_gm": p_gm, "w_out": w_out, "wq": wq, "wo": wo,
         "w_gu": w_gu, "b_gu": b_gu.reshape(depth, N_EXPERTS, 1, -1),
         "w_down": w_down, "b_down": b_down.reshape(depth, N_EXPERTS, 1, -1)}
    xf, xb = _entry_ln(x.reshape(t, d), ln0_g, ln0_b)
    for l in range(depth):
        p = _layer_params(l, w_in, conv_w, conv_b, dt_bias, a_log, d_skip, ssd_norm_w, gm_ln_g,
                          gm_ln_b, w_sp, b_sp, w_router, b_router, ln_g, ln_b)
        bm = min(MM_BLOCK_M, t)
        zxa = _matmul(xb, w_in, l, 3 * d, bm)
        zxb = _matmul(xb, w["w_in_tail"], l, 4 * d, bm)
        y_ssd, y_gm = _mixer(zxa, zxb, xb, p, batch, seq)
        xf, xb = _merge(y_ssd, y_gm, zxb, xf, l, w, p)
        kv = _matmul(memb, w["w_kv"], l, 2 * d, min(MM_BLOCK_M, batch * mem_len))
        xf, xb, xg = _cross_attn(xb, xf, kv, l, w, p, batch, seq, mem_len)
        xf, xb = _moe(xb, xf, xg, l, w, p)
    return xf.reshape(batch, seq, d)
```

```python
import functools
import math

import jax
import jax.numpy as jnp
from jax import lax
from jax.experimental import pallas as pl
from jax.experimental.pallas import tpu as pltpu

F32 = jnp.float32
BF16 = jnp.bfloat16

D_MODEL = 1024
DEPTH = 2
CHUNK = 128
SSD_HEADS = 16
SSD_HEAD_DIM = 64
SSD_GROUPS = 4
SSD_HPG = SSD_HEADS // SSD_GROUPS
SSD_STATE = 128
SSD_GROUP_W = SSD_HPG * SSD_HEAD_DIM
CONV_K = 4
CONV_TAIL = CHUNK
GM_GROUPS = 8
GM_GROUP_DIM = D_MODEL // GM_GROUPS
XA_HEADS = 4
XA_HEAD_DIM = D_MODEL // XA_HEADS
N_EXPERTS = 32
TOP_K = 4
D_EXPERT = D_MODEL
SWIGLU_LIMIT = 7.0
SWIGLU_ALPHA = 1.702
DN_ALPHA = (2 * DEPTH) ** 0.25
EPS = 1e-5

LANES = 128
SUBLANES = 8
VMEM_LIMIT = 56 * 1024 * 1024

ROW_BLOCK = 512
MM_BLOCK_M = 2048
MM_BLOCK_N = 1024
CAST_ROWS = 128
ROUTE_BLOCK = 512
EXPERT_TILE = 512
NEG_BIG = -1e30
TILE_LINES = EXPERT_TILE * SUBLANES
PAIRS_PER_BLOCK = ROUTE_BLOCK * TOP_K
ROW_UNROLL = 8


def _cparams(*sem):
    return pltpu.CompilerParams(dimension_semantics=sem, vmem_limit_bytes=VMEM_LIMIT)


def _layer_norm(x, g, b):
    mu = jnp.mean(x, axis=-1, keepdims=True)
    xc = x - mu
    var = jnp.mean(xc * xc, axis=-1, keepdims=True)
    return xc * lax.rsqrt(var + EPS) * g + b


def _dot(a, b):
    return jnp.dot(a, b, preferred_element_type=F32)


def _dot_nt(a, b):
    return lax.dot_general(a, b, (((1,), (1,)), ((), ())), preferred_element_type=F32)


def _dot_tn(a, b):
    return lax.dot_general(a, b, (((0,), (0,)), ((), ())), preferred_element_type=F32)


def _split3(v):
    hi = v.astype(BF16)
    r1 = v - hi.astype(F32)
    mid = r1.astype(BF16)
    lo = (r1 - mid.astype(F32)).astype(BF16)
    return hi, mid, lo


def _dot_exact_rhs(sel, v):
    hi, mid, lo = _split3(v)
    return _dot(sel, hi) + _dot(sel, mid) + _dot(sel, lo)


def _dot_exact_lhs(v, sel):
    hi, mid, lo = _split3(v)
    return _dot(hi, sel) + _dot(mid, sel) + _dot(lo, sel)


def _dot_select_lhs(v, sel):
    hi = v.astype(BF16)
    lo = (v - hi.astype(F32)).astype(BF16)
    return _dot(hi, sel) + _dot(lo, sel)


def _sigmoid(x):
    return 1.0 / (1.0 + jnp.exp(-x))


def _gelu(x):
    return 0.5 * x * (1.0 + lax.erf(x * math.sqrt(0.5)))


def _softplus(x):
    return jnp.maximum(x, 0.0) + jnp.log1p(jnp.exp(-jnp.abs(x)))


def _cast_weight(dst_ref, src_ref):
    def body(c, carry):
        rows = pl.ds(pl.multiple_of(c * CAST_ROWS, CAST_ROWS), CAST_ROWS)
        dst_ref[rows, :] = src_ref[rows, :].astype(BF16)
        return carry

    lax.fori_loop(0, src_ref.shape[0] // CAST_ROWS, body, 0)


def _layer_weight(l, k, n):
    return pl.BlockSpec((1, k, n), lambda *_: (l, 0, 0), pipeline_mode=pl.Buffered(1))


def _ln_kernel(x_ref, g_ref, b_ref, of_ref, ob_ref):
    y = _layer_norm(x_ref[...], g_ref[...], b_ref[...])
    of_ref[...] = y
    ob_ref[...] = y.astype(BF16)


def _entry_ln(x, g, b):
    t, d = x.shape
    row = pl.BlockSpec((ROW_BLOCK, d), lambda i: (i, 0))
    vec = pl.BlockSpec((1, d), lambda i: (0, 0))
    return pl.pallas_call(
        _ln_kernel,
        grid=(t // ROW_BLOCK,),
        in_specs=[row, vec, vec],
        out_specs=[row, row],
        out_shape=[jax.ShapeDtypeStruct((t, d), F32), jax.ShapeDtypeStruct((t, d), BF16)],
        compiler_params=_cparams("arbitrary"),
        name="entry_ln",
    )(x, g.reshape(1, d), b.reshape(1, d))


def _mm_kernel(a_ref, w_ref, o_ref, wb_ref):
    @pl.when(pl.program_id(1) == 0)
    def _():
        _cast_weight(wb_ref, w_ref.at[0])

    o_ref[...] = _dot(a_ref[...].astype(BF16), wb_ref[...]).astype(o_ref.dtype)


def _matmul(a, w, l, n, bm):
    m, k = a.shape
    bn = MM_BLOCK_N
    return pl.pallas_call(
        _mm_kernel,
        grid=(n // bn, m // bm),
        in_specs=[pl.BlockSpec((bm, k), lambda j, i: (i, 0)),
                  pl.BlockSpec((1, k, bn), lambda j, i: (l, 0, j))],
        out_specs=pl.BlockSpec((bm, bn), lambda j, i: (i, j)),
        out_shape=jax.ShapeDtypeStruct((m, n), BF16),
        scratch_shapes=[pltpu.VMEM((k, bn), BF16)],
        compiler_params=_cparams("arbitrary", "arbitrary"),
        name="matmul",
    )(a, w)


def _conv_silu(raw_ref, tail_ref, cols, shift, w, b):
    raw = raw_ref[:, cols]
    aug = jnp.concatenate([tail_ref[:, cols], raw], axis=0)
    shifted = _dot(shift, aug)
    acc = raw.astype(F32) * w[CONV_K - 1:CONV_K] + b
    for j in range(1, CONV_K):
        acc = acc + shifted[(j - 1) * CHUNK:j * CHUNK] * w[CONV_K - 1 - j:CONV_K - j]
    tail_ref[:, cols] = raw[CHUNK - CONV_TAIL:CHUNK]
    return acc * _sigmoid(acc)


def _mixer_kernel(z_ref, xs_ref, bc_ref, u_ref, v_ref, xb_ref, wdt_ref, cw_ref, cb_ref, dtb_ref,
                  alog_ref, dskip_ref, nw_ref, lng_ref, lnb_ref, wsp_ref, bsp_ref, hexp_ref, shift_ref,
                  yssd_ref, ygm_ref, tailx_ref, tailbc_ref, state_ref):
    @pl.when(pl.program_id(1) == 0)
    def _():
        tailx_ref[...] = jnp.zeros_like(tailx_ref)
        tailbc_ref[...] = jnp.zeros_like(tailbc_ref)
        state_ref[...] = jnp.zeros_like(state_ref)

    row = lax.broadcasted_iota(jnp.int32, (CHUNK, CHUNK), 0)
    col = lax.broadcasted_iota(jnp.int32, (CHUNK, CHUNK), 1)
    causal = col <= row
    tri = jnp.where(causal, 1.0, 0.0).astype(BF16)

    cw = cw_ref[...]
    cb = cb_ref[...]
    shift = shift_ref[...]
    bc = _conv_silu(bc_ref, tailbc_ref, slice(0, D_MODEL), shift, cw[:, D_MODEL:], cb[:, D_MODEL:])
    gn = SSD_GROUPS * SSD_STATE

    dt = _softplus(_dot(xb_ref[...], wdt_ref[...]) + dtb_ref[...])
    a = -jnp.exp(alog_ref[...])
    cs = _dot_exact_rhs(tri, dt * a)
    cs_t = cs.T
    seg = lax.shift_right_logical(lax.broadcasted_iota(jnp.int32, (CHUNK, SSD_GROUP_W), 1),
                                  int(math.log2(SSD_HEAD_DIM)))

    for g in range(SSD_GROUPS):
        cols = slice(g * SSD_GROUP_W, (g + 1) * SSD_GROUP_W)
        xs = _conv_silu(xs_ref, tailx_ref, cols, shift, cw[:, cols], cb[:, cols])
        hexp = hexp_ref[:, cols]
        dt_x = _dot_select_lhs(dt, hexp)
        cs_x = _dot_select_lhs(cs, hexp)
        tot_x = cs_x[CHUNK - 1:CHUNK]
        xdt = xs * dt_x
        xdt_b = xdt.astype(BF16)
        xdec_b = (xdt * jnp.exp(tot_x - cs_x)).astype(BF16)

        b_g = bc[:, g * SSD_STATE:(g + 1) * SSD_STATE].astype(BF16)
        c_g = bc[:, gn + g * SSD_STATE:gn + (g + 1) * SSD_STATE].astype(BF16)
        cb_g = _dot_nt(c_g, b_g)
        st = state_ref[:, cols]
        y_g = _dot(c_g, st.astype(BF16)) * jnp.exp(cs_x)
        for hh in range(SSD_HPG):
            h = g * SSD_HPG + hh
            diff = cs[:, h:h + 1] - cs_t[h:h + 1, :]
            m_h = (cb_g * jnp.exp(jnp.where(causal, diff, -jnp.inf))).astype(BF16)
            y_g = y_g + jnp.where(seg == hh, _dot(m_h, xdt_b), 0.0)
        state_ref[:, cols] = st * jnp.exp(tot_x) + _dot_tn(b_g, xdec_b)

        z = z_ref[:, cols].astype(F32)
        y_g = (y_g + dskip_ref[:, cols] * xs) * (z * _sigmoid(z))
        y_g = y_g * lax.rsqrt(jnp.mean(y_g * y_g, axis=-1, keepdims=True) + EPS)
        yssd_ref[:, cols] = (y_g * nw_ref[:, cols]).astype(BF16)

    u = _gelu(u_ref[...].astype(F32))
    v = _layer_norm(_gelu(v_ref[...].astype(F32)), lng_ref[...], lnb_ref[...]).astype(BF16)
    sv_parts = []
    for g in range(GM_GROUPS):
        w_g = jnp.where(causal, wsp_ref[g], 0.0).astype(BF16)
        sv_parts.append(_dot(w_g, v[:, g * GM_GROUP_DIM:(g + 1) * GM_GROUP_DIM]))
    ygm_ref[...] = (u * (jnp.concatenate(sv_parts, axis=1) + bsp_ref[...])).astype(BF16)


def _mixer(zxa, zxb, xb, p, batch, seq):
    t, d = xb.shape
    nchunk = seq // CHUNK

    def blk(j):
        return pl.BlockSpec((CHUNK, d), lambda b, c, j=j: (b * nchunk + c, j))

    def const(shape):
        return pl.BlockSpec(shape, lambda b, c: (0,) * len(shape))

    out = pl.BlockSpec((CHUNK, d), lambda b, c: (b * nchunk + c, 0))
    return pl.pallas_call(
        _mixer_kernel,
        grid=(batch, nchunk),
        in_specs=[blk(0), blk(1), blk(2), blk(0), blk(1),
                  pl.BlockSpec((CHUNK, d), lambda b, c: (b * nchunk + c, 0)),
                  const((d, LANES)), const((CONV_K, 2 * d)), const((1, 2 * d)), const((1, LANES)),
                  const((1, LANES)), const((1, d)), const((1, d)), const((1, d)), const((1, d)),
                  const((GM_GROUPS, CHUNK, CHUNK)), const((CHUNK, d)), const((LANES, d)),
                  const(((CONV_K - 1) * CHUNK, CONV_TAIL + CHUNK))],
        out_specs=[out, out],
        out_shape=[jax.ShapeDtypeStruct((t, d), BF16), jax.ShapeDtypeStruct((t, d), BF16)],
        scratch_shapes=[pltpu.VMEM((CONV_TAIL, d), BF16), pltpu.VMEM((CONV_TAIL, d), BF16),
                        pltpu.VMEM((SSD_STATE, d), F32)],
        compiler_params=_cparams("arbitrary", "arbitrary"),
        name="mixer",
    )(zxa, zxa, zxa, zxb, zxb, xb, p["w_dt"], p["conv_w"], p["conv_b"], p["dt_bias"], p["a_log"],
      p["d_skip"], p["ssd_norm_w"], p["gm_ln_g"], p["gm_ln_b"], p["w_sp"], p["b_sp"], p["head_expand"],
      p["conv_shift"])


def _merge_kernel(ys_ref, yg_ref, gs_ref, gg_ref, x_ref, ps_ref, pg_ref, wo_ref, g_ref, b_ref,
                  of_ref, ob_ref, psb_ref, pgb_ref, wob_ref):
    @pl.when(pl.program_id(0) == 0)
    def _():
        _cast_weight(psb_ref, ps_ref.at[0])
        _cast_weight(pgb_ref, pg_ref.at[0])
        _cast_weight(wob_ref, wo_ref.at[0])

    h = (_sigmoid(gs_ref[...].astype(F32)) * _dot(ys_ref[...], psb_ref[...])
         + _sigmoid(gg_ref[...].astype(F32)) * _dot(yg_ref[...], pgb_ref[...]))
    mix = _dot(h.astype(BF16), wob_ref[...])
    y = _layer_norm(DN_ALPHA * x_ref[...] + mix, g_ref[...], b_ref[...])
    of_ref[...] = y
    ob_ref[...] = y.astype(BF16)


def _merge(y_ssd, y_gm, zxb, xf, l, w, p):
    t, d = xf.shape
    row = pl.BlockSpec((ROW_BLOCK, d), lambda i: (i, 0))
    mat = _layer_weight(l, d, d)
    vec = pl.BlockSpec((1, d), lambda i: (0, 0))
    return pl.pallas_call(
        _merge_kernel,
        grid=(t // ROW_BLOCK,),
        in_specs=[row, row, pl.BlockSpec((ROW_BLOCK, d), lambda i: (i, 2)),
                  pl.BlockSpec((ROW_BLOCK, d), lambda i: (i, 3)), row, mat, mat, mat, vec, vec],
        out_specs=[row, row],
        out_shape=[jax.ShapeDtypeStruct((t, d), F32), jax.ShapeDtypeStruct((t, d), BF16)],
        scratch_shapes=[pltpu.VMEM((d, d), BF16)] * 3,
        compiler_params=_cparams("arbitrary"),
        name="merge",
    )(y_ssd, y_gm, zxb, zxb, xf, w["p_ssd"], w["p_gm"], w["w_out"], p["ln_g0"], p["ln_b0"])


def _to_token_tiles(ref, y):
    m = y.shape[0]
    for j in range(SUBLANES):
        ref[pl.ds(j, m, stride=SUBLANES), :] = y[:, j * LANES:(j + 1) * LANES]


def _from_token_tiles(ref, m):
    return jnp.concatenate([ref[pl.ds(j, m, stride=SUBLANES), :] for j in range(SUBLANES)], axis=1)


def _attn_kernel(xb_ref, xf_ref, kv_ref, wq_ref, wo_ref, g_ref, b_ref, of_ref, ob_ref, og_ref,
                 wqb_ref, wob_ref):
    @pl.when(jnp.logical_and(pl.program_id(0) == 0, pl.program_id(1) == 0))
    def _():
        _cast_weight(wqb_ref, wq_ref.at[0])
        _cast_weight(wob_ref, wo_ref.at[0])

    q = _dot(xb_ref[...], wqb_ref[...]).astype(BF16)
    kv = kv_ref[...]
    outs = []
    for h in range(XA_HEADS):
        cols = slice(h * XA_HEAD_DIM, (h + 1) * XA_HEAD_DIM)
        s = _dot_nt(q[:, cols], kv[:, cols]) * (XA_HEAD_DIM ** -0.5)
        e = jnp.exp(s - jnp.max(s, axis=-1, keepdims=True))
        p = (e / jnp.sum(e, axis=-1, keepdims=True)).astype(BF16)
        outs.append(_dot(p, kv[:, D_MODEL + h * XA_HEAD_DIM:D_MODEL + (h + 1) * XA_HEAD_DIM]))
    o = jnp.concatenate(outs, axis=1).astype(BF16)
    y = _layer_norm(DN_ALPHA * xf_ref[...] + _dot(o, wob_ref[...]), g_ref[...], b_ref[...])
    of_ref[...] = y
    ob_ref[...] = y.astype(BF16)
    _to_token_tiles(og_ref, y)


def _cross_attn(xb, xf, kv, l, w, p, batch, seq, mem_len):
    t, d = xf.shape
    nblk = seq // ROW_BLOCK
    row = pl.BlockSpec((ROW_BLOCK, d), lambda b, i: (b * nblk + i, 0))
    mat = _layer_weight(l, d, d)
    vec = pl.BlockSpec((1, d), lambda b, i: (0, 0))
    return pl.pallas_call(
        _attn_kernel,
        grid=(batch, nblk),
        in_specs=[row, row, pl.BlockSpec((mem_len, 2 * d), lambda b, i: (b, 0)), mat, mat, vec, vec],
        out_specs=[row, row, pl.BlockSpec((ROW_BLOCK * SUBLANES, LANES), lambda b, i: (b * nblk + i, 0))],
        out_shape=[jax.ShapeDtypeStruct((t, d), F32), jax.ShapeDtypeStruct((t, d), BF16),
                   jax.ShapeDtypeStruct((t * SUBLANES, LANES), F32)],
        scratch_shapes=[pltpu.VMEM((d, d), BF16)] * 2,
        compiler_params=_cparams("arbitrary", "arbitrary"),
        name="cross_attn",
    )(xb, xf, kv, w["wq"], w["wo"], p["ln_g1"], p["ln_b1"])


def _rows_from(rows, shape):
    sub = lax.broadcasted_iota(jnp.int32, shape, 0)
    out = jnp.zeros(shape, rows[0].dtype)
    for k, r in enumerate(rows):
        out = jnp.where(sub == k, r, out)
    return out


def _col_to_row(col):
    return jnp.broadcast_to(col, (LANES, LANES)).T[0:1]


def _row_to_col(row):
    return jnp.broadcast_to(row, (LANES, LANES)).T[:, 0:1]


def _router_kernel(xb_ref, wr_ref, br_ref, upper_ref, gate_ref, lpos_ref, bcnt_ref, bbase_ref, bstart_ref,
                   carry_ref):
    @pl.when(pl.program_id(0) == 0)
    def _():
        carry_ref[...] = jnp.zeros_like(carry_ref)

    m = xb_ref.shape[0]
    logits = _dot(xb_ref[...], wr_ref[...]) + br_ref[...]
    lt = jnp.concatenate([logits[r:r + LANES].T for r in range(0, m, LANES)], axis=1)[:N_EXPERTS]
    sub = lax.broadcasted_iota(jnp.int32, lt.shape, 0)
    tops, hots = [], []
    for _ in range(TOP_K):
        top = jnp.max(lt, axis=0, keepdims=True)
        idx = jnp.min(jnp.where(lt == top, sub, N_EXPERTS), axis=0, keepdims=True)
        hot = sub == idx
        lt = jnp.where(hot, -jnp.inf, lt)
        tops.append(top)
        hots.append(hot)
    es = [jnp.exp(v - tops[0]) for v in tops]
    den = es[0] + es[1] + es[2] + es[3]

    hot_all = jnp.zeros(lt.shape, F32)
    for hot in hots:
        hot_all = hot_all + jnp.where(hot, 1.0, 0.0)
    cnt_col = jnp.sum(hot_all, axis=1, keepdims=True)
    cnt = _col_to_row(jnp.concatenate([cnt_col, jnp.zeros((LANES - N_EXPERTS, 1), F32)], axis=0))
    er = lax.broadcasted_iota(jnp.int32, (LANES, LANES), 0)
    ec = lax.broadcasted_iota(jnp.int32, (LANES, LANES), 1)
    lower_experts = jnp.where(er < ec, 1.0, 0.0).astype(BF16)
    lstart = _dot_exact_lhs(jnp.broadcast_to(cnt, (SUBLANES, LANES)), lower_experts)[0:1]
    local = _row_to_col(lstart)[:N_EXPERTS] + _dot(hot_all.astype(BF16), upper_ref[...])
    lpos = [jnp.sum(jnp.where(hot, local, 0.0), axis=0, keepdims=True) for hot in hots]

    half = jnp.bitwise_and(pl.program_id(0), 1).astype(F32) * PAIRS_PER_BLOCK
    lines = [(v + half) * SUBLANES for v in lpos]
    gate_ref[...] = _rows_from([e / den for e in es], gate_ref.shape)
    lpos_ref[...] = _rows_from(lines, lpos_ref.shape).astype(jnp.int32)
    bcnt_ref[0] = cnt.astype(jnp.int32)
    bbase_ref[0] = carry_ref[...].astype(jnp.int32)
    bstart_ref[0] = lstart.astype(jnp.int32)
    carry_ref[...] = carry_ref[...] + cnt


def _router(xb, p):
    t, d = xb.shape
    nblk = t // ROUTE_BLOCK
    per_token = pl.BlockSpec((SUBLANES, ROUTE_BLOCK), lambda i: (i, 0))
    one = pl.BlockSpec((1, LANES), lambda i: (0, 0))
    per_block = pl.BlockSpec((1, 1, LANES), lambda i: (i, 0, 0))
    per_block_shape = jax.ShapeDtypeStruct((nblk, 1, LANES), jnp.int32)
    tok = jnp.arange(ROUTE_BLOCK, dtype=jnp.int32)
    earlier = (tok[:, None] < tok[None, :]).astype(BF16)
    return pl.pallas_call(
        _router_kernel,
        grid=(nblk,),
        in_specs=[pl.BlockSpec((ROUTE_BLOCK, d), lambda i: (i, 0)),
                  pl.BlockSpec((d, LANES), lambda i: (0, 0)), one,
                  pl.BlockSpec((ROUTE_BLOCK, ROUTE_BLOCK), lambda i: (0, 0))],
        out_specs=[per_token, per_token, per_block, per_block, per_block],
        out_shape=[jax.ShapeDtypeStruct((nblk * SUBLANES, ROUTE_BLOCK), F32),
                   jax.ShapeDtypeStruct((nblk * SUBLANES, ROUTE_BLOCK), jnp.int32),
                   per_block_shape, per_block_shape, per_block_shape],
        scratch_shapes=[pltpu.VMEM((1, LANES), F32)],
        compiler_params=_cparams("arbitrary"),
        name="router",
    )(xb, p["w_router"], p["b_router"], earlier)


def _rows(ref, row, n):
    return ref.at[pl.ds(pl.multiple_of(row * SUBLANES, SUBLANES), n * SUBLANES)]


def _range_copies(n, near_ref, near_row, far_hbm, far_row, sem, to_far, wait=False, same_near=False):
    p = ROUTE_BLOCK
    while p >= 1:
        done = jnp.bitwise_and(n, -2 * p)

        @pl.when(jnp.bitwise_and(n, p) != 0)
        def _(p=p, done=done):
            near = _rows(near_ref, near_row if same_near else near_row + done, p)
            far = _rows(far_hbm, far_row + done, p)
            copy = pltpu.make_async_copy(near, far, sem) if to_far else pltpu.make_async_copy(far, near, sem)
            copy.wait() if wait else copy.start()

        p //= 2


def _stage_half(stage_ref, s):
    return _rows(stage_ref, s * PAIRS_PER_BLOCK, PAIRS_PER_BLOCK)


def _tile_at(ref, line):
    return ref.at[pl.ds(pl.multiple_of(line, SUBLANES), SUBLANES)]


def _dispatch_kernel(cnt_ref, lstart_ref, gstart_ref, fill_ref, lpos_ref, xg_ref, xs_hbm,
                     stage_ref, zero_ref, sem, zsem):
    b = pl.program_id(0)
    last = pl.num_programs(0) - 1
    slot = jnp.bitwise_and(b, 1)

    def wait_half(s):
        pltpu.make_async_copy(_stage_half(stage_ref, s), _rows(xs_hbm, 0, PAIRS_PER_BLOCK), sem.at[s]).wait()

    @pl.when(b == 0)
    def _():
        zero_ref[...] = jnp.zeros_like(zero_ref)

        def zfill(e, wait):
            _range_copies(fill_ref[N_EXPERTS + 1 + e], zero_ref, 0, xs_hbm, fill_ref[e], zsem, True,
                          wait=wait, same_near=True)

        def zfill_start(e, carry):
            zfill(e, False)
            return carry

        def zfill_wait(e, carry):
            zfill(e, True)
            return carry

        lax.fori_loop(0, N_EXPERTS, zfill_start, 0)
        lax.fori_loop(0, N_EXPERTS, zfill_wait, 0)

        def tail_copy(i):
            return pltpu.make_async_copy(zero_ref, _rows(xs_hbm, i * EXPERT_TILE, EXPERT_TILE), zsem)

        def tail_start(i, carry):
            tail_copy(i).start()
            return carry

        def tail_wait(i, carry):
            tail_copy(i).wait()
            return carry

        n_all = xs_hbm.shape[0] // TILE_LINES
        lax.fori_loop(fill_ref[N_EXPERTS], n_all, tail_start, 0)
        lax.fori_loop(fill_ref[N_EXPERTS], n_all, tail_wait, 0)

    lines_of = [lpos_ref.at[pl.ds(k * ROUTE_BLOCK, ROUTE_BLOCK)] for k in range(TOP_K)]

    def fill(c, carry):
        for u in range(ROW_UNROLL):
            t = c * ROW_UNROLL + u
            tile = _tile_at(xg_ref, t * SUBLANES)[...]
            for k in range(TOP_K):
                _tile_at(stage_ref, lines_of[k][t])[...] = tile
        return carry

    lax.fori_loop(0, ROUTE_BLOCK // ROW_UNROLL, fill, 0)

    def ranges(e, carry):
        j = b * N_EXPERTS + e
        _range_copies(cnt_ref[j], stage_ref, slot * PAIRS_PER_BLOCK + lstart_ref[j], xs_hbm, gstart_ref[j],
                      sem.at[slot], True)
        return carry

    lax.fori_loop(0, N_EXPERTS, ranges, 0)

    @pl.when(b > 0)
    def _():
        wait_half(1 - slot)

    @pl.when(b == last)
    def _():
        wait_half(slot)


def _dispatch(xg, lpos, cnt, lstart, gstart, fill_from, n_slots):
    block_lines = ROUTE_BLOCK * SUBLANES
    nblk = xg.shape[0] // block_lines
    grid_spec = pltpu.PrefetchScalarGridSpec(
        num_scalar_prefetch=4,
        grid=(nblk,),
        in_specs=[pl.BlockSpec((PAIRS_PER_BLOCK,), lambda b, *_: (b,), memory_space=pltpu.SMEM),
                  pl.BlockSpec((block_lines, LANES), lambda b, *_: (b, 0))],
        out_specs=pl.BlockSpec(memory_space=pl.ANY),
        scratch_shapes=[pltpu.VMEM((2 * PAIRS_PER_BLOCK * SUBLANES, LANES), F32),
                        pltpu.VMEM((TILE_LINES, LANES), F32),
                        pltpu.SemaphoreType.DMA((2,)), pltpu.SemaphoreType.DMA(())],
    )
    return pl.pallas_call(
        _dispatch_kernel,
        grid_spec=grid_spec,
        out_shape=jax.ShapeDtypeStruct((n_slots * SUBLANES, LANES), F32),
        compiler_params=_cparams("arbitrary"),
        name="moe_dispatch",
    )(cnt, lstart, gstart, fill_from, lpos, xg)


def _expert_kernel(l, te_ref, nxt_ref, nu_ref, x_ref, wgu_hbm, bgu_ref, wd_hbm, bd_ref, o_ref,
                   wguf_ref, wdf_ref, wgub_ref, wdb_ref, sem):
    i = pl.program_id(0)

    def fetch(e):
        return (pltpu.make_async_copy(wgu_hbm.at[l, e], wguf_ref, sem.at[0]),
                pltpu.make_async_copy(wd_hbm.at[l, e], wdf_ref, sem.at[1]))

    @pl.when(i == 0)
    def _():
        for copy in fetch(te_ref[0]):
            copy.start()

    @pl.when(jnp.logical_or(i == 0, te_ref[i] != te_ref[jnp.maximum(i - 1, 0)]))
    def _():
        for copy in fetch(te_ref[i]):
            copy.wait()
        _cast_weight(wgub_ref, wguf_ref)
        _cast_weight(wdb_ref, wdf_ref)

        @pl.when(nxt_ref[i] >= 0)
        def _():
            for copy in fetch(nxt_ref[i]):
                copy.start()

    @pl.when(i < nu_ref[0])
    def _():
        x = _from_token_tiles(x_ref, EXPERT_TILE).astype(BF16)
        hgu = _dot(x, wgub_ref[...]) + bgu_ref[0, 0]
        gate = jnp.minimum(hgu[:, :D_EXPERT], SWIGLU_LIMIT)
        up = jnp.clip(hgu[:, D_EXPERT:], -SWIGLU_LIMIT, SWIGLU_LIMIT)
        glu = gate * _sigmoid(SWIGLU_ALPHA * gate)
        _to_token_tiles(o_ref, _dot(((up + 1.0) * glu).astype(BF16), wdb_ref[...]) + bd_ref[0, 0])

    @pl.when(i >= nu_ref[0])
    def _():
        o_ref[...] = jnp.zeros_like(o_ref)


def _experts(xs, tile_expert, n_used, n_tiles, l, w):
    d = D_MODEL

    def of_expert(shape):
        return pl.BlockSpec((1, 1) + shape, lambda i, te, nxt, nu: (l, te[i], 0, 0))

    later = jnp.where(tile_expert[None, :] > tile_expert[:, None], tile_expert[None, :], N_EXPERTS)
    next_expert = jnp.min(later, axis=1)
    next_expert = jnp.where(next_expert == N_EXPERTS, -1, next_expert).astype(jnp.int32)

    grid_spec = pltpu.PrefetchScalarGridSpec(
        num_scalar_prefetch=3,
        grid=(n_tiles,),
        in_specs=[
            pl.BlockSpec((TILE_LINES, LANES), lambda i, te, nxt, nu: (jnp.minimum(i, nu[0] - 1), 0)),
            pl.BlockSpec(memory_space=pl.ANY), of_expert((1, 2 * D_EXPERT)),
            pl.BlockSpec(memory_space=pl.ANY), of_expert((1, d)),
        ],
        out_specs=pl.BlockSpec((TILE_LINES, LANES), lambda i, te, nxt, nu: (i, 0)),
        scratch_shapes=[pltpu.VMEM((d, 2 * D_EXPERT), F32), pltpu.VMEM((D_EXPERT, d), F32),
                        pltpu.VMEM((d, 2 * D_EXPERT), BF16), pltpu.VMEM((D_EXPERT, d), BF16),
                        pltpu.SemaphoreType.DMA((2,))],
    )
    return pl.pallas_call(
        functools.partial(_expert_kernel, l),
        grid_spec=grid_spec,
        out_shape=jax.ShapeDtypeStruct((n_tiles * TILE_LINES, LANES), F32),
        compiler_params=_cparams("arbitrary"),
        name="moe_experts",
    )(tile_expert, next_expert, n_used, xs, w["w_gu"], w["b_gu"], w["w_down"], w["b_down"])


def _combine_kernel(cnt_ref, lstart_ref, gstart_ref, lpos_ref, gate_ref, ys_hbm, x_ref, g_ref, b_ref,
                    of_ref, ob_ref, stage_ref, acc_ref, sem):
    b = pl.program_id(0)
    last = pl.num_programs(0) - 1
    slot = jnp.bitwise_and(b, 1)

    def fetch(blk, s):
        def ranges(e, carry):
            j = blk * N_EXPERTS + e
            _range_copies(cnt_ref[j], stage_ref, s * PAIRS_PER_BLOCK + lstart_ref[j], ys_hbm, gstart_ref[j],
                          sem.at[s], False)
            return carry

        lax.fori_loop(0, N_EXPERTS, ranges, 0)

    @pl.when(b == 0)
    def _():
        fetch(b, slot)

    @pl.when(b < last)
    def _():
        fetch(b + 1, 1 - slot)

    pltpu.make_async_copy(_rows(ys_hbm, 0, PAIRS_PER_BLOCK), _stage_half(stage_ref, slot), sem.at[slot]).wait()

    lines_of = [lpos_ref.at[pl.ds(k * ROUTE_BLOCK, ROUTE_BLOCK)] for k in range(TOP_K)]
    gates_of = [gate_ref.at[pl.ds(k * ROUTE_BLOCK, ROUTE_BLOCK)] for k in range(TOP_K)]

    def gather(c, carry):
        for u in range(ROW_UNROLL):
            t = c * ROW_UNROLL + u
            tile = jnp.zeros((SUBLANES, LANES), F32)
            for k in range(TOP_K):
                tile = tile + gates_of[k][t] * _tile_at(stage_ref, lines_of[k][t])[...]
            _tile_at(acc_ref, t * SUBLANES)[...] = tile
        return carry

    lax.fori_loop(0, ROUTE_BLOCK // ROW_UNROLL, gather, 0)

    y = _layer_norm(DN_ALPHA * x_ref[...] + _from_token_tiles(acc_ref, ROUTE_BLOCK), g_ref[...], b_ref[...])
    of_ref[...] = y
    ob_ref[...] = y.astype(BF16)


def _combine(ys, lpos, gates, cnt, lstart, gstart, xf, p):
    t, d = xf.shape
    nblk = t // ROUTE_BLOCK
    row = pl.BlockSpec((ROUTE_BLOCK, d), lambda b, *_: (b, 0))
    vec = pl.BlockSpec((1, d), lambda b, *_: (0, 0))
    pairs = pl.BlockSpec((PAIRS_PER_BLOCK,), lambda b, *_: (b,), memory_space=pltpu.SMEM)
    grid_spec = pltpu.PrefetchScalarGridSpec(
        num_scalar_prefetch=3,
        grid=(nblk,),
        in_specs=[pairs, pairs, pl.BlockSpec(memory_space=pl.ANY), row, vec, vec],
        out_specs=[row, row],
        scratch_shapes=[pltpu.VMEM((2 * PAIRS_PER_BLOCK * SUBLANES, LANES), F32),
                        pltpu.VMEM((ROUTE_BLOCK * SUBLANES, LANES), F32), pltpu.SemaphoreType.DMA((2,))],
    )
    return pl.pallas_call(
        _combine_kernel,
        grid_spec=grid_spec,
        out_shape=[jax.ShapeDtypeStruct((t, d), F32), jax.ShapeDtypeStruct((t, d), BF16)],
        compiler_params=_cparams("arbitrary"),
        name="moe_combine",
    )(cnt, lstart, gstart, lpos, gates, ys, xf, p["ln_g2"], p["ln_b2"])


def _moe(xb, xf, xg, l, w, p):
    t, d = xf.shape
    gates, lpos, bcnt, bbase, bstart = _router(xb, p)
    bcnt = bcnt[:, 0, :N_EXPERTS]
    counts = jnp.sum(bcnt, axis=0)
    padded = (counts + EXPERT_TILE - 1) // EXPERT_TILE * EXPERT_TILE
    ends = jnp.cumsum(padded)
    starts = ends - padded
    gstart = (starts[None, :] + bbase[:, 0, :N_EXPERTS]).reshape(-1)
    lstart = bstart[:, 0, :N_EXPERTS].reshape(-1)
    n_tiles = (t * TOP_K) // EXPERT_TILE + N_EXPERTS
    n_used = ends[-1] // EXPERT_TILE
    tile_start = jnp.minimum(jnp.arange(n_tiles, dtype=jnp.int32), n_used - 1) * EXPERT_TILE
    tile_expert = jnp.sum((ends[None, :] <= tile_start[:, None]).astype(jnp.int32), axis=1)

    fill_from = jnp.concatenate([starts + counts, n_used.reshape(1), padded - counts])
    n_slots = n_tiles * EXPERT_TILE
    def k_major(v):
        return v.reshape(-1, SUBLANES, ROUTE_BLOCK)[:, :TOP_K].reshape(-1)

    lpos, gates = k_major(lpos), k_major(gates)
    xs = _dispatch(xg, lpos, bcnt.reshape(-1), lstart, gstart, fill_from, n_slots)
    ys = _experts(xs, tile_expert, n_used.reshape(1), n_tiles, l, w)
    return _combine(ys, lpos, gates, bcnt.reshape(-1), lstart, gstart, xf, p)


def _layer_params(l, w_in, conv_w, conv_b, dt_bias, a_log, d_skip, ssd_norm_w, gm_ln_g, gm_ln_b,
                  w_sp, b_sp, w_router, b_router, ln_g, ln_b):
    d = D_MODEL
    off_dt = d + conv_w.shape[-1]
    off_u = off_dt + SSD_HEADS
    pad_h = LANES - SSD_HEADS
    pad_e = LANES - N_EXPERTS
    head_of_channel = jnp.arange(d, dtype=jnp.int32) // SSD_HEAD_DIM
    shift_row = jnp.arange((CONV_K - 1) * CHUNK, dtype=jnp.int32)
    shift_col = CONV_TAIL + shift_row % CHUNK - (1 + shift_row // CHUNK)
    return {
        "conv_shift": (jnp.arange(CONV_TAIL + CHUNK, dtype=jnp.int32)[None, :] == shift_col[:, None]).astype(BF16),
        "w_dt": jnp.pad(w_in[l, :, off_dt:off_u], ((0, 0), (0, pad_h))).astype(BF16),
        "conv_w": conv_w[l], "conv_b": conv_b[l].reshape(1, -1),
        "dt_bias": jnp.pad(dt_bias[l], (0, pad_h)).reshape(1, LANES),
        "a_log": jnp.pad(a_log[l], (0, pad_h)).reshape(1, LANES),
        "d_skip": d_skip[l][head_of_channel].reshape(1, d),
        "ssd_norm_w": ssd_norm_w[l].reshape(1, d),
        "gm_ln_g": gm_ln_g[l].reshape(1, d), "gm_ln_b": gm_ln_b[l].reshape(1, d),
        "w_sp": w_sp[l],
        "b_sp": jnp.repeat(b_sp[l].T, GM_GROUP_DIM, axis=1),
        "head_expand": (jnp.arange(LANES, dtype=jnp.int32)[:, None] == head_of_channel[None, :]).astype(BF16),
        "w_router": jnp.pad(w_router[l], ((0, 0), (0, pad_e))).astype(BF16),
        "b_router": jnp.pad(b_router[l], (0, pad_e), constant_values=NEG_BIG).reshape(1, LANES),
        "ln_g0": ln_g[l, 0].reshape(1, d), "ln_b0": ln_b[l, 0].reshape(1, d),
        "ln_g1": ln_g[l, 1].reshape(1, d), "ln_b1": ln_b[l, 1].reshape(1, d),
        "ln_g2": ln_g[l, 2].reshape(1, d), "ln_b2": ln_b[l, 2].reshape(1, d),
    }


def kernel(x, mem, ln0_g, ln0_b, w_in, conv_w, conv_b, dt_bias, a_log, d_skip, ssd_norm_w, gm_ln_g, gm_ln_b, w_sp, b_sp, p_ssd, p_gm, w_out, wq, wk, wv, wo, w_router, b_router, w_gu, b_gu, w_down, b_down, ln_g, ln_b):
    batch, seq, d = x.shape
    mem_len = mem.shape[1]
    depth = w_in.shape[0]
    assert d == D_MODEL and seq % ROW_BLOCK == 0 and seq % CHUNK == 0
    t = batch * seq
    memb = mem.reshape(batch * mem_len, d)
    off_u = d + conv_w.shape[-1] + SSD_HEADS
    w = {"w_in_tail": w_in[:, :, off_u:],
         "w_kv": jnp.concatenate([wk, wv], axis=2),
         "p_ssd": p_ssd, "p_gm": p_gm, "w_out": w_out, "wq": wq, "wo": wo,
         "w_gu": w_gu, "b_gu": b_gu.reshape(depth, N_EXPERTS, 1, -1),
         "w_down": w_down, "b_down": b_down.reshape(depth, N_EXPERTS, 1, -1)}
    xf, xb = _entry_ln(x.reshape(t, d), ln0_g, ln0_b)
    for l in range(depth):
        p = _layer_params(l, w_in, conv_w, conv_b, dt_bias, a_log, d_skip, ssd_norm_w, gm_ln_g,
                          gm_ln_b, w_sp, b_sp, w_router, b_router, ln_g, ln_b)
        bm = min(MM_BLOCK_M, t)
        zxa = _matmul(xb, w_in, l, 3 * d, bm)
        zxb = _matmul(xb, w["w_in_tail"], l, 4 * d, bm)
        y_ssd, y_gm = _mixer(zxa, zxb, xb, p, batch, seq)
        xf, xb = _merge(y_ssd, y_gm, zxb, xf, l, w, p)
        kv = _matmul(memb, w["w_kv"], l, 2 * d, min(MM_BLOCK_M, batch * mem_len))
        xf, xb, xg = _cross_attn(xb, xf, kv, l, w, p, batch, seq, mem_len)
        xf, xb = _moe(xb, xf, xg, l, w, p)
    return xf.reshape(batch, seq, d)
```

```python
import functools
import math

import jax
import jax.numpy as jnp
from jax import lax
from jax.experimental import pallas as pl
from jax.experimental.pallas import tpu as pltpu

F32 = jnp.float32
BF16 = jnp.bfloat16

D_MODEL = 1024
DEPTH = 2
CHUNK = 128
SSD_HEADS = 16
SSD_HEAD_DIM = 64
SSD_GROUPS = 4
SSD_HPG = SSD_HEADS // SSD_GROUPS
SSD_STATE = 128
SSD_GROUP_W = SSD_HPG * SSD_HEAD_DIM
CONV_K = 4
CONV_TAIL = CHUNK
GM_GROUPS = 8
GM_GROUP_DIM = D_MODEL // GM_GROUPS
XA_HEADS = 4
XA_HEAD_DIM = D_MODEL // XA_HEADS
N_EXPERTS = 32
TOP_K = 4
D_EXPERT = D_MODEL
SWIGLU_LIMIT = 7.0
SWIGLU_ALPHA = 1.702
DN_ALPHA = (2 * DEPTH) ** 0.25
EPS = 1e-5

LANES = 128
SUBLANES = 8
VMEM_LIMIT = 56 * 1024 * 1024

ROW_BLOCK = 512
MM_BLOCK_M = 2048
MM_BLOCK_N = 1024
CAST_ROWS = 128
ROUTE_BLOCK = 512
EXPERT_TILE = 512
NEG_BIG = -1e30
TILE_LINES = EXPERT_TILE * SUBLANES
PAIRS_PER_BLOCK = ROUTE_BLOCK * TOP_K
ROW_UNROLL = 8


def _cparams(*sem):
    return pltpu.CompilerParams(dimension_semantics=sem, vmem_limit_bytes=VMEM_LIMIT)


def _layer_norm(x, g, b):
    mu = jnp.mean(x, axis=-1, keepdims=True)
    xc = x - mu
    var = jnp.mean(xc * xc, axis=-1, keepdims=True)
    return xc * lax.rsqrt(var + EPS) * g + b


def _dot(a, b):
    return jnp.dot(a, b, preferred_element_type=F32)


def _dot_nt(a, b):
    return lax.dot_general(a, b, (((1,), (1,)), ((), ())), preferred_element_type=F32)


def _dot_tn(a, b):
    return lax.dot_general(a, b, (((0,), (0,)), ((), ())), preferred_element_type=F32)


def _split3(v):
    hi = v.astype(BF16)
    r1 = v - hi.astype(F32)
    mid = r1.astype(BF16)
    lo = (r1 - mid.astype(F32)).astype(BF16)
    return hi, mid, lo


def _dot_exact_rhs(sel, v):
    hi, mid, lo = _split3(v)
    return _dot(sel, hi) + _dot(sel, mid) + _dot(sel, lo)


def _dot_exact_lhs(v, sel):
    hi, mid, lo = _split3(v)
    return _dot(hi, sel) + _dot(mid, sel) + _dot(lo, sel)


def _dot_select_lhs(v, sel):
    hi = v.astype(BF16)
    lo = (v - hi.astype(F32)).astype(BF16)
    return _dot(hi, sel) + _dot(lo, sel)


def _sigmoid(x):
    return 1.0 / (1.0 + jnp.exp(-x))


def _gelu(x):
    return 0.5 * x * (1.0 + lax.erf(x * math.sqrt(0.5)))


def _softplus(x):
    return jnp.maximum(x, 0.0) + jnp.log1p(jnp.exp(-jnp.abs(x)))


def _cast_weight(dst_ref, src_ref):
    def body(c, carry):
        rows = pl.ds(pl.multiple_of(c * CAST_ROWS, CAST_ROWS), CAST_ROWS)
        dst_ref[rows, :] = src_ref[rows, :].astype(BF16)
        return carry

    lax.fori_loop(0, src_ref.shape[0] // CAST_ROWS, body, 0)


def _layer_weight(l, k, n):
    return pl.BlockSpec((1, k, n), lambda *_: (l, 0, 0), pipeline_mode=pl.Buffered(1))


def _ln_kernel(x_ref, g_ref, b_ref, of_ref, ob_ref):
    y = _layer_norm(x_ref[...], g_ref[...], b_ref[...])
    of_ref[...] = y
    ob_ref[...] = y.astype(BF16)


def _entry_ln(x, g, b):
    t, d = x.shape
    row = pl.BlockSpec((ROW_BLOCK, d), lambda i: (i, 0))
    vec = pl.BlockSpec((1, d), lambda i: (0, 0))
    return pl.pallas_call(
        _ln_kernel,
        grid=(t // ROW_BLOCK,),
        in_specs=[row, vec, vec],
        out_specs=[row, row],
        out_shape=[jax.ShapeDtypeStruct((t, d), F32), jax.ShapeDtypeStruct((t, d), BF16)],
        compiler_params=_cparams("arbitrary"),
        name="entry_ln",
    )(x, g.reshape(1, d), b.reshape(1, d))


def _mm_kernel(a_ref, w_ref, o_ref, wb_ref):
    @pl.when(pl.program_id(1) == 0)
    def _():
        _cast_weight(wb_ref, w_ref.at[0])

    o_ref[...] = _dot(a_ref[...].astype(BF16), wb_ref[...]).astype(o_ref.dtype)


def _matmul(a, w, l, n, bm):
    m, k = a.shape
    bn = MM_BLOCK_N
    return pl.pallas_call(
        _mm_kernel,
        grid=(n // bn, m // bm),
        in_specs=[pl.BlockSpec((bm, k), lambda j, i: (i, 0)),
                  pl.BlockSpec((1, k, bn), lambda j, i: (l, 0, j))],
        out_specs=pl.BlockSpec((bm, bn), lambda j, i: (i, j)),
        out_shape=jax.ShapeDtypeStruct((m, n), BF16),
        scratch_shapes=[pltpu.VMEM((k, bn), BF16)],
        compiler_params=_cparams("arbitrary", "arbitrary"),
        name="matmul",
    )(a, w)


def _conv_silu(raw_ref, tail_ref, cols, shift, w, b):
    raw = raw_ref[:, cols]
    aug = jnp.concatenate([tail_ref[:, cols], raw], axis=0)
    shifted = _dot(shift, aug)
    acc = raw.astype(F32) * w[CONV_K - 1:CONV_K] + b
    for j in range(1, CONV_K):
        acc = acc + shifted[(j - 1) * CHUNK:j * CHUNK] * w[CONV_K - 1 - j:CONV_K - j]
    tail_ref[:, cols] = raw[CHUNK - CONV_TAIL:CHUNK]
    return acc * _sigmoid(acc)


def _mixer_kernel(z_ref, xs_ref, bc_ref, u_ref, v_ref, xb_ref, wdt_ref, cw_ref, cb_ref, dtb_ref,
                  alog_ref, dskip_ref, nw_ref, lng_ref, lnb_ref, wsp_ref, bsp_ref, hexp_ref, shift_ref,
                  yssd_ref, ygm_ref, tailx_ref, tailbc_ref, state_ref):
    @pl.when(pl.program_id(1) == 0)
    def _():
        tailx_ref[...] = jnp.zeros_like(tailx_ref)
        tailbc_ref[...] = jnp.zeros_like(tailbc_ref)
        state_ref[...] = jnp.zeros_like(state_ref)

    row = lax.broadcasted_iota(jnp.int32, (CHUNK, CHUNK), 0)
    col = lax.broadcasted_iota(jnp.int32, (CHUNK, CHUNK), 1)
    causal = col <= row
    tri = jnp.where(causal, 1.0, 0.0).astype(BF16)

    cw = cw_ref[...]
    cb = cb_ref[...]
    shift = shift_ref[...]
    bc = _conv_silu(bc_ref, tailbc_ref, slice(0, D_MODEL), shift, cw[:, D_MODEL:], cb[:, D_MODEL:])
    gn = SSD_GROUPS * SSD_STATE

    dt = _softplus(_dot(xb_ref[...], wdt_ref[...]) + dtb_ref[...])
    a = -jnp.exp(alog_ref[...])
    cs = _dot_exact_rhs(tri, dt * a)
    cs_t = cs.T
    seg = lax.shift_right_logical(lax.broadcasted_iota(jnp.int32, (CHUNK, SSD_GROUP_W), 1),
                                  int(math.log2(SSD_HEAD_DIM)))

    for g in range(SSD_GROUPS):
        cols = slice(g * SSD_GROUP_W, (g + 1) * SSD_GROUP_W)
        xs = _conv_silu(xs_ref, tailx_ref, cols, shift, cw[:, cols], cb[:, cols])
        hexp = hexp_ref[:, cols]
        dt_x = _dot_select_lhs(dt, hexp)
        cs_x = _dot_select_lhs(cs, hexp)
        tot_x = cs_x[CHUNK - 1:CHUNK]
        xdt = xs * dt_x
        xdt_b = xdt.astype(BF16)
        xdec_b = (xdt * jnp.exp(tot_x - cs_x)).astype(BF16)

        b_g = bc[:, g * SSD_STATE:(g + 1) * SSD_STATE].astype(BF16)
        c_g = bc[:, gn + g * SSD_STATE:gn + (g + 1) * SSD_STATE].astype(BF16)
        cb_g = _dot_nt(c_g, b_g)
        st = state_ref[:, cols]
        y_g = _dot(c_g, st.astype(BF16)) * jnp.exp(cs_x)
        for hh in range(SSD_HPG):
            h = g * SSD_HPG + hh
            diff = cs[:, h:h + 1] - cs_t[h:h + 1, :]
            m_h = (cb_g * jnp.exp(jnp.where(causal, diff, -jnp.inf))).astype(BF16)
            y_g = y_g + jnp.where(seg == hh, _dot(m_h, xdt_b), 0.0)
        state_ref[:, cols] = st * jnp.exp(tot_x) + _dot_tn(b_g, xdec_b)

        z = z_ref[:, cols].astype(F32)
        y_g = (y_g + dskip_ref[:, cols] * xs) * (z * _sigmoid(z))
        y_g = y_g * lax.rsqrt(jnp.mean(y_g * y_g, axis=-1, keepdims=True) + EPS)
        yssd_ref[:, cols] = (y_g * nw_ref[:, cols]).astype(BF16)

    u = _gelu(u_ref[...].astype(F32))
    v = _layer_norm(_gelu(v_ref[...].astype(F32)), lng_ref[...], lnb_ref[...]).astype(BF16)
    sv_parts = []
    for g in range(GM_GROUPS):
        w_g = jnp.where(causal, wsp_ref[g], 0.0).astype(BF16)
        sv_parts.append(_dot(w_g, v[:, g * GM_GROUP_DIM:(g + 1) * GM_GROUP_DIM]))
    ygm_ref[...] = (u * (jnp.concatenate(sv_parts, axis=1) + bsp_ref[...])).astype(BF16)


def _mixer(zxa, zxb, xb, p, batch, seq):
    t, d = xb.shape
    nchunk = seq // CHUNK

    def blk(j):
        return pl.BlockSpec((CHUNK, d), lambda b, c, j=j: (b * nchunk + c, j))

    def const(shape):
        return pl.BlockSpec(shape, lambda b, c: (0,) * len(shape))

    out = pl.BlockSpec((CHUNK, d), lambda b, c: (b * nchunk + c, 0))
    return pl.pallas_call(
        _mixer_kernel,
        grid=(batch, nchunk),
        in_specs=[blk(0), blk(1), blk(2), blk(0), blk(1),
                  pl.BlockSpec((CHUNK, d), lambda b, c: (b * nchunk + c, 0)),
                  const((d, LANES)), const((CONV_K, 2 * d)), const((1, 2 * d)), const((1, LANES)),
                  const((1, LANES)), const((1, d)), const((1, d)), const((1, d)), const((1, d)),
                  const((GM_GROUPS, CHUNK, CHUNK)), const((CHUNK, d)), const((LANES, d)),
                  const(((CONV_K - 1) * CHUNK, CONV_TAIL + CHUNK))],
        out_specs=[out, out],
        out_shape=[jax.ShapeDtypeStruct((t, d), BF16), jax.ShapeDtypeStruct((t, d), BF16)],
        scratch_shapes=[pltpu.VMEM((CONV_TAIL, d), BF16), pltpu.VMEM((CONV_TAIL, d), BF16),
                        pltpu.VMEM((SSD_STATE, d), F32)],
        compiler_params=_cparams("arbitrary", "arbitrary"),
        name="mixer",
    )(zxa, zxa, zxa, zxb, zxb, xb, p["w_dt"], p["conv_w"], p["conv_b"], p["dt_bias"], p["a_log"],
      p["d_skip"], p["ssd_norm_w"], p["gm_ln_g"], p["gm_ln_b"], p["w_sp"], p["b_sp"], p["head_expand"],
      p["conv_shift"])


def _merge_kernel(ys_ref, yg_ref, gs_ref, gg_ref, x_ref, ps_ref, pg_ref, wo_ref, g_ref, b_ref,
                  of_ref, ob_ref, psb_ref, pgb_ref, wob_ref):
    @pl.when(pl.program_id(0) == 0)
    def _():
        _cast_weight(psb_ref, ps_ref.at[0])
        _cast_weight(pgb_ref, pg_ref.at[0])
        _cast_weight(wob_ref, wo_ref.at[0])

    h = (_sigmoid(gs_ref[...].astype(F32)) * _dot(ys_ref[...], psb_ref[...])
         + _sigmoid(gg_ref[...].astype(F32)) * _dot(yg_ref[...], pgb_ref[...]))
    mix = _dot(h.astype(BF16), wob_ref[...])
    y = _layer_norm(DN_ALPHA * x_ref[...] + mix, g_ref[...], b_ref[...])
    of_ref[...] = y
    ob_ref[...] = y.astype(BF16)


def _merge(y_ssd, y_gm, zxb, xf, l, w, p):
    t, d = xf.shape
    row = pl.BlockSpec((ROW_BLOCK, d), lambda i: (i, 0))
    mat = _layer_weight(l, d, d)
    vec = pl.BlockSpec((1, d), lambda i: (0, 0))
    return pl.pallas_call(
        _merge_kernel,
        grid=(t // ROW_BLOCK,),
        in_specs=[row, row, pl.BlockSpec((ROW_BLOCK, d), lambda i: (i, 2)),
                  pl.BlockSpec((ROW_BLOCK, d), lambda i: (i, 3)), row, mat, mat, mat, vec, vec],
        out_specs=[row, row],
        out_shape=[jax.ShapeDtypeStruct((t, d), F32), jax.ShapeDtypeStruct((t, d), BF16)],
        scratch_shapes=[pltpu.VMEM((d, d), BF16)] * 3,
        compiler_params=_cparams("arbitrary"),
        name="merge",
    )(y_ssd, y_gm, zxb, zxb, xf, w["p_ssd"], w["p_gm"], w["w_out"], p["ln_g0"], p["ln_b0"])


def _to_token_tiles(ref, y):
    m = y.shape[0]
    for j in range(SUBLANES):
        ref[pl.ds(j, m, stride=SUBLANES), :] = y[:, j * LANES:(j + 1) * LANES]


def _from_token_tiles(ref, m):
    return jnp.concatenate([ref[pl.ds(j, m, stride=SUBLANES), :] for j in range(SUBLANES)], axis=1)


def _attn_kernel(xb_ref, xf_ref, kv_ref, wq_ref, wo_ref, g_ref, b_ref, wr_ref, br_ref, upper_ref,
                 of_ref, ob_ref, og_ref, gate_ref, lpos_ref, bcnt_ref, bbase_ref, bstart_ref,
                 wqb_ref, wob_ref, carry_ref):
    @pl.when(jnp.logical_and(pl.program_id(0) == 0, pl.program_id(1) == 0))
    def _():
        _cast_weight(wqb_ref, wq_ref.at[0])
        _cast_weight(wob_ref, wo_ref.at[0])
        carry_ref[...] = jnp.zeros_like(carry_ref)

    q = _dot(xb_ref[...], wqb_ref[...]).astype(BF16)
    kv = kv_ref[...]
    outs = []
    for h in range(XA_HEADS):
        cols = slice(h * XA_HEAD_DIM, (h + 1) * XA_HEAD_DIM)
        s = _dot_nt(q[:, cols], kv[:, cols]) * (XA_HEAD_DIM ** -0.5)
        e = jnp.exp(s - jnp.max(s, axis=-1, keepdims=True))
        p = (e / jnp.sum(e, axis=-1, keepdims=True)).astype(BF16)
        outs.append(_dot(p, kv[:, D_MODEL + h * XA_HEAD_DIM:D_MODEL + (h + 1) * XA_HEAD_DIM]))
    o = jnp.concatenate(outs, axis=1).astype(BF16)
    y = _layer_norm(DN_ALPHA * xf_ref[...] + _dot(o, wob_ref[...]), g_ref[...], b_ref[...])
    yb = y.astype(BF16)
    of_ref[...] = y
    ob_ref[...] = yb
    _to_token_tiles(og_ref, y)
    blk = pl.program_id(0) * pl.num_programs(1) + pl.program_id(1)
    _route_block(yb, blk, wr_ref, br_ref, upper_ref, gate_ref, lpos_ref, bcnt_ref, bbase_ref, bstart_ref,
                 carry_ref)


def _cross_attn(xb, xf, kv, l, w, p, batch, seq, mem_len):
    assert ROUTE_BLOCK == ROW_BLOCK
    t, d = xf.shape
    nblk = seq // ROW_BLOCK
    nb = t // ROW_BLOCK
    row = pl.BlockSpec((ROW_BLOCK, d), lambda b, i: (b * nblk + i, 0))
    mat = _layer_weight(l, d, d)
    vec = pl.BlockSpec((1, d), lambda b, i: (0, 0))
    per_token = pl.BlockSpec((SUBLANES, ROUTE_BLOCK), lambda b, i: (b * nblk + i, 0))
    per_block = pl.BlockSpec((1, 1, LANES), lambda b, i: (b * nblk + i, 0, 0))
    per_block_shape = jax.ShapeDtypeStruct((nb, 1, LANES), jnp.int32)
    tok = jnp.arange(ROUTE_BLOCK, dtype=jnp.int32)
    earlier = (tok[:, None] < tok[None, :]).astype(BF16)
    return pl.pallas_call(
        _attn_kernel,
        grid=(batch, nblk),
        in_specs=[row, row, pl.BlockSpec((mem_len, 2 * d), lambda b, i: (b, 0)), mat, mat, vec, vec,
                  pl.BlockSpec((d, LANES), lambda b, i: (0, 0)), pl.BlockSpec((1, LANES), lambda b, i: (0, 0)),
                  pl.BlockSpec((ROUTE_BLOCK, ROUTE_BLOCK), lambda b, i: (0, 0))],
        out_specs=[row, row, pl.BlockSpec((ROW_BLOCK * SUBLANES, LANES), lambda b, i: (b * nblk + i, 0)),
                   per_token, per_token, per_block, per_block, per_block],
        out_shape=[jax.ShapeDtypeStruct((t, d), F32), jax.ShapeDtypeStruct((t, d), BF16),
                   jax.ShapeDtypeStruct((t * SUBLANES, LANES), F32),
                   jax.ShapeDtypeStruct((nb * SUBLANES, ROUTE_BLOCK), F32),
                   jax.ShapeDtypeStruct((nb * SUBLANES, ROUTE_BLOCK), jnp.int32),
                   per_block_shape, per_block_shape, per_block_shape],
        scratch_shapes=[pltpu.VMEM((d, d), BF16), pltpu.VMEM((d, d), BF16), pltpu.VMEM((1, LANES), F32)],
        compiler_params=_cparams("arbitrary", "arbitrary"),
        name="cross_attn",
    )(xb, xf, kv, w["wq"], w["wo"], p["ln_g1"], p["ln_b1"], p["w_router"], p["b_router"], earlier)


def _rows_from(rows, shape):
    sub = lax.broadcasted_iota(jnp.int32, shape, 0)
    out = jnp.zeros(shape, rows[0].dtype)
    for k, r in enumerate(rows):
        out = jnp.where(sub == k, r, out)
    return out


def _col_to_row(col):
    return jnp.broadcast_to(col, (LANES, LANES)).T[0:1]


def _row_to_col(row):
    return jnp.broadcast_to(row, (LANES, LANES)).T[:, 0:1]


def _route_block(xb, blk, wr_ref, br_ref, upper_ref, gate_ref, lpos_ref, bcnt_ref, bbase_ref, bstart_ref,
                 carry_ref):
    m = xb.shape[0]
    logits = _dot(xb, wr_ref[...]) + br_ref[...]
    lt = jnp.concatenate([logits[r:r + LANES].T for r in range(0, m, LANES)], axis=1)[:N_EXPERTS]
    sub = lax.broadcasted_iota(jnp.int32, lt.shape, 0)
    tops, hots = [], []
    for _ in range(TOP_K):
        top = jnp.max(lt, axis=0, keepdims=True)
        idx = jnp.min(jnp.where(lt == top, sub, N_EXPERTS), axis=0, keepdims=True)
        hot = sub == idx
        lt = jnp.where(hot, -jnp.inf, lt)
        tops.append(top)
        hots.append(hot)
    es = [jnp.exp(v - tops[0]) for v in tops]
    den = es[0] + es[1] + es[2] + es[3]

    hot_all = jnp.zeros(lt.shape, F32)
    for hot in hots:
        hot_all = hot_all + jnp.where(hot, 1.0, 0.0)
    cnt_col = jnp.sum(hot_all, axis=1, keepdims=True)
    cnt = _col_to_row(jnp.concatenate([cnt_col, jnp.zeros((LANES - N_EXPERTS, 1), F32)], axis=0))
    er = lax.broadcasted_iota(jnp.int32, (LANES, LANES), 0)
    ec = lax.broadcasted_iota(jnp.int32, (LANES, LANES), 1)
    lower_experts = jnp.where(er < ec, 1.0, 0.0).astype(BF16)
    lstart = _dot_exact_lhs(jnp.broadcast_to(cnt, (SUBLANES, LANES)), lower_experts)[0:1]
    local = _row_to_col(lstart)[:N_EXPERTS] + _dot(hot_all.astype(BF16), upper_ref[...])
    lpos = [jnp.sum(jnp.where(hot, local, 0.0), axis=0, keepdims=True) for hot in hots]

    half = jnp.bitwise_and(blk, 1).astype(F32) * PAIRS_PER_BLOCK
    lines = [(v + half) * SUBLANES for v in lpos]
    gate_ref[...] = _rows_from([e / den for e in es], gate_ref.shape)
    lpos_ref[...] = _rows_from(lines, lpos_ref.shape).astype(jnp.int32)
    bcnt_ref[0] = cnt.astype(jnp.int32)
    bbase_ref[0] = carry_ref[...].astype(jnp.int32)
    bstart_ref[0] = lstart.astype(jnp.int32)
    carry_ref[...] = carry_ref[...] + cnt


def _rows(ref, row, n):
    return ref.at[pl.ds(pl.multiple_of(row * SUBLANES, SUBLANES), n * SUBLANES)]


def _range_copies(n, near_ref, near_row, far_hbm, far_row, sem, to_far, wait=False, same_near=False):
    p = ROUTE_BLOCK
    while p >= 1:
        done = jnp.bitwise_and(n, -2 * p)

        @pl.when(jnp.bitwise_and(n, p) != 0)
        def _(p=p, done=done):
            near = _rows(near_ref, near_row if same_near else near_row + done, p)
            far = _rows(far_hbm, far_row + done, p)
            copy = pltpu.make_async_copy(near, far, sem) if to_far else pltpu.make_async_copy(far, near, sem)
            copy.wait() if wait else copy.start()

        p //= 2


def _stage_half(stage_ref, s):
    return _rows(stage_ref, s * PAIRS_PER_BLOCK, PAIRS_PER_BLOCK)


def _tile_at(ref, line):
    return ref.at[pl.ds(pl.multiple_of(line, SUBLANES), SUBLANES)]


def _dispatch_kernel(cnt_ref, lstart_ref, gstart_ref, fill_ref, lpos_ref, xg_ref, xs_hbm,
                     stage_ref, zero_ref, sem, zsem):
    b = pl.program_id(0)
    last = pl.num_programs(0) - 1
    slot = jnp.bitwise_and(b, 1)

    def wait_half(s):
        pltpu.make_async_copy(_stage_half(stage_ref, s), _rows(xs_hbm, 0, PAIRS_PER_BLOCK), sem.at[s]).wait()

    @pl.when(b == 0)
    def _():
        zero_ref[...] = jnp.zeros_like(zero_ref)

        def zfill(e, wait):
            _range_copies(fill_ref[N_EXPERTS + 1 + e], zero_ref, 0, xs_hbm, fill_ref[e], zsem, True,
                          wait=wait, same_near=True)

        def zfill_start(e, carry):
            zfill(e, False)
            return carry

        def zfill_wait(e, carry):
            zfill(e, True)
            return carry

        lax.fori_loop(0, N_EXPERTS, zfill_start, 0)
        lax.fori_loop(0, N_EXPERTS, zfill_wait, 0)

        def tail_copy(i):
            return pltpu.make_async_copy(zero_ref, _rows(xs_hbm, i * EXPERT_TILE, EXPERT_TILE), zsem)

        def tail_start(i, carry):
            tail_copy(i).start()
            return carry

        def tail_wait(i, carry):
            tail_copy(i).wait()
            return carry

        n_all = xs_hbm.shape[0] // TILE_LINES
        lax.fori_loop(fill_ref[N_EXPERTS], n_all, tail_start, 0)
        lax.fori_loop(fill_ref[N_EXPERTS], n_all, tail_wait, 0)

    lines_of = [lpos_ref.at[pl.ds(k * ROUTE_BLOCK, ROUTE_BLOCK)] for k in range(TOP_K)]

    def fill(c, carry):
        for u in range(ROW_UNROLL):
            t = c * ROW_UNROLL + u
            tile = _tile_at(xg_ref, t * SUBLANES)[...]
            for k in range(TOP_K):
                _tile_at(stage_ref, lines_of[k][t])[...] = tile
        return carry

    lax.fori_loop(0, ROUTE_BLOCK // ROW_UNROLL, fill, 0)

    def ranges(e, carry):
        j = b * N_EXPERTS + e
        _range_copies(cnt_ref[j], stage_ref, slot * PAIRS_PER_BLOCK + lstart_ref[j], xs_hbm, gstart_ref[j],
                      sem.at[slot], True)
        return carry

    lax.fori_loop(0, N_EXPERTS, ranges, 0)

    @pl.when(b > 0)
    def _():
        wait_half(1 - slot)

    @pl.when(b == last)
    def _():
        wait_half(slot)


def _dispatch(xg, lpos, cnt, lstart, gstart, fill_from, n_slots):
    block_lines = ROUTE_BLOCK * SUBLANES
    nblk = xg.shape[0] // block_lines
    grid_spec = pltpu.PrefetchScalarGridSpec(
        num_scalar_prefetch=4,
        grid=(nblk,),
        in_specs=[pl.BlockSpec((PAIRS_PER_BLOCK,), lambda b, *_: (b,), memory_space=pltpu.SMEM),
                  pl.BlockSpec((block_lines, LANES), lambda b, *_: (b, 0))],
        out_specs=pl.BlockSpec(memory_space=pl.ANY),
        scratch_shapes=[pltpu.VMEM((2 * PAIRS_PER_BLOCK * SUBLANES, LANES), F32),
                        pltpu.VMEM((TILE_LINES, LANES), F32),
                        pltpu.SemaphoreType.DMA((2,)), pltpu.SemaphoreType.DMA(())],
    )
    return pl.pallas_call(
        _dispatch_kernel,
        grid_spec=grid_spec,
        out_shape=jax.ShapeDtypeStruct((n_slots * SUBLANES, LANES), F32),
        compiler_params=_cparams("arbitrary"),
        name="moe_dispatch",
    )(cnt, lstart, gstart, fill_from, lpos, xg)


def _expert_kernel(l, te_ref, nxt_ref, nu_ref, x_ref, wgu_hbm, bgu_ref, wd_hbm, bd_ref, o_ref,
                   wguf_ref, wdf_ref, wgub_ref, wdb_ref, sem):
    i = pl.program_id(0)

    def fetch(e):
        return (pltpu.make_async_copy(wgu_hbm.at[l, e], wguf_ref, sem.at[0]),
                pltpu.make_async_copy(wd_hbm.at[l, e], wdf_ref, sem.at[1]))

    @pl.when(i == 0)
    def _():
        for copy in fetch(te_ref[0]):
            copy.start()

    @pl.when(jnp.logical_or(i == 0, te_ref[i] != te_ref[jnp.maximum(i - 1, 0)]))
    def _():
        for copy in fetch(te_ref[i]):
            copy.wait()
        _cast_weight(wgub_ref, wguf_ref)
        _cast_weight(wdb_ref, wdf_ref)

        @pl.when(nxt_ref[i] >= 0)
        def _():
            for copy in fetch(nxt_ref[i]):
                copy.start()

    @pl.when(i < nu_ref[0])
    def _():
        x = _from_token_tiles(x_ref, EXPERT_TILE).astype(BF16)
        hgu = _dot(x, wgub_ref[...]) + bgu_ref[0, 0]
        gate = jnp.minimum(hgu[:, :D_EXPERT], SWIGLU_LIMIT)
        up = jnp.clip(hgu[:, D_EXPERT:], -SWIGLU_LIMIT, SWIGLU_LIMIT)
        glu = gate * _sigmoid(SWIGLU_ALPHA * gate)
        _to_token_tiles(o_ref, _dot(((up + 1.0) * glu).astype(BF16), wdb_ref[...]) + bd_ref[0, 0])

    @pl.when(i >= nu_ref[0])
    def _():
        o_ref[...] = jnp.zeros_like(o_ref)


def _experts(xs, tile_expert, n_used, n_tiles, l, w):
    d = D_MODEL

    def of_expert(shape):
        return pl.BlockSpec((1, 1) + shape, lambda i, te, nxt, nu: (l, te[i], 0, 0))

    later = jnp.where(tile_expert[None, :] > tile_expert[:, None], tile_expert[None, :], N_EXPERTS)
    next_expert = jnp.min(later, axis=1)
    next_expert = jnp.where(next_expert == N_EXPERTS, -1, next_expert).astype(jnp.int32)

    grid_spec = pltpu.PrefetchScalarGridSpec(
        num_scalar_prefetch=3,
        grid=(n_tiles,),
        in_specs=[
            pl.BlockSpec((TILE_LINES, LANES), lambda i, te, nxt, nu: (jnp.minimum(i, nu[0] - 1), 0)),
            pl.BlockSpec(memory_space=pl.ANY), of_expert((1, 2 * D_EXPERT)),
            pl.BlockSpec(memory_space=pl.ANY), of_expert((1, d)),
        ],
        out_specs=pl.BlockSpec((TILE_LINES, LANES), lambda i, te, nxt, nu: (i, 0)),
        scratch_shapes=[pltpu.VMEM((d, 2 * D_EXPERT), F32), pltpu.VMEM((D_EXPERT, d), F32),
                        pltpu.VMEM((d, 2 * D_EXPERT), BF16), pltpu.VMEM((D_EXPERT, d), BF16),
                        pltpu.SemaphoreType.DMA((2,))],
    )
    return pl.pallas_call(
        functools.partial(_expert_kernel, l),
        grid_spec=grid_spec,
        out_shape=jax.ShapeDtypeStruct((n_tiles * TILE_LINES, LANES), F32),
        compiler_params=_cparams("arbitrary"),
        name="moe_experts",
    )(tile_expert, next_expert, n_used, xs, w["w_gu"], w["b_gu"], w["w_down"], w["b_down"])


def _combine_kernel(cnt_ref, lstart_ref, gstart_ref, lpos_ref, gate_ref, ys_hbm, x_ref, g_ref, b_ref,
                    of_ref, ob_ref, stage_ref, acc_ref, sem):
    b = pl.program_id(0)
    last = pl.num_programs(0) - 1
    slot = jnp.bitwise_and(b, 1)

    def fetch(blk, s):
        def ranges(e, carry):
            j = blk * N_EXPERTS + e
            _range_copies(cnt_ref[j], stage_ref, s * PAIRS_PER_BLOCK + lstart_ref[j], ys_hbm, gstart_ref[j],
                          sem.at[s], False)
            return carry

        lax.fori_loop(0, N_EXPERTS, ranges, 0)

    @pl.when(b == 0)
    def _():
        fetch(b, slot)

    @pl.when(b < last)
    def _():
        fetch(b + 1, 1 - slot)

    pltpu.make_async_copy(_rows(ys_hbm, 0, PAIRS_PER_BLOCK), _stage_half(stage_ref, slot), sem.at[slot]).wait()

    lines_of = [lpos_ref.at[pl.ds(k * ROUTE_BLOCK, ROUTE_BLOCK)] for k in range(TOP_K)]
    gates_of = [gate_ref.at[pl.ds(k * ROUTE_BLOCK, ROUTE_BLOCK)] for k in range(TOP_K)]

    def gather(c, carry):
        for u in range(ROW_UNROLL):
            t = c * ROW_UNROLL + u
            tile = jnp.zeros((SUBLANES, LANES), F32)
            for k in range(TOP_K):
                tile = tile + gates_of[k][t] * _tile_at(stage_ref, lines_of[k][t])[...]
            _tile_at(acc_ref, t * SUBLANES)[...] = tile
        return carry

    lax.fori_loop(0, ROUTE_BLOCK // ROW_UNROLL, gather, 0)

    y = _layer_norm(DN_ALPHA * x_ref[...] + _from_token_tiles(acc_ref, ROUTE_BLOCK), g_ref[...], b_ref[...])
    of_ref[...] = y
    ob_ref[...] = y.astype(BF16)


def _combine(ys, lpos, gates, cnt, lstart, gstart, xf, p):
    t, d = xf.shape
    nblk = t // ROUTE_BLOCK
    row = pl.BlockSpec((ROUTE_BLOCK, d), lambda b, *_: (b, 0))
    vec = pl.BlockSpec((1, d), lambda b, *_: (0, 0))
    pairs = pl.BlockSpec((PAIRS_PER_BLOCK,), lambda b, *_: (b,), memory_space=pltpu.SMEM)
    grid_spec = pltpu.PrefetchScalarGridSpec(
        num_scalar_prefetch=3,
        grid=(nblk,),
        in_specs=[pairs, pairs, pl.BlockSpec(memory_space=pl.ANY), row, vec, vec],
        out_specs=[row, row],
        scratch_shapes=[pltpu.VMEM((2 * PAIRS_PER_BLOCK * SUBLANES, LANES), F32),
                        pltpu.VMEM((ROUTE_BLOCK * SUBLANES, LANES), F32), pltpu.SemaphoreType.DMA((2,))],
    )
    return pl.pallas_call(
        _combine_kernel,
        grid_spec=grid_spec,
        out_shape=[jax.ShapeDtypeStruct((t, d), F32), jax.ShapeDtypeStruct((t, d), BF16)],
        compiler_params=_cparams("arbitrary"),
        name="moe_combine",
    )(cnt, lstart, gstart, lpos, gates, ys, xf, p["ln_g2"], p["ln_b2"])


def _moe(xf, xg, routing, l, w, p):
    t, d = xf.shape
    gates, lpos, bcnt, bbase, bstart = routing
    bcnt = bcnt[:, 0, :N_EXPERTS]
    counts = jnp.sum(bcnt, axis=0)
    padded = (counts + EXPERT_TILE - 1) // EXPERT_TILE * EXPERT_TILE
    ends = jnp.cumsum(padded)
    starts = ends - padded
    gstart = (starts[None, :] + bbase[:, 0, :N_EXPERTS]).reshape(-1)
    lstart = bstart[:, 0, :N_EXPERTS].reshape(-1)
    n_tiles = (t * TOP_K) // EXPERT_TILE + N_EXPERTS
    n_used = ends[-1] // EXPERT_TILE
    tile_start = jnp.minimum(jnp.arange(n_tiles, dtype=jnp.int32), n_used - 1) * EXPERT_TILE
    tile_expert = jnp.sum((ends[None, :] <= tile_start[:, None]).astype(jnp.int32), axis=1)

    fill_from = jnp.concatenate([starts + counts, n_used.reshape(1), padded - counts])
    n_slots = n_tiles * EXPERT_TILE
    def k_major(v):
        return v.reshape(-1, SUBLANES, ROUTE_BLOCK)[:, :TOP_K].reshape(-1)

    lpos, gates = k_major(lpos), k_major(gates)
    xs = _dispatch(xg, lpos, bcnt.reshape(-1), lstart, gstart, fill_from, n_slots)
    ys = _experts(xs, tile_expert, n_used.reshape(1), n_tiles, l, w)
    return _combine(ys, lpos, gates, bcnt.reshape(-1), lstart, gstart, xf, p)


def _layer_params(l, w_in, conv_w, conv_b, dt_bias, a_log, d_skip, ssd_norm_w, gm_ln_g, gm_ln_b,
                  w_sp, b_sp, w_router, b_router, ln_g, ln_b):
    d = D_MODEL
    off_dt = d + conv_w.shape[-1]
    off_u = off_dt + SSD_HEADS
    pad_h = LANES - SSD_HEADS
    pad_e = LANES - N_EXPERTS
    head_of_channel = jnp.arange(d, dtype=jnp.int32) // SSD_HEAD_DIM
    shift_row = jnp.arange((CONV_K - 1) * CHUNK, dtype=jnp.int32)
    shift_col = CONV_TAIL + shift_row % CHUNK - (1 + shift_row // CHUNK)
    return {
        "conv_shift": (jnp.arange(CONV_TAIL + CHUNK, dtype=jnp.int32)[None, :] == shift_col[:, None]).astype(BF16),
        "w_dt": jnp.pad(w_in[l, :, off_dt:off_u], ((0, 0), (0, pad_h))).astype(BF16),
        "conv_w": conv_w[l], "conv_b": conv_b[l].reshape(1, -1),
        "dt_bias": jnp.pad(dt_bias[l], (0, pad_h)).reshape(1, LANES),
        "a_log": jnp.pad(a_log[l], (0, pad_h)).reshape(1, LANES),
        "d_skip": d_skip[l][head_of_channel].reshape(1, d),
        "ssd_norm_w": ssd_norm_w[l].reshape(1, d),
        "gm_ln_g": gm_ln_g[l].reshape(1, d), "gm_ln_b": gm_ln_b[l].reshape(1, d),
        "w_sp": w_sp[l],
        "b_sp": jnp.repeat(b_sp[l].T, GM_GROUP_DIM, axis=1),
        "head_expand": (jnp.arange(LANES, dtype=jnp.int32)[:, None] == head_of_channel[None, :]).astype(BF16),
        "w_router": jnp.pad(w_router[l], ((0, 0), (0, pad_e))).astype(BF16),
        "b_router": jnp.pad(b_router[l], (0, pad_e), constant_values=NEG_BIG).reshape(1, LANES),
        "ln_g0": ln_g[l, 0].reshape(1, d), "ln_b0": ln_b[l, 0].reshape(1, d),
        "ln_g1": ln_g[l, 1].reshape(1, d), "ln_b1": ln_b[l, 1].reshape(1, d),
        "ln_g2": ln_g[l, 2].reshape(1, d), "ln_b2": ln_b[l, 2].reshape(1, d),
    }


def kernel(x, mem, ln0_g, ln0_b, w_in, conv_w, conv_b, dt_bias, a_log, d_skip, ssd_norm_w, gm_ln_g, gm_ln_b, w_sp, b_sp, p_ssd, p_gm, w_out, wq, wk, wv, wo, w_router, b_router, w_gu, b_gu, w_down, b_down, ln_g, ln_b):
    batch, seq, d = x.shape
    mem_len = mem.shape[1]
    depth = w_in.shape[0]
    assert d == D_MODEL and seq % ROW_BLOCK == 0 and seq % CHUNK == 0
    t = batch * seq
    memb = mem.reshape(batch * mem_len, d)
    off_u = d + conv_w.shape[-1] + SSD_HEADS
    w = {"w_in_tail": w_in[:, :, off_u:],
         "w_kv": jnp.concatenate([wk, wv], axis=2),
         "p_ssd": p_ssd, "p_gm": p_gm, "w_out": w_out, "wq": wq, "wo": wo,
         "w_gu": w_gu, "b_gu": b_gu.reshape(depth, N_EXPERTS, 1, -1),
         "w_down": w_down, "b_down": b_down.reshape(depth, N_EXPERTS, 1, -1)}
    xf, xb = _entry_ln(x.reshape(t, d), ln0_g, ln0_b)
    for l in range(depth):
        p = _layer_params(l, w_in, conv_w, conv_b, dt_bias, a_log, d_skip, ssd_norm_w, gm_ln_g,
                          gm_ln_b, w_sp, b_sp, w_router, b_router, ln_g, ln_b)
        bm = min(MM_BLOCK_M, t)
        zxa = _matmul(xb, w_in, l, 3 * d, bm)
        zxb = _matmul(xb, w["w_in_tail"], l, 4 * d, bm)
        y_ssd, y_gm = _mixer(zxa, zxb, xb, p, batch, seq)
        xf, xb = _merge(y_ssd, y_gm, zxb, xf, l, w, p)
        kv = _matmul(memb, w["w_kv"], l, 2 * d, min(MM_BLOCK_M, batch * mem_len))
        xf, xb, xg, *routing = _cross_attn(xb, xf, kv, l, w, p, batch, seq, mem_len)
        xf, xb = _moe(xf, xg, routing, l, w, p)
    return xf.reshape(batch, seq, d)
```

```python
import functools
import math

import jax
import jax.numpy as jnp
from jax import lax
from jax.experimental import pallas as pl
from jax.experimental.pallas import tpu as pltpu

F32 = jnp.float32
BF16 = jnp.bfloat16

D_MODEL = 1024
DEPTH = 2
CHUNK = 128
SSD_HEADS = 16
SSD_HEAD_DIM = 64
SSD_GROUPS = 4
SSD_HPG = SSD_HEADS // SSD_GROUPS
SSD_STATE = 128
SSD_GROUP_W = SSD_HPG * SSD_HEAD_DIM
CONV_K = 4
CONV_TAIL = CHUNK
GM_GROUPS = 8
GM_GROUP_DIM = D_MODEL // GM_GROUPS
XA_HEADS = 4
XA_HEAD_DIM = D_MODEL // XA_HEADS
N_EXPERTS = 32
TOP_K = 4
D_EXPERT = D_MODEL
SWIGLU_LIMIT = 7.0
SWIGLU_ALPHA = 1.702
DN_ALPHA = (2 * DEPTH) ** 0.25
EPS = 1e-5

LANES = 128
SUBLANES = 8
VMEM_LIMIT = 56 * 1024 * 1024

ROW_BLOCK = 512
MM_BLOCK_M = 2048
MM_BLOCK_N = 1024
CAST_ROWS = 128
ROUTE_BLOCK = 512
EXPERT_TILE = 512
NEG_BIG = -1e30
TILE_LINES = EXPERT_TILE * SUBLANES
PAIRS_PER_BLOCK = ROUTE_BLOCK * TOP_K
ROW_UNROLL = 8


def _cparams(*sem):
    return pltpu.CompilerParams(dimension_semantics=sem, vmem_limit_bytes=VMEM_LIMIT)


def _layer_norm(x, g, b):
    mu = jnp.mean(x, axis=-1, keepdims=True)
    xc = x - mu
    var = jnp.mean(xc * xc, axis=-1, keepdims=True)
    return xc * lax.rsqrt(var + EPS) * g + b


def _dot(a, b):
    return jnp.dot(a, b, preferred_element_type=F32)


def _dot_nt(a, b):
    return lax.dot_general(a, b, (((1,), (1,)), ((), ())), preferred_element_type=F32)


def _dot_tn(a, b):
    return lax.dot_general(a, b, (((0,), (0,)), ((), ())), preferred_element_type=F32)


def _split3(v):
    hi = v.astype(BF16)
    r1 = v - hi.astype(F32)
    mid = r1.astype(BF16)
    lo = (r1 - mid.astype(F32)).astype(BF16)
    return hi, mid, lo


def _dot_exact_rhs(sel, v):
    hi, mid, lo = _split3(v)
    return _dot(sel, hi) + _dot(sel, mid) + _dot(sel, lo)


def _dot_exact_lhs(v, sel):
    hi, mid, lo = _split3(v)
    return _dot(hi, sel) + _dot(mid, sel) + _dot(lo, sel)


def _dot_select_lhs(v, sel):
    hi = v.astype(BF16)
    lo = (v - hi.astype(F32)).astype(BF16)
    return _dot(hi, sel) + _dot(lo, sel)


def _sigmoid(x):
    return 1.0 / (1.0 + jnp.exp(-x))


def _gelu(x):
    return 0.5 * x * (1.0 + lax.erf(x * math.sqrt(0.5)))


def _softplus(x):
    return jnp.maximum(x, 0.0) + jnp.log1p(jnp.exp(-jnp.abs(x)))


def _cast_weight(dst_ref, src_ref):
    def body(c, carry):
        rows = pl.ds(pl.multiple_of(c * CAST_ROWS, CAST_ROWS), CAST_ROWS)
        dst_ref[rows, :] = src_ref[rows, :].astype(BF16)
        return carry

    lax.fori_loop(0, src_ref.shape[0] // CAST_ROWS, body, 0)


def _layer_weight(l, k, n):
    return pl.BlockSpec((1, k, n), lambda *_: (l, 0, 0), pipeline_mode=pl.Buffered(1))


def _ln_kernel(x_ref, g_ref, b_ref, of_ref, ob_ref):
    y = _layer_norm(x_ref[...], g_ref[...], b_ref[...])
    of_ref[...] = y
    ob_ref[...] = y.astype(BF16)


def _entry_ln(x, g, b):
    t, d = x.shape
    row = pl.BlockSpec((ROW_BLOCK, d), lambda i: (i, 0))
    vec = pl.BlockSpec((1, d), lambda i: (0, 0))
    return pl.pallas_call(
        _ln_kernel,
        grid=(t // ROW_BLOCK,),
        in_specs=[row, vec, vec],
        out_specs=[row, row],
        out_shape=[jax.ShapeDtypeStruct((t, d), F32), jax.ShapeDtypeStruct((t, d), BF16)],
        compiler_params=_cparams("arbitrary"),
        name="entry_ln",
    )(x, g.reshape(1, d), b.reshape(1, d))


def _mm_kernel(a_ref, w_ref, o_ref, wb_ref):
    @pl.when(pl.program_id(1) == 0)
    def _():
        _cast_weight(wb_ref, w_ref.at[0])

    o_ref[...] = _dot(a_ref[...].astype(BF16), wb_ref[...]).astype(o_ref.dtype)


def _matmul(a, w, l, n, bm):
    m, k = a.shape
    bn = MM_BLOCK_N
    return pl.pallas_call(
        _mm_kernel,
        grid=(n // bn, m // bm),
        in_specs=[pl.BlockSpec((bm, k), lambda j, i: (i, 0)),
                  pl.BlockSpec((1, k, bn), lambda j, i: (l, 0, j))],
        out_specs=pl.BlockSpec((bm, bn), lambda j, i: (i, j)),
        out_shape=jax.ShapeDtypeStruct((m, n), BF16),
        scratch_shapes=[pltpu.VMEM((k, bn), BF16)],
        compiler_params=_cparams("arbitrary", "arbitrary"),
        name="matmul",
    )(a, w)


def _conv_silu(raw_ref, tail_ref, cols, shift, w, b):
    raw = raw_ref[:, cols]
    aug = jnp.concatenate([tail_ref[:, cols], raw], axis=0)
    shifted = _dot(shift, aug)
    acc = raw.astype(F32) * w[CONV_K - 1:CONV_K] + b
    for j in range(1, CONV_K):
        acc = acc + shifted[(j - 1) * CHUNK:j * CHUNK] * w[CONV_K - 1 - j:CONV_K - j]
    tail_ref[:, cols] = raw[CHUNK - CONV_TAIL:CHUNK]
    return acc * _sigmoid(acc)


def _mixer_kernel(z_ref, xs_ref, bc_ref, u_ref, v_ref, xb_ref, wdt_ref, cw_ref, cb_ref, dtb_ref,
                  alog_ref, dskip_ref, nw_ref, lng_ref, lnb_ref, wsp_ref, bsp_ref, hexp_ref, shift_ref,
                  yssd_ref, ygm_ref, tailx_ref, tailbc_ref, state_ref):
    @pl.when(pl.program_id(1) == 0)
    def _():
        tailx_ref[...] = jnp.zeros_like(tailx_ref)
        tailbc_ref[...] = jnp.zeros_like(tailbc_ref)
        state_ref[...] = jnp.zeros_like(state_ref)

    row = lax.broadcasted_iota(jnp.int32, (CHUNK, CHUNK), 0)
    col = lax.broadcasted_iota(jnp.int32, (CHUNK, CHUNK), 1)
    causal = col <= row
    tri = jnp.where(causal, 1.0, 0.0).astype(BF16)

    cw = cw_ref[...]
    cb = cb_ref[...]
    shift = shift_ref[...]
    bc = _conv_silu(bc_ref, tailbc_ref, slice(0, D_MODEL), shift, cw[:, D_MODEL:], cb[:, D_MODEL:])
    gn = SSD_GROUPS * SSD_STATE

    dt = _softplus(_dot(xb_ref[...], wdt_ref[...]) + dtb_ref[...])
    a = -jnp.exp(alog_ref[...])
    cs = _dot_exact_rhs(tri, dt * a)
    cs_t = cs.T
    seg = lax.shift_right_logical(lax.broadcasted_iota(jnp.int32, (CHUNK, SSD_GROUP_W), 1),
                                  int(math.log2(SSD_HEAD_DIM)))

    for g in range(SSD_GROUPS):
        cols = slice(g * SSD_GROUP_W, (g + 1) * SSD_GROUP_W)
        xs = _conv_silu(xs_ref, tailx_ref, cols, shift, cw[:, cols], cb[:, cols])
        hexp = hexp_ref[:, cols]
        dt_x = _dot_select_lhs(dt, hexp)
        cs_x = _dot_select_lhs(cs, hexp)
        tot_x = cs_x[CHUNK - 1:CHUNK]
        xdt = xs * dt_x
        xdt_b = xdt.astype(BF16)
        xdec_b = (xdt * jnp.exp(tot_x - cs_x)).astype(BF16)

        b_g = bc[:, g * SSD_STATE:(g + 1) * SSD_STATE].astype(BF16)
        c_g = bc[:, gn + g * SSD_STATE:gn + (g + 1) * SSD_STATE].astype(BF16)
        cb_g = _dot_nt(c_g, b_g)
        st = state_ref[:, cols]
        y_g = _dot(c_g, st.astype(BF16)) * jnp.exp(cs_x)
        for hh in range(SSD_HPG):
            h = g * SSD_HPG + hh
            diff = cs[:, h:h + 1] - cs_t[h:h + 1, :]
            m_h = (cb_g * jnp.exp(jnp.where(causal, diff, -jnp.inf))).astype(BF16)
            y_g = y_g + jnp.where(seg == hh, _dot(m_h, xdt_b), 0.0)
        state_ref[:, cols] = st * jnp.exp(tot_x) + _dot_tn(b_g, xdec_b)

        z = z_ref[:, cols].astype(F32)
        y_g = (y_g + dskip_ref[:, cols] * xs) * (z * _sigmoid(z))
        y_g = y_g * lax.rsqrt(jnp.mean(y_g * y_g, axis=-1, keepdims=True) + EPS)
        yssd_ref[:, cols] = (y_g * nw_ref[:, cols]).astype(BF16)

    u = _gelu(u_ref[...].astype(F32))
    v = _layer_norm(_gelu(v_ref[...].astype(F32)), lng_ref[...], lnb_ref[...]).astype(BF16)
    sv_parts = []
    for g in range(GM_GROUPS):
        w_g = jnp.where(causal, wsp_ref[g], 0.0).astype(BF16)
        sv_parts.append(_dot(w_g, v[:, g * GM_GROUP_DIM:(g + 1) * GM_GROUP_DIM]))
    ygm_ref[...] = (u * (jnp.concatenate(sv_parts, axis=1) + bsp_ref[...])).astype(BF16)


def _mixer(zxa, zxb, xb, p, batch, seq):
    t, d = xb.shape
    nchunk = seq // CHUNK

    def blk(j):
        return pl.BlockSpec((CHUNK, d), lambda b, c, j=j: (b * nchunk + c, j))

    def const(shape):
        return pl.BlockSpec(shape, lambda b, c: (0,) * len(shape))

    out = pl.BlockSpec((CHUNK, d), lambda b, c: (b * nchunk + c, 0))
    return pl.pallas_call(
        _mixer_kernel,
        grid=(batch, nchunk),
        in_specs=[blk(0), blk(1), blk(2), blk(0), blk(1),
                  pl.BlockSpec((CHUNK, d), lambda b, c: (b * nchunk + c, 0)),
                  const((d, LANES)), const((CONV_K, 2 * d)), const((1, 2 * d)), const((1, LANES)),
                  const((1, LANES)), const((1, d)), const((1, d)), const((1, d)), const((1, d)),
                  const((GM_GROUPS, CHUNK, CHUNK)), const((CHUNK, d)), const((LANES, d)),
                  const(((CONV_K - 1) * CHUNK, CONV_TAIL + CHUNK))],
        out_specs=[out, out],
        out_shape=[jax.ShapeDtypeStruct((t, d), BF16), jax.ShapeDtypeStruct((t, d), BF16)],
        scratch_shapes=[pltpu.VMEM((CONV_TAIL, d), BF16), pltpu.VMEM((CONV_TAIL, d), BF16),
                        pltpu.VMEM((SSD_STATE, d), F32)],
        compiler_params=_cparams("arbitrary", "arbitrary"),
        name="mixer",
    )(zxa, zxa, zxa, zxb, zxb, xb, p["w_dt"], p["conv_w"], p["conv_b"], p["dt_bias"], p["a_log"],
      p["d_skip"], p["ssd_norm_w"], p["gm_ln_g"], p["gm_ln_b"], p["w_sp"], p["b_sp"], p["head_expand"],
      p["conv_shift"])


def _merge_kernel(ys_ref, yg_ref, gs_ref, gg_ref, x_ref, ps_ref, pg_ref, wo_ref, g_ref, b_ref,
                  of_ref, ob_ref, psb_ref, pgb_ref, wob_ref):
    @pl.when(pl.program_id(0) == 0)
    def _():
        _cast_weight(psb_ref, ps_ref.at[0])
        _cast_weight(pgb_ref, pg_ref.at[0])
        _cast_weight(wob_ref, wo_ref.at[0])

    h = (_sigmoid(gs_ref[...].astype(F32)) * _dot(ys_ref[...], psb_ref[...])
         + _sigmoid(gg_ref[...].astype(F32)) * _dot(yg_ref[...], pgb_ref[...]))
    mix = _dot(h.astype(BF16), wob_ref[...])
    y = _layer_norm(DN_ALPHA * x_ref[...] + mix, g_ref[...], b_ref[...])
    of_ref[...] = y
    ob_ref[...] = y.astype(BF16)


def _merge(y_ssd, y_gm, zxb, xf, l, w, p):
    t, d = xf.shape
    row = pl.BlockSpec((ROW_BLOCK, d), lambda i: (i, 0))
    mat = _layer_weight(l, d, d)
    vec = pl.BlockSpec((1, d), lambda i: (0, 0))
    return pl.pallas_call(
        _merge_kernel,
        grid=(t // ROW_BLOCK,),
        in_specs=[row, row, pl.BlockSpec((ROW_BLOCK, d), lambda i: (i, 2)),
                  pl.BlockSpec((ROW_BLOCK, d), lambda i: (i, 3)), row, mat, mat, mat, vec, vec],
        out_specs=[row, row],
        out_shape=[jax.ShapeDtypeStruct((t, d), F32), jax.ShapeDtypeStruct((t, d), BF16)],
        scratch_shapes=[pltpu.VMEM((d, d), BF16)] * 3,
        compiler_params=_cparams("arbitrary"),
        name="merge",
    )(y_ssd, y_gm, zxb, zxb, xf, w["p_ssd"], w["p_gm"], w["w_out"], p["ln_g0"], p["ln_b0"])


def _to_token_tiles(ref, y):
    m = y.shape[0]
    for j in range(SUBLANES):
        ref[pl.ds(j, m, stride=SUBLANES), :] = y[:, j * LANES:(j + 1) * LANES]


def _from_token_tiles(ref, m):
    return jnp.concatenate([ref[pl.ds(j, m, stride=SUBLANES), :] for j in range(SUBLANES)], axis=1)


def _attn_kernel(xb_ref, xf_ref, kv_ref, wq_ref, wo_ref, g_ref, b_ref, wr_ref, br_ref, upper_ref,
                 of_ref, ob_ref, og_ref, gate_ref, lpos_ref, bcnt_ref, bbase_ref, bstart_ref,
                 wqb_ref, wob_ref, carry_ref):
    @pl.when(jnp.logical_and(pl.program_id(0) == 0, pl.program_id(1) == 0))
    def _():
        _cast_weight(wqb_ref, wq_ref.at[0])
        _cast_weight(wob_ref, wo_ref.at[0])
        carry_ref[...] = jnp.zeros_like(carry_ref)

    q = _dot(xb_ref[...], wqb_ref[...]).astype(BF16)
    kv = kv_ref[...]
    outs = []
    for h in range(XA_HEADS):
        cols = slice(h * XA_HEAD_DIM, (h + 1) * XA_HEAD_DIM)
        s = _dot_nt(q[:, cols], kv[:, cols]) * (XA_HEAD_DIM ** -0.5)
        e = jnp.exp(s - jnp.max(s, axis=-1, keepdims=True))
        p = (e / jnp.sum(e, axis=-1, keepdims=True)).astype(BF16)
        outs.append(_dot(p, kv[:, D_MODEL + h * XA_HEAD_DIM:D_MODEL + (h + 1) * XA_HEAD_DIM]))
    o = jnp.concatenate(outs, axis=1).astype(BF16)
    y = _layer_norm(DN_ALPHA * xf_ref[...] + _dot(o, wob_ref[...]), g_ref[...], b_ref[...])
    yb = y.astype(BF16)
    of_ref[...] = y
    ob_ref[...] = yb
    _to_token_tiles(og_ref, y)
    blk = pl.program_id(0) * pl.num_programs(1) + pl.program_id(1)
    _route_block(yb, blk, wr_ref, br_ref, upper_ref, gate_ref, lpos_ref, bcnt_ref, bbase_ref, bstart_ref,
                 carry_ref)


def _cross_attn(xb, xf, kv, l, w, p, batch, seq, mem_len):
    assert ROUTE_BLOCK == ROW_BLOCK
    t, d = xf.shape
    nblk = seq // ROW_BLOCK
    nb = t // ROW_BLOCK
    row = pl.BlockSpec((ROW_BLOCK, d), lambda b, i: (b * nblk + i, 0))
    mat = _layer_weight(l, d, d)
    vec = pl.BlockSpec((1, d), lambda b, i: (0, 0))
    per_token = pl.BlockSpec((SUBLANES, ROUTE_BLOCK), lambda b, i: (b * nblk + i, 0))
    per_block = pl.BlockSpec((1, 1, LANES), lambda b, i: (b * nblk + i, 0, 0))
    per_block_shape = jax.ShapeDtypeStruct((nb, 1, LANES), jnp.int32)
    tok = jnp.arange(ROUTE_BLOCK, dtype=jnp.int32)
    earlier = (tok[:, None] < tok[None, :]).astype(BF16)
    return pl.pallas_call(
        _attn_kernel,
        grid=(batch, nblk),
        in_specs=[row, row, pl.BlockSpec((mem_len, 2 * d), lambda b, i: (b, 0)), mat, mat, vec, vec,
                  pl.BlockSpec((d, LANES), lambda b, i: (0, 0)), pl.BlockSpec((1, LANES), lambda b, i: (0, 0)),
                  pl.BlockSpec((ROUTE_BLOCK, ROUTE_BLOCK), lambda b, i: (0, 0))],
        out_specs=[row, row, pl.BlockSpec((ROW_BLOCK * SUBLANES, LANES), lambda b, i: (b * nblk + i, 0)),
                   per_token, per_token, per_block, per_block, per_block],
        out_shape=[jax.ShapeDtypeStruct((t, d), F32), jax.ShapeDtypeStruct((t, d), BF16),
                   jax.ShapeDtypeStruct((t * SUBLANES, LANES), F32),
                   jax.ShapeDtypeStruct((nb * SUBLANES, ROUTE_BLOCK), F32),
                   jax.ShapeDtypeStruct((nb * SUBLANES, ROUTE_BLOCK), jnp.int32),
                   per_block_shape, per_block_shape, per_block_shape],
        scratch_shapes=[pltpu.VMEM((d, d), BF16), pltpu.VMEM((d, d), BF16), pltpu.VMEM((1, LANES), F32)],
        compiler_params=_cparams("arbitrary", "arbitrary"),
        name="cross_attn",
    )(xb, xf, kv, w["wq"], w["wo"], p["ln_g1"], p["ln_b1"], p["w_router"], p["b_router"], earlier)


def _rows_from(rows, shape):
    sub = lax.broadcasted_iota(jnp.int32, shape, 0)
    out = jnp.zeros(shape, rows[0].dtype)
    for k, r in enumerate(rows):
        out = jnp.where(sub == k, r, out)
    return out


def _col_to_row(col):
    return jnp.broadcast_to(col, (LANES, LANES)).T[0:1]


def _row_to_col(row):
    return jnp.broadcast_to(row, (LANES, LANES)).T[:, 0:1]


def _route_block(xb, blk, wr_ref, br_ref, upper_ref, gate_ref, lpos_ref, bcnt_ref, bbase_ref, bstart_ref,
                 carry_ref):
    m = xb.shape[0]
    logits = _dot(xb, wr_ref[...]) + br_ref[...]
    lt = jnp.concatenate([logits[r:r + LANES].T for r in range(0, m, LANES)], axis=1)[:N_EXPERTS]
    sub = lax.broadcasted_iota(jnp.int32, lt.shape, 0)
    tops, hots = [], []
    for _ in range(TOP_K):
        top = jnp.max(lt, axis=0, keepdims=True)
        idx = jnp.min(jnp.where(lt == top, sub, N_EXPERTS), axis=0, keepdims=True)
        hot = sub == idx
        lt = jnp.where(hot, -jnp.inf, lt)
        tops.append(top)
        hots.append(hot)
    es = [jnp.exp(v - tops[0]) for v in tops]
    den = es[0] + es[1] + es[2] + es[3]

    hot_all = jnp.zeros(lt.shape, F32)
    for hot in hots:
        hot_all = hot_all + jnp.where(hot, 1.0, 0.0)
    cnt_col = jnp.sum(hot_all, axis=1, keepdims=True)
    cnt = _col_to_row(jnp.concatenate([cnt_col, jnp.zeros((LANES - N_EXPERTS, 1), F32)], axis=0))
    er = lax.broadcasted_iota(jnp.int32, (LANES, LANES), 0)
    ec = lax.broadcasted_iota(jnp.int32, (LANES, LANES), 1)
    lower_experts = jnp.where(er < ec, 1.0, 0.0).astype(BF16)
    lstart = _dot_exact_lhs(jnp.broadcast_to(cnt, (SUBLANES, LANES)), lower_experts)[0:1]
    local = _row_to_col(lstart)[:N_EXPERTS] + _dot(hot_all.astype(BF16), upper_ref[...])
    lpos = [jnp.sum(jnp.where(hot, local, 0.0), axis=0, keepdims=True) for hot in hots]

    half = jnp.bitwise_and(blk, 1).astype(F32) * PAIRS_PER_BLOCK
    lines = [(v + half) * SUBLANES for v in lpos]
    gate_ref[...] = _rows_from([e / den for e in es], gate_ref.shape)
    lpos_ref[...] = _rows_from(lines, lpos_ref.shape).astype(jnp.int32)
    bcnt_ref[0] = cnt.astype(jnp.int32)
    bbase_ref[0] = carry_ref[...].astype(jnp.int32)
    bstart_ref[0] = lstart.astype(jnp.int32)
    carry_ref[...] = carry_ref[...] + cnt


def _rows(ref, row, n):
    return ref.at[pl.ds(pl.multiple_of(row * SUBLANES, SUBLANES), n * SUBLANES)]


def _range_copies(n, near_ref, near_row, far_hbm, far_row, sem, to_far, wait=False, same_near=False):
    p = ROUTE_BLOCK
    while p >= 1:
        done = jnp.bitwise_and(n, -2 * p)

        @pl.when(jnp.bitwise_and(n, p) != 0)
        def _(p=p, done=done):
            near = _rows(near_ref, near_row if same_near else near_row + done, p)
            far = _rows(far_hbm, far_row + done, p)
            copy = pltpu.make_async_copy(near, far, sem) if to_far else pltpu.make_async_copy(far, near, sem)
            copy.wait() if wait else copy.start(priority=p.bit_length() % 2)

        p //= 2


def _stage_half(stage_ref, s):
    return _rows(stage_ref, s * PAIRS_PER_BLOCK, PAIRS_PER_BLOCK)


def _tile_at(ref, line):
    return ref.at[pl.ds(pl.multiple_of(line, SUBLANES), SUBLANES)]


def _dispatch_kernel(cnt_ref, lstart_ref, gstart_ref, fill_ref, lpos_ref, xg_ref, xs_hbm,
                     stage_ref, zero_ref, sem, zsem):
    b = pl.program_id(0)
    last = pl.num_programs(0) - 1
    slot = jnp.bitwise_and(b, 1)

    def wait_half(s):
        pltpu.make_async_copy(_stage_half(stage_ref, s), _rows(xs_hbm, 0, PAIRS_PER_BLOCK), sem.at[s]).wait()

    @pl.when(b == 0)
    def _():
        zero_ref[...] = jnp.zeros_like(zero_ref)

        def zfill(e, wait):
            _range_copies(fill_ref[N_EXPERTS + 1 + e], zero_ref, 0, xs_hbm, fill_ref[e], zsem, True,
                          wait=wait, same_near=True)

        def zfill_start(e, carry):
            zfill(e, False)
            return carry

        def zfill_wait(e, carry):
            zfill(e, True)
            return carry

        lax.fori_loop(0, N_EXPERTS, zfill_start, 0)
        lax.fori_loop(0, N_EXPERTS, zfill_wait, 0)

        def tail_copy(i):
            return pltpu.make_async_copy(zero_ref, _rows(xs_hbm, i * EXPERT_TILE, EXPERT_TILE), zsem)

        def tail_start(i, carry):
            tail_copy(i).start()
            return carry

        def tail_wait(i, carry):
            tail_copy(i).wait()
            return carry

        n_all = xs_hbm.shape[0] // TILE_LINES
        lax.fori_loop(fill_ref[N_EXPERTS], n_all, tail_start, 0)
        lax.fori_loop(fill_ref[N_EXPERTS], n_all, tail_wait, 0)

    lines_of = [lpos_ref.at[pl.ds(k * ROUTE_BLOCK, ROUTE_BLOCK)] for k in range(TOP_K)]

    def fill(c, carry):
        for u in range(ROW_UNROLL):
            t = c * ROW_UNROLL + u
            tile = _tile_at(xg_ref, t * SUBLANES)[...]
            for k in range(TOP_K):
                _tile_at(stage_ref, lines_of[k][t])[...] = tile
        return carry

    lax.fori_loop(0, ROUTE_BLOCK // ROW_UNROLL, fill, 0)

    def ranges(e, carry):
        j = b * N_EXPERTS + e
        _range_copies(cnt_ref[j], stage_ref, slot * PAIRS_PER_BLOCK + lstart_ref[j], xs_hbm, gstart_ref[j],
                      sem.at[slot], True)
        return carry

    lax.fori_loop(0, N_EXPERTS, ranges, 0)

    @pl.when(b > 0)
    def _():
        wait_half(1 - slot)

    @pl.when(b == last)
    def _():
        wait_half(slot)


def _dispatch(xg, lpos, cnt, lstart, gstart, fill_from, n_slots):
    block_lines = ROUTE_BLOCK * SUBLANES
    nblk = xg.shape[0] // block_lines
    grid_spec = pltpu.PrefetchScalarGridSpec(
        num_scalar_prefetch=4,
        grid=(nblk,),
        in_specs=[pl.BlockSpec((PAIRS_PER_BLOCK,), lambda b, *_: (b,), memory_space=pltpu.SMEM),
                  pl.BlockSpec((block_lines, LANES), lambda b, *_: (b, 0))],
        out_specs=pl.BlockSpec(memory_space=pl.ANY),
        scratch_shapes=[pltpu.VMEM((2 * PAIRS_PER_BLOCK * SUBLANES, LANES), F32),
                        pltpu.VMEM((TILE_LINES, LANES), F32),
                        pltpu.SemaphoreType.DMA((2,)), pltpu.SemaphoreType.DMA(())],
    )
    return pl.pallas_call(
        _dispatch_kernel,
        grid_spec=grid_spec,
        out_shape=jax.ShapeDtypeStruct((n_slots * SUBLANES, LANES), F32),
        compiler_params=_cparams("arbitrary"),
        name="moe_dispatch",
    )(cnt, lstart, gstart, fill_from, lpos, xg)


def _expert_kernel(l, te_ref, nxt_ref, nu_ref, x_ref, wgu_hbm, bgu_ref, wd_hbm, bd_ref, o_ref,
                   wguf_ref, wdf_ref, wgub_ref, wdb_ref, sem):
    i = pl.program_id(0)

    def fetch(e):
        return (pltpu.make_async_copy(wgu_hbm.at[l, e], wguf_ref, sem.at[0]),
                pltpu.make_async_copy(wd_hbm.at[l, e], wdf_ref, sem.at[1]))

    @pl.when(i == 0)
    def _():
        for copy in fetch(te_ref[0]):
            copy.start()

    @pl.when(jnp.logical_or(i == 0, te_ref[i] != te_ref[jnp.maximum(i - 1, 0)]))
    def _():
        for copy in fetch(te_ref[i]):
            copy.wait()
        _cast_weight(wgub_ref, wguf_ref)
        _cast_weight(wdb_ref, wdf_ref)

        @pl.when(nxt_ref[i] >= 0)
        def _():
            for copy in fetch(nxt_ref[i]):
                copy.start()

    @pl.when(i < nu_ref[0])
    def _():
        x = _from_token_tiles(x_ref, EXPERT_TILE).astype(BF16)
        hgu = _dot(x, wgub_ref[...]) + bgu_ref[0, 0]
        gate = jnp.minimum(hgu[:, :D_EXPERT], SWIGLU_LIMIT)
        up = jnp.clip(hgu[:, D_EXPERT:], -SWIGLU_LIMIT, SWIGLU_LIMIT)
        glu = gate * _sigmoid(SWIGLU_ALPHA * gate)
        _to_token_tiles(o_ref, _dot(((up + 1.0) * glu).astype(BF16), wdb_ref[...]) + bd_ref[0, 0])

    @pl.when(i >= nu_ref[0])
    def _():
        o_ref[...] = jnp.zeros_like(o_ref)


def _experts(xs, tile_expert, n_used, n_tiles, l, w):
    d = D_MODEL

    def of_expert(shape):
        return pl.BlockSpec((1, 1) + shape, lambda i, te, nxt, nu: (l, te[i], 0, 0))

    later = jnp.where(tile_expert[None, :] > tile_expert[:, None], tile_expert[None, :], N_EXPERTS)
    next_expert = jnp.min(later, axis=1)
    next_expert = jnp.where(next_expert == N_EXPERTS, -1, next_expert).astype(jnp.int32)

    grid_spec = pltpu.PrefetchScalarGridSpec(
        num_scalar_prefetch=3,
        grid=(n_tiles,),
        in_specs=[
            pl.BlockSpec((TILE_LINES, LANES), lambda i, te, nxt, nu: (jnp.minimum(i, nu[0] - 1), 0)),
            pl.BlockSpec(memory_space=pl.ANY), of_expert((1, 2 * D_EXPERT)),
            pl.BlockSpec(memory_space=pl.ANY), of_expert((1, d)),
        ],
        out_specs=pl.BlockSpec((TILE_LINES, LANES), lambda i, te, nxt, nu: (i, 0)),
        scratch_shapes=[pltpu.VMEM((d, 2 * D_EXPERT), F32), pltpu.VMEM((D_EXPERT, d), F32),
                        pltpu.VMEM((d, 2 * D_EXPERT), BF16), pltpu.VMEM((D_EXPERT, d), BF16),
                        pltpu.SemaphoreType.DMA((2,))],
    )
    return pl.pallas_call(
        functools.partial(_expert_kernel, l),
        grid_spec=grid_spec,
        out_shape=jax.ShapeDtypeStruct((n_tiles * TILE_LINES, LANES), F32),
        compiler_params=_cparams("arbitrary"),
        name="moe_experts",
    )(tile_expert, next_expert, n_used, xs, w["w_gu"], w["b_gu"], w["w_down"], w["b_down"])


def _combine_kernel(cnt_ref, lstart_ref, gstart_ref, lpos_ref, gate_ref, ys_hbm, x_ref, g_ref, b_ref,
                    of_ref, ob_ref, stage_ref, acc_ref, sem):
    b = pl.program_id(0)
    last = pl.num_programs(0) - 1
    slot = jnp.bitwise_and(b, 1)

    def fetch(blk, s):
        def ranges(e, carry):
            j = blk * N_EXPERTS + e
            _range_copies(cnt_ref[j], stage_ref, s * PAIRS_PER_BLOCK + lstart_ref[j], ys_hbm, gstart_ref[j],
                          sem.at[s], False)
            return carry

        lax.fori_loop(0, N_EXPERTS, ranges, 0)

    @pl.when(b == 0)
    def _():
        fetch(b, slot)

    @pl.when(b < last)
    def _():
        fetch(b + 1, 1 - slot)

    pltpu.make_async_copy(_rows(ys_hbm, 0, PAIRS_PER_BLOCK), _stage_half(stage_ref, slot), sem.at[slot]).wait()

    lines_of = [lpos_ref.at[pl.ds(k * ROUTE_BLOCK, ROUTE_BLOCK)] for k in range(TOP_K)]
    gates_of = [gate_ref.at[pl.ds(k * ROUTE_BLOCK, ROUTE_BLOCK)] for k in range(TOP_K)]

    def gather(c, carry):
        for u in range(ROW_UNROLL):
            t = c * ROW_UNROLL + u
            tile = jnp.zeros((SUBLANES, LANES), F32)
            for k in range(TOP_K):
                tile = tile + gates_of[k][t] * _tile_at(stage_ref, lines_of[k][t])[...]
            _tile_at(acc_ref, t * SUBLANES)[...] = tile
        return carry

    lax.fori_loop(0, ROUTE_BLOCK // ROW_UNROLL, gather, 0)

    y = _layer_norm(DN_ALPHA * x_ref[...] + _from_token_tiles(acc_ref, ROUTE_BLOCK), g_ref[...], b_ref[...])
    of_ref[...] = y
    ob_ref[...] = y.astype(BF16)


def _combine(ys, lpos, gates, cnt, lstart, gstart, xf, p):
    t, d = xf.shape
    nblk = t // ROUTE_BLOCK
    row = pl.BlockSpec((ROUTE_BLOCK, d), lambda b, *_: (b, 0))
    vec = pl.BlockSpec((1, d), lambda b, *_: (0, 0))
    pairs = pl.BlockSpec((PAIRS_PER_BLOCK,), lambda b, *_: (b,), memory_space=pltpu.SMEM)
    grid_spec = pltpu.PrefetchScalarGridSpec(
        num_scalar_prefetch=3,
        grid=(nblk,),
        in_specs=[pairs, pairs, pl.BlockSpec(memory_space=pl.ANY), row, vec, vec],
        out_specs=[row, row],
        scratch_shapes=[pltpu.VMEM((2 * PAIRS_PER_BLOCK * SUBLANES, LANES), F32),
                        pltpu.VMEM((ROUTE_BLOCK * SUBLANES, LANES), F32), pltpu.SemaphoreType.DMA((2,))],
    )
    return pl.pallas_call(
        _combine_kernel,
        grid_spec=grid_spec,
        out_shape=[jax.ShapeDtypeStruct((t, d), F32), jax.ShapeDtypeStruct((t, d), BF16)],
        compiler_params=_cparams("arbitrary"),
        name="moe_combine",
    )(cnt, lstart, gstart, lpos, gates, ys, xf, p["ln_g2"], p["ln_b2"])


def _moe(xf, xg, routing, l, w, p):
    t, d = xf.shape
    gates, lpos, bcnt, bbase, bstart = routing
    bcnt = bcnt[:, 0, :N_EXPERTS]
    counts = jnp.sum(bcnt, axis=0)
    padded = (counts + EXPERT_TILE - 1) // EXPERT_TILE * EXPERT_TILE
    ends = jnp.cumsum(padded)
    starts = ends - padded
    gstart = (starts[None, :] + bbase[:, 0, :N_EXPERTS]).reshape(-1)
    lstart = bstart[:, 0, :N_EXPERTS].reshape(-1)
    n_tiles = (t * TOP_K) // EXPERT_TILE + N_EXPERTS
    n_used = ends[-1] // EXPERT_TILE
    tile_start = jnp.minimum(jnp.arange(n_tiles, dtype=jnp.int32), n_used - 1) * EXPERT_TILE
    tile_expert = jnp.sum((ends[None, :] <= tile_start[:, None]).astype(jnp.int32), axis=1)

    fill_from = jnp.concatenate([starts + counts, n_used.reshape(1), padded - counts])
    n_slots = n_tiles * EXPERT_TILE
    def k_major(v):
        return v.reshape(-1, SUBLANES, ROUTE_BLOCK)[:, :TOP_K].reshape(-1)

    lpos, gates = k_major(lpos), k_major(gates)
    xs = _dispatch(xg, lpos, bcnt.reshape(-1), lstart, gstart, fill_from, n_slots)
    ys = _experts(xs, tile_expert, n_used.reshape(1), n_tiles, l, w)
    return _combine(ys, lpos, gates, bcnt.reshape(-1), lstart, gstart, xf, p)


def _layer_params(l, w_in, conv_w, conv_b, dt_bias, a_log, d_skip, ssd_norm_w, gm_ln_g, gm_ln_b,
                  w_sp, b_sp, w_router, b_router, ln_g, ln_b):
    d = D_MODEL
    off_dt = d + conv_w.shape[-1]
    off_u = off_dt + SSD_HEADS
    pad_h = LANES - SSD_HEADS
    pad_e = LANES - N_EXPERTS
    head_of_channel = jnp.arange(d, dtype=jnp.int32) // SSD_HEAD_DIM
    shift_row = jnp.arange((CONV_K - 1) * CHUNK, dtype=jnp.int32)
    shift_col = CONV_TAIL + shift_row % CHUNK - (1 + shift_row // CHUNK)
    return {
        "conv_shift": (jnp.arange(CONV_TAIL + CHUNK, dtype=jnp.int32)[None, :] == shift_col[:, None]).astype(BF16),
        "w_dt": jnp.pad(w_in[l, :, off_dt:off_u], ((0, 0), (0, pad_h))).astype(BF16),
        "conv_w": conv_w[l], "conv_b": conv_b[l].reshape(1, -1),
        "dt_bias": jnp.pad(dt_bias[l], (0, pad_h)).reshape(1, LANES),
        "a_log": jnp.pad(a_log[l], (0, pad_h)).reshape(1, LANES),
        "d_skip": d_skip[l][head_of_channel].reshape(1, d),
        "ssd_norm_w": ssd_norm_w[l].reshape(1, d),
        "gm_ln_g": gm_ln_g[l].reshape(1, d), "gm_ln_b": gm_ln_b[l].reshape(1, d),
        "w_sp": w_sp[l],
        "b_sp": jnp.repeat(b_sp[l].T, GM_GROUP_DIM, axis=1),
        "head_expand": (jnp.arange(LANES, dtype=jnp.int32)[:, None] == head_of_channel[None, :]).astype(BF16),
        "w_router": jnp.pad(w_router[l], ((0, 0), (0, pad_e))).astype(BF16),
        "b_router": jnp.pad(b_router[l], (0, pad_e), constant_values=NEG_BIG).reshape(1, LANES),
        "ln_g0": ln_g[l, 0].reshape(1, d), "ln_b0": ln_b[l, 0].reshape(1, d),
        "ln_g1": ln_g[l, 1].reshape(1, d), "ln_b1": ln_b[l, 1].reshape(1, d),
        "ln_g2": ln_g[l, 2].reshape(1, d), "ln_b2": ln_b[l, 2].reshape(1, d),
    }


def kernel(x, mem, ln0_g, ln0_b, w_in, conv_w, conv_b, dt_bias, a_log, d_skip, ssd_norm_w, gm_ln_g, gm_ln_b, w_sp, b_sp, p_ssd, p_gm, w_out, wq, wk, wv, wo, w_router, b_router, w_gu, b_gu, w_down, b_down, ln_g, ln_b):
    batch, seq, d = x.shape
    mem_len = mem.shape[1]
    depth = w_in.shape[0]
    assert d == D_MODEL and seq % ROW_BLOCK == 0 and seq % CHUNK == 0
    t = batch * seq
    memb = mem.reshape(batch * mem_len, d)
    off_u = d + conv_w.shape[-1] + SSD_HEADS
    w = {"w_in_tail": w_in[:, :, off_u:],
         "w_kv": jnp.concatenate([wk, wv], axis=2),
         "p_ssd": p_ssd, "p_gm": p_gm, "w_out": w_out, "wq": wq, "wo": wo,
         "w_gu": w_gu, "b_gu": b_gu.reshape(depth, N_EXPERTS, 1, -1),
         "w_down": w_down, "b_down": b_down.reshape(depth, N_EXPERTS, 1, -1)}
    xf, xb = _entry_ln(x.reshape(t, d), ln0_g, ln0_b)
    for l in range(depth):
        p = _layer_params(l, w_in, conv_w, conv_b, dt_bias, a_log, d_skip, ssd_norm_w, gm_ln_g,
                          gm_ln_b, w_sp, b_sp, w_router, b_router, ln_g, ln_b)
        bm = min(MM_BLOCK_M, t)
        zxa = _matmul(xb, w_in, l, 3 * d, bm)
        zxb = _matmul(xb, w["w_in_tail"], l, 4 * d, bm)
        y_ssd, y_gm = _mixer(zxa, zxb, xb, p, batch, seq)
        xf, xb = _merge(y_ssd, y_gm, zxb, xf, l, w, p)
        kv = _matmul(memb, w["w_kv"], l, 2 * d, min(MM_BLOCK_M, batch * mem_len))
        xf, xb, xg, *routing = _cross_attn(xb, xf, kv, l, w, p, batch, seq, mem_len)
        xf, xb = _moe(xf, xg, routing, l, w, p)
    return xf.reshape(batch, seq, d)
```
